```python
import math
import jax
import jax.numpy as jnp
from jax import lax
import numpy as np

D_MODEL = 2048
BATCH = 4
SEQ = 4096
DEPTH = 2

GRID_W = 64
CTX_LEN = 256
ROPE_BASE = 10000.0
NORM_EPS = 1e-6

N_DIFF_HEADS = 8
DIFF_DH = 64
DIFF_WIDTH = N_DIFF_HEADS * 2 * DIFF_DH
Q_BLOCK = 128

SSD_INNER = 1024
SSD_HEAD_DIM = 64
SSD_HEADS = SSD_INNER // SSD_HEAD_DIM
SSD_GROUPS = 4
SSD_STATE = 128
SSD_CONV_W = 5
SSD_CHUNK = 128
SSD_XBC = SSD_INNER + 2 * SSD_GROUPS * SSD_STATE

WIN_HEADS = 16
WIN_KV_HEADS = 4
WIN_DH = 64
WINDOW = 128
WIN_BLOCK = 128
WIN_WIDTH = WIN_HEADS * WIN_DH
WIN_KV_WIDTH = WIN_KV_HEADS * WIN_DH

D_FF = 5632
FFN_CONV_W = 3

IN_SIZES = (DIFF_WIDTH, DIFF_WIDTH, DIFF_WIDTH,
            SSD_INNER, SSD_XBC, 2 * SSD_HEADS,
            WIN_WIDTH, WIN_KV_WIDTH, WIN_KV_WIDTH,
            3 * D_MODEL)
IN_WIDTH = sum(IN_SIZES)

kernel_name = 'hybrid_diff_ssd_window_convffn_trunk'


def split_cols(u, sizes):
    out, start = [], 0
    for s in sizes:
        out.append(u[..., start:start + s])
        start += s
    return out


def rmsnorm(x, w):
    x32 = x.astype(jnp.float32)
    y = x32 * lax.rsqrt(jnp.mean(x32 * x32, axis=-1, keepdims=True) + NORM_EPS)
    return (y * w.astype(jnp.float32)).astype(x.dtype)


def axial_rope(n, head_dim):
    rows = n // GRID_W
    row = jnp.repeat(jnp.arange(rows, dtype=jnp.float32), GRID_W)
    col = jnp.tile(jnp.arange(GRID_W, dtype=jnp.float32), rows)
    axis_dim = head_dim // 2
    inv = ROPE_BASE ** (-jnp.arange(0, axis_dim, 2, dtype=jnp.float32) / axis_dim)
    ang = jnp.concatenate([row[:, None] * inv, col[:, None] * inv], axis=-1)
    return jnp.cos(ang), jnp.sin(ang)


def apply_rope(x, cos, sin):
    shape = (cos.shape[0],) + (1,) * (x.ndim - 3) + (cos.shape[1],)
    cos, sin = cos.reshape(shape), sin.reshape(shape)
    x1, x2 = jnp.split(x.astype(jnp.float32), 2, axis=-1)
    return jnp.concatenate([x1 * cos - x2 * sin, x2 * cos + x1 * sin], axis=-1).astype(x.dtype)


def dwconv_centred(u, w, b):
    k = w.shape[0]
    p = k // 2
    length = u.shape[1]
    up = jnp.pad(u, ((0, 0), (p, p), (0, 0)))
    out = b
    for j in range(k):
        out = out + up[:, j:j + length] * w[j]
    return out


def _diff_attend(q, k, v, lam):
    s = jnp.einsum('bqhjd,bkhjd->bhjqk', q, k, preferred_element_type=jnp.float32) * (DIFF_DH ** -0.5)
    p = jax.nn.softmax(s, axis=-1)
    pd = p[:, :, 0] - lam * p[:, :, 1]
    return jnp.einsum('bhqk,bkhe->bqhe', pd.astype(v.dtype), v)


def diff_attention(q, k, v, qc, kc, vc, lam_p, norm_w, lam_init, cos, sin, need_ctx):
    b, n, _ = q.shape
    m = qc.shape[1]
    H, d = N_DIFF_HEADS, DIFF_DH
    q = apply_rope(q.reshape(b, n, H, 2, d), cos, sin)
    k = apply_rope(k.reshape(b, n, H, 2, d), cos, sin)
    v = v.reshape(b, n, H, 2 * d)
    qc = qc.reshape(b, m, H, 2, d)
    kc = kc.reshape(b, m, H, 2, d)
    vc = vc.reshape(b, m, H, 2 * d)
    lp = lam_p.astype(jnp.float32)
    lam = jnp.exp(jnp.sum(lp[0] * lp[1])) - jnp.exp(jnp.sum(lp[2] * lp[3])) + lam_init
    k_all = jnp.concatenate([k, kc], axis=1)
    v_all = jnp.concatenate([v, vc], axis=1)
    nb = n // Q_BLOCK
    q_blocks = jnp.moveaxis(q.reshape(b, nb, Q_BLOCK, H, 2, d), 1, 0)
    o = lax.map(lambda qb: _diff_attend(qb, k_all, v_all, lam), q_blocks)
    o = jnp.moveaxis(o, 0, 1).reshape(b, n, H, 2 * d)

    def finish(o_, length):
        return (rmsnorm(o_, norm_w) * (1.0 - lam_init)).reshape(b, length, H * 2 * d)

    y = finish(o, n)
    yc = finish(_diff_attend(qc, kc, vc, lam), m) if need_ctx else None
    return y, yc


def segsum(a):
    t = a.shape[-1]
    cs = jnp.cumsum(a, axis=-1)
    diff = cs[..., :, None] - cs[..., None, :]
    return jnp.where(jnp.tril(jnp.ones((t, t), dtype=bool)), diff, -jnp.inf)


def ssd_scan(x, dt, a, bm, cm, h0):
    b, l, h, p = x.shape
    g, n = bm.shape[2], bm.shape[3]
    r = h // g
    nc = l // SSD_CHUNK
    xs = (x.astype(jnp.float32) * dt[..., None]).reshape(b, nc, SSD_CHUNK, g, r, p)
    ad = jnp.moveaxis((dt * a).reshape(b, nc, SSD_CHUNK, g, r), (1, 2), (3, 4))
    bc = bm.astype(jnp.float32).reshape(b, nc, SSD_CHUNK, g, n)
    cc = cm.astype(jnp.float32).reshape(b, nc, SSD_CHUNK, g, n)
    a_cum = jnp.cumsum(ad, axis=-1)
    decay_in = jnp.exp(segsum(ad))
    cb = jnp.einsum('bclgn,bcsgn->bgcls', cc, bc)
    y_diag = jnp.einsum('bgcls,bgrcls,bcsgrp->bclgrp', cb, decay_in, xs)
    decay_states = jnp.exp(a_cum[..., -1:] - a_cum)
    states = jnp.einsum('bclgn,bgrcl,bclgrp->bcgrpn', bc, decay_states, xs)
    states = jnp.concatenate([h0.astype(jnp.float32).reshape(b, 1, g, r, p, n), states], axis=1)
    decay_chunk = jnp.exp(segsum(jnp.pad(a_cum[..., -1], ((0, 0), (0, 0), (0, 0), (1, 0)))))
    states = jnp.einsum('bgrzc,bcgrpn->bzgrpn', decay_chunk, states)
    y_off = jnp.einsum('bclgn,bcgrpn,bgrcl->bclgrp', cc, states[:, :-1], jnp.exp(a_cum))
    y = (y_diag + y_off).reshape(b, l, h, p)
    return y, states[:, -1].reshape(b, h, p, n)


def ssd_mixer(z, xbc, dtr, zc, xbcc, dtrc, conv_w, conv_b, a_log, dt_bias, d_skip, norm_w, need_ctx):
    def prep(xbc_, dtr_):
        u = jax.nn.silu(dwconv_centred(xbc_, conv_w, conv_b))
        bsz, length = u.shape[:2]
        xs, bm, cm = split_cols(u, (SSD_INNER, SSD_GROUPS * SSD_STATE, SSD_GROUPS * SSD_STATE))
        xs = xs.reshape(bsz, length, SSD_HEADS, SSD_HEAD_DIM)
        bm = bm.reshape(bsz, length, SSD_GROUPS, SSD_STATE)
        cm = cm.reshape(bsz, length, SSD_GROUPS, SSD_STATE)
        dt = jax.nn.softplus(dtr_.astype(jnp.float32).reshape(bsz, length, 2, SSD_HEADS)
                             + dt_bias.astype(jnp.float32))
        return xs, bm, cm, dt

    a = -jnp.exp(a_log.astype(jnp.float32))
    xs, bm, cm, dt = prep(xbc, dtr)
    xs_c, bm_c, cm_c, dt_c = prep(xbcc, dtrc)
    b = xs.shape[0]
    h0 = jnp.zeros((b, SSD_HEADS, SSD_HEAD_DIM, SSD_STATE), jnp.float32)
    y_lat, y_ctx = 0.0, 0.0
    for direction in range(2):
        orient = (lambda t: t) if direction == 0 else (lambda t: jnp.flip(t, axis=1))
        yc_d, state = ssd_scan(orient(xs_c), orient(dt_c[:, :, direction]), a[direction],
                               orient(bm_c), orient(cm_c), h0)
        y_d, _ = ssd_scan(orient(xs), orient(dt[:, :, direction]), a[direction], orient(bm), orient(cm), state)
        y_lat = y_lat + orient(y_d)
        y_ctx = y_ctx + orient(yc_d)

    def finish(y, xs_, z_):
        y = y + d_skip.astype(jnp.float32)[:, None] * xs_
        y = y.reshape(z_.shape).astype(z_.dtype)
        return rmsnorm(y * jax.nn.silu(z_), norm_w)

    out = finish(y_lat, xs, z)
    out_c = finish(y_ctx, xs_c, zc) if need_ctx else None
    return out, out_c


def window_attention(q, k, v, qc, kc, vc, sink, cos, sin, need_ctx):
    b, n, _ = q.shape
    m = qc.shape[1]
    KV, G, dh = WIN_KV_HEADS, WIN_HEADS // WIN_KV_HEADS, WIN_DH
    scale = dh ** -0.5
    q = apply_rope(q.reshape(b, n, KV, G, dh), cos, sin)
    k = apply_rope(k.reshape(b, n, KV, dh), cos, sin)
    v = v.reshape(b, n, KV, dh)
    qc = qc.reshape(b, m, KV, G, dh)
    kc = kc.reshape(b, m, KV, dh)
    vc = vc.reshape(b, m, KV, dh)
    sink = sink.astype(jnp.float32).reshape(KV, G)
    nb = n // WIN_BLOCK
    qb = q.reshape(b, nb, WIN_BLOCK, KV, G, dh)

    def band(t):
        tp = jnp.pad(t, ((0, 0), (WIN_BLOCK, WIN_BLOCK), (0, 0), (0, 0))).reshape(b, nb + 2, WIN_BLOCK, KV, dh)
        return jnp.concatenate([tp[:, :-2], tp[:, 1:-1], tp[:, 2:]], axis=2)

    kb, vb = band(k), band(v)
    rel = jnp.arange(3 * WIN_BLOCK)[None, :] - WIN_BLOCK - jnp.arange(WIN_BLOCK)[:, None]
    kpos = (jnp.arange(nb)[:, None] - 1) * WIN_BLOCK + jnp.arange(3 * WIN_BLOCK)[None, :]
    mask = (jnp.abs(rel) <= WINDOW)[None] & ((kpos >= 0) & (kpos < n))[:, None, :]
    s_win = jnp.einsum('bnqkgd,bnmkd->bnkgqm', qb, kb, preferred_element_type=jnp.float32) * scale
    s_win = jnp.where(mask[None, :, None, None], s_win, -jnp.inf)
    s_ctx = jnp.einsum('bnqkgd,bmkd->bnkgqm', qb, kc, preferred_element_type=jnp.float32) * scale
    s_sink = jnp.broadcast_to(sink[None, None, :, :, None, None], s_win.shape[:-1] + (1,))
    p = jax.nn.softmax(jnp.concatenate([s_win, s_ctx, s_sink], axis=-1), axis=-1)
    pw = p[..., :3 * WIN_BLOCK].astype(v.dtype)
    pc = p[..., 3 * WIN_BLOCK:3 * WIN_BLOCK + m].astype(v.dtype)
    o = jnp.einsum('bnkgqm,bnmkd->bnqkgd', pw, vb) + jnp.einsum('bnkgqm,bmkd->bnqkgd', pc, vc)
    y = o.reshape(b, n, WIN_WIDTH)
    yc = None
    if need_ctx:
        s = jnp.einsum('bqkgd,bmkd->bkgqm', qc, kc, preferred_element_type=jnp.float32) * scale
        s = jnp.concatenate([s, jnp.broadcast_to(sink[None, :, :, None, None], s.shape[:-1] + (1,))], axis=-1)
        pcc = jax.nn.softmax(s, axis=-1)[..., :m].astype(vc.dtype)
        yc = jnp.einsum('bkgqm,bmkd->bqkgd', pcc, vc).reshape(b, m, WIN_WIDTH)
    return y, yc


def token_mixer(h, hc, w_in, diff_lambda, diff_norm, lam_init, ssd_conv_w, ssd_conv_b, ssd_a_log, ssd_dt_bias,
                ssd_d, ssd_norm, win_sink, w_br_diff, w_br_ssd, w_br_win, w_out, rope_d, rope_w, need_ctx):
    dq, dk, dv, sz, sxbc, sdt, wq, wk, wv, gl = split_cols(h @ w_in, IN_SIZES)
    dqc, dkc, dvc, szc, sxbcc, sdtc, wqc, wkc, wvc, glc = split_cols(hc @ w_in, IN_SIZES)
    yd, ydc = diff_attention(dq, dk, dv, dqc, dkc, dvc, diff_lambda, diff_norm, lam_init, *rope_d, need_ctx)
    ys, ysc = ssd_mixer(sz, sxbc, sdt, szc, sxbcc, sdtc, ssd_conv_w, ssd_conv_b, ssd_a_log, ssd_dt_bias,
                        ssd_d, ssd_norm, need_ctx)
    yw, ywc = window_attention(wq, wk, wv, wqc, wkc, wvc, win_sink, *rope_w, need_ctx)

    def merge(gates, a_, s_, w_):
        gd, gs, gw = jnp.split(jax.nn.sigmoid(gates), 3, axis=-1)
        return (gd * (a_ @ w_br_diff) + gs * (s_ @ w_br_ssd) + gw * (w_ @ w_br_win)) @ w_out

    y = merge(gl, yd, ys, yw)
    yc = merge(glc, ydc, ysc, ywc) if need_ctx else None
    return y, yc


def conv_ffn(h, w_up, conv_w, conv_b, w_down):
    u = dwconv_centred(h @ w_up, conv_w, conv_b)
    a, g = jnp.split(u, 2, axis=-1)
    return (jax.nn.silu(a) * g) @ w_down


def setup_inputs(seed: int = 0) -> dict:
    key = jax.random.key(seed)
    ks = jax.random.split(key, 26)
    f32 = jnp.float32
    L, D = DEPTH, D_MODEL

    def normal(k, shape, scale):
        return jax.random.normal(k, shape, f32) * scale

    def gain(k, shape, scale=0.02):
        return 1.0 + scale * jax.random.normal(k, shape, f32)

    a_log = jnp.log(jax.random.uniform(ks[13], (L, 2, SSD_HEADS), f32, 1.0, 16.0))
    dt0 = jnp.exp(jax.random.uniform(ks[14], (L, 2, SSD_HEADS), f32, math.log(1e-3), math.log(1e-1)))
    dt_bias = dt0 + jnp.log(-jnp.expm1(-dt0))
    return {
        'x': normal(ks[0], (BATCH, SEQ, D), 1.0),
        'c': normal(ks[1], (BATCH, D), 1.0),
        'ctx': normal(ks[2], (BATCH, CTX_LEN, D), 1.0),
        'c_ctx': normal(ks[3], (D,), 1.0),
        'w_ada': normal(ks[4], (L, D, 6 * D), D ** -0.5),
        'b_ada': normal(ks[5], (L, 6 * D), 0.02),
        'norm_g': gain(ks[6], (L, 4, D)),
        'w_in': normal(ks[7], (L, D, IN_WIDTH), D ** -0.5),
        'diff_lambda': normal(ks[8], (L, 4, DIFF_DH), 0.1),
        'diff_norm': gain(ks[9], (L, 2 * DIFF_DH)),
        'ssd_conv_w': normal(ks[10], (L, SSD_CONV_W, SSD_XBC), SSD_CONV_W ** -0.5),
        'ssd_conv_b': normal(ks[11], (L, SSD_XBC), 0.02),
        'ssd_a_log': a_log,
        'ssd_dt_bias': dt_bias,
        'ssd_d': gain(ks[12], (L, SSD_HEADS), 0.1),
        'ssd_norm': gain(ks[15], (L, SSD_INNER)),
        'win_sink': normal(ks[16], (L, WIN_HEADS), 0.5),
        'w_br_diff': normal(ks[17], (L, DIFF_WIDTH, D), DIFF_WIDTH ** -0.5),
        'w_br_ssd': normal(ks[18], (L, SSD_INNER, D), SSD_INNER ** -0.5),
        'w_br_win': normal(ks[19], (L, WIN_WIDTH, D), WIN_WIDTH ** -0.5),
        'w_out': normal(ks[20], (L, D, D), D ** -0.5),
        'ffn_w_up': normal(ks[21], (L, D, 2 * D_FF), D ** -0.5),
        'ffn_conv_w': normal(ks[22], (L, FFN_CONV_W, 2 * D_FF), FFN_CONV_W ** -0.5),
        'ffn_conv_b': normal(ks[23], (L, 2 * D_FF), 0.02),
        'ffn_w_down': normal(ks[24], (L, D_FF, D), D_FF ** -0.5),
    }


def reference(x, c, ctx, c_ctx, w_ada, b_ada, norm_g, w_in, diff_lambda, diff_norm, ssd_conv_w, ssd_conv_b,
              ssd_a_log, ssd_dt_bias, ssd_d, ssd_norm, win_sink, w_br_diff, w_br_ssd, w_br_win, w_out,
              ffn_w_up, ffn_conv_w, ffn_conv_b, ffn_w_down):
    n = x.shape[1]
    rope_d = axial_rope(n, DIFF_DH)
    rope_w = axial_rope(n, WIN_DH)
    cond = jax.nn.silu(c)
    cond_ctx = jax.nn.silu(c_ctx)
    xc = ctx
    for l in range(DEPTH):
        need_ctx = l < DEPTH - 1
        lam_init = 0.8 - 0.6 * math.exp(-0.3 * l)
        mod = (cond @ w_ada[l] + b_ada[l])[:, None, :]
        mod_c = cond_ctx @ w_ada[l] + b_ada[l]
        sh1, sc1, g1, sh2, sc2, g2 = jnp.split(mod, 6, axis=-1)
        sh1c, sc1c, g1c, sh2c, sc2c, g2c = jnp.split(mod_c, 6, axis=-1)
        h = rmsnorm(x, norm_g[l, 0]) * (1.0 + sc1) + sh1
        hc = rmsnorm(xc, norm_g[l, 0]) * (1.0 + sc1c) + sh1c
        y, yc = token_mixer(h, hc, w_in[l], diff_lambda[l], diff_norm[l], lam_init, ssd_conv_w[l], ssd_conv_b[l],
                            ssd_a_log[l], ssd_dt_bias[l], ssd_d[l], ssd_norm[l], win_sink[l],
                            w_br_diff[l], w_br_ssd[l], w_br_win[l], w_out[l], rope_d, rope_w, need_ctx)
        x = x + g1 * rmsnorm(y, norm_g[l, 1])
        h = rmsnorm(x, norm_g[l, 2]) * (1.0 + sc2) + sh2
        x = x + g2 * rmsnorm(conv_ffn(h, ffn_w_up[l], ffn_conv_w[l], ffn_conv_b[l], ffn_w_down[l]), norm_g[l, 3])
        if need_ctx:
            xc = xc + g1c * rmsnorm(yc, norm_g[l, 1])
            hc = rmsnorm(xc, norm_g[l, 2]) * (1.0 + sc2c) + sh2c
            xc = xc + g2c * rmsnorm(conv_ffn(hc, ffn_w_up[l], ffn_conv_w[l], ffn_conv_b[l], ffn_w_down[l]),
                                    norm_g[l, 3])
    return x
```

```python
import functools
import math

import jax
import jax.numpy as jnp
from jax import lax
from jax.experimental import pallas as pl
from jax.experimental.pallas import tpu as pltpu

F32 = jnp.float32
BF16 = jnp.bfloat16

GRID_W = 64
ROPE_BASE = 10000.0
NORM_EPS = 1e-6
N_DIFF_HEADS = 8
DIFF_DH = 64
DIFF_WIDTH = N_DIFF_HEADS * 2 * DIFF_DH
SSD_INNER = 1024
SSD_HEAD_DIM = 64
SSD_HEADS = 16
SSD_GROUPS = 4
SSD_STATE = 128
SSD_CONV_W = 5
SSD_CHUNK = 128
SSD_XBC = SSD_INNER + 2 * SSD_GROUPS * SSD_STATE
WIN_HEADS = 16
WIN_KV_HEADS = 4
WIN_G = WIN_HEADS // WIN_KV_HEADS
WIN_DH = 64
WINDOW = 128
WIN_BLOCK = 128
WIN_WIDTH = WIN_HEADS * WIN_DH
WIN_KV_WIDTH = WIN_KV_HEADS * WIN_DH
FFN_CONV_W = 3

LOG2E = 1.4426950408889634
NEG_BIG = -1e30

LANES = 128
HALO = 16
CONV_ROWS = 256
VMEM_LIMIT = 56 * 1024 * 1024

PW = 1024
COL_DQ = 0 * PW
COL_DK = 1 * PW
COL_WQ = 2 * PW
COL_WKV = 3 * PW
COL_DT = COL_WKV + 2 * WIN_KV_WIDTH
COL_DV = 4 * PW
COL_SZ = 5 * PW
COL_XBC = 6 * PW
COL_GL = 8 * PW
P_WIDTH = 14 * PW
KIND_NONE, KIND_Q, KIND_K, KIND_K256 = 0, 1, 2, 3
COL_KINDS = (KIND_Q, KIND_K, KIND_Q, KIND_K256) + (KIND_NONE,) * 10


def _cparams(*sem):
    return pltpu.CompilerParams(dimension_semantics=sem, vmem_limit_bytes=VMEM_LIMIT)


def _sigmoid(x):
    return 1.0 / (1.0 + jnp.exp(-x))


def _silu(x):
    return x * _sigmoid(x)


def _pick(n, *cands):
    for c in cands:
        if all(v % c == 0 for v in n):
            return c
    raise ValueError(f"no block size among {cands} divides {n}")


def _mod_kernel(cv_ref, w_ref, b_ref, o_ref):
    s = _silu(cv_ref[...]).astype(BF16)
    o_ref[0] = jnp.dot(s, w_ref[0].astype(BF16), preferred_element_type=F32) + b_ref[0]


def _mod_call(cv, w_ada, b_ada):
    L, D, W6 = w_ada.shape
    tn = 1024
    return pl.pallas_call(
        _mod_kernel,
        grid=(L, W6 // tn),
        in_specs=[
            pl.BlockSpec((16, D), lambda l, j: (0, 0)),
            pl.BlockSpec((1, D, tn), lambda l, j: (l, 0, j)),
            pl.BlockSpec((1, 1, tn), lambda l, j: (l, 0, j)),
        ],
        out_specs=pl.BlockSpec((1, 16, tn), lambda l, j: (l, 0, j)),
        out_shape=jax.ShapeDtypeStruct((L, 16, W6), F32),
        compiler_params=_cparams("parallel", "parallel"),
        name="adaln_mod",
    )(cv, w_ada, b_ada.reshape(L, 1, W6))


def _norm_mod(x, g, sc, sh):
    ms = jnp.mean(x * x, axis=-1, keepdims=True)
    y = x * lax.rsqrt(ms + NORM_EPS) * g
    return y * (1.0 + sc) + sh


def _rope(y, cos, sa, sb):
    w = y.shape[1]
    rep = w // LANES
    if rep > 1:
        cos, sa, sb = (pltpu.repeat(t, rep, axis=1) for t in (cos, sa, sb))
    up = pltpu.roll(y, w - DIFF_DH // 2, axis=1)
    dn = pltpu.roll(y, DIFF_DH // 2, axis=1)
    return y * cos + up * sa + dn * sb


def _inproj_kernel(kind_ref, x_ref, sc_ref, sh_ref, g_ref, w_ref, tab_ref, o_ref, h_scr):
    j = pl.program_id(1)

    @pl.when(j == 0)
    def _():
        h_scr[...] = _norm_mod(x_ref[...], g_ref[...], sc_ref[0], sh_ref[0]).astype(BF16)

    y = jnp.dot(h_scr[...], w_ref[...], preferred_element_type=F32)
    kind = kind_ref[j]

    @pl.when(kind == KIND_NONE)
    def _():
        o_ref[...] = y.astype(BF16)

    @pl.when(jnp.logical_or(kind == KIND_Q, kind == KIND_K))
    def _():
        o_ref[...] = _rope(y, tab_ref[0, 0], tab_ref[0, 1], tab_ref[0, 2]).astype(BF16)

    @pl.when(kind == KIND_K256)
    def _():
        kw = WIN_KV_WIDTH
        o_ref[:, :kw] = _rope(y[:, :kw], tab_ref[0, 0], tab_ref[0, 1], tab_ref[0, 2]).astype(BF16)
        o_ref[:, kw:] = y[:, kw:].astype(BF16)


def _inproj_call(X, mod3, g, W, tab, kinds, *, B, N, M, sc_part, sh_part):
    T, D = X.shape
    tm = _pick((N, B * M), 1024, 512, 256)
    nlat = N // tm
    grp = lambda i: jnp.minimum((i * tm) // N, B)
    rblk = lambda i: jnp.where(i < B * nlat, i % nlat, nlat)
    ksel = lambda k: jnp.where(jnp.logical_or(k == KIND_K, k == KIND_K256), 1, 0)
    grid_spec = pltpu.PrefetchScalarGridSpec(
        num_scalar_prefetch=1,
        grid=(T // tm, P_WIDTH // PW),
        in_specs=[
            pl.BlockSpec((tm, D), lambda i, j, kr: (i, 0)),
            pl.BlockSpec((1, 1, D), lambda i, j, kr: (grp(i), 0, sc_part)),
            pl.BlockSpec((1, 1, D), lambda i, j, kr: (grp(i), 0, sh_part)),
            pl.BlockSpec((1, D), lambda i, j, kr: (0, 0)),
            pl.BlockSpec((D, PW), lambda i, j, kr: (0, j)),
            pl.BlockSpec((1, 3, tm, LANES), lambda i, j, kr: (ksel(kr[j]), 0, rblk(i), 0)),
        ],
        out_specs=pl.BlockSpec((tm, PW), lambda i, j, kr: (i, j)),
        scratch_shapes=[pltpu.VMEM((tm, D), BF16)],
    )
    return pl.pallas_call(
        _inproj_kernel,
        grid_spec=grid_spec,
        out_shape=jax.ShapeDtypeStruct((T, P_WIDTH), BF16),
        compiler_params=_cparams("parallel", "arbitrary"),
        name="in_proj",
    )(kinds, X, mod3, mod3, g, W, tab)


def _ffnup_kernel(x_ref, sc_ref, sh_ref, g_ref, w_ref, o_ref, h_scr):
    @pl.when(pl.program_id(1) == 0)
    def _():
        h_scr[...] = _norm_mod(x_ref[...], g_ref[...], sc_ref[0], sh_ref[0]).astype(BF16)

    o_ref[...] = jnp.dot(h_scr[...], w_ref[...], preferred_element_type=F32).astype(BF16)


def _ffnup_call(X, mod3, g, W, *, R, B, N, M, sc_part, sh_part):
    D = X.shape[1]
    NO = W.shape[1]
    tm = _pick((N, B * M), 1024, 512, 256)
    tn = 1024
    grp = lambda i: jnp.minimum((i * tm) // N, B)
    return pl.pallas_call(
        _ffnup_kernel,
        grid=(R // tm, NO // tn),
        in_specs=[
            pl.BlockSpec((tm, D), lambda i, j: (i, 0)),
            pl.BlockSpec((1, 1, D), lambda i, j: (grp(i), 0, sc_part)),
            pl.BlockSpec((1, 1, D), lambda i, j: (grp(i), 0, sh_part)),
            pl.BlockSpec((1, D), lambda i, j: (0, 0)),
            pl.BlockSpec((D, tn), lambda i, j: (0, j)),
        ],
        out_specs=pl.BlockSpec((tm, tn), lambda i, j: (i, j)),
        out_shape=jax.ShapeDtypeStruct((R, NO), BF16),
        scratch_shapes=[pltpu.VMEM((tm, D), BF16)],
        compiler_params=_cparams("parallel", "arbitrary"),
        name="ffn_up",
    )(X, mod3, mod3, g, W)


def _mnr_kernel(a_ref, w_ref, x_ref, gate_ref, g_ref, o_ref, acc):
    k = pl.program_id(1)

    @pl.when(k == 0)
    def _():
        acc[...] = jnp.zeros_like(acc)

    acc[...] += jnp.dot(a_ref[...], w_ref[...], preferred_element_type=F32)

    @pl.when(k == pl.num_programs(1) - 1)
    def _():
        y = acc[...]
        ms = jnp.mean(y * y, axis=-1, keepdims=True)
        r = y * lax.rsqrt(ms + NORM_EPS) * g_ref[...]
        o_ref[...] = x_ref[...] + gate_ref[0] * r


def _mnr_call(A, W, X, mod3, g, *, R, B, N, M, gate_part, tk, name):
    K, D = W.shape
    tm = _pick((N, B * M), 512, 256)
    grp = lambda i: jnp.minimum((i * tm) // N, B)
    return pl.pallas_call(
        _mnr_kernel,
        grid=(R // tm, K // tk),
        in_specs=[
            pl.BlockSpec((tm, tk), lambda i, k: (i, k)),
            pl.BlockSpec((tk, D), lambda i, k: (k, 0)),
            pl.BlockSpec((tm, D), lambda i, k: (i, 0)),
            pl.BlockSpec((1, 1, D), lambda i, k: (grp(i), 0, gate_part)),
            pl.BlockSpec((1, D), lambda i, k: (0, 0)),
        ],
        out_specs=pl.BlockSpec((tm, D), lambda i, k: (i, 0)),
        out_shape=jax.ShapeDtypeStruct((R, D), F32),
        scratch_shapes=[pltpu.VMEM((tm, D), F32)],
        compiler_params=_cparams("parallel", "arbitrary"),
        name=name,
    )(A, W, X, mod3, g)


def _diff_kernel(lam_ref, nw_ref, q_ref, kl_ref, kc_ref, vl_ref, vc_ref, o_ref, m_scr, l_scr, acc_scr,
                 *, tq, tk, n_lat_q, n_lat_k, lam_init):
    qi = pl.program_id(2)
    q = q_ref[...]
    lane = lax.broadcasted_iota(jnp.int32, q.shape, 1)
    zero = jnp.zeros_like(q)
    qs = jnp.concatenate([jnp.where(lane < DIFF_DH, q, zero), jnp.where(lane >= DIFF_DH, q, zero)], axis=0)

    m_scr[...] = jnp.full_like(m_scr, NEG_BIG)
    l_scr[...] = jnp.zeros_like(l_scr)
    acc_scr[...] = jnp.zeros_like(acc_scr)

    def step(kc, vc):
        s = lax.dot_general(qs, kc, (((1,), (1,)), ((), ())), preferred_element_type=F32)
        m_prev = m_scr[...]
        m_new = jnp.maximum(m_prev, jnp.max(s, axis=1, keepdims=True))
        alpha = jnp.exp2(m_prev - m_new)
        p = jnp.exp2(s - m_new[:, 0:1])
        l_scr[...] = alpha * l_scr[...] + jnp.sum(p, axis=1, keepdims=True)
        acc_scr[...] = alpha * acc_scr[...] + jnp.dot(p.astype(BF16), vc, preferred_element_type=F32)
        m_scr[...] = m_new

    def lat_step(c, carry):
        r0 = pl.multiple_of(c * tk, tk)
        step(kl_ref[pl.ds(r0, tk), :], vl_ref[pl.ds(r0, tk), :])
        return carry

    lax.fori_loop(0, jnp.where(qi < n_lat_q, n_lat_k, 0), lat_step, 0)
    step(kc_ref[...], vc_ref[...])

    lp = lam_ref[...]
    lam = (jnp.exp(jnp.sum(lp[0:1] * lp[1:2], axis=1, keepdims=True))
           - jnp.exp(jnp.sum(lp[2:3] * lp[3:4], axis=1, keepdims=True)) + lam_init)
    acc = acc_scr[...]
    l = l_scr[...]
    o = acc[:tq] / l[:tq] - lam * (acc[tq:] / l[tq:])
    ms = jnp.mean(o * o, axis=-1, keepdims=True)
    o_ref[...] = (o * lax.rsqrt(ms + NORM_EPS) * nw_ref[...] * (1.0 - lam_init)).astype(BF16)


def _diff_call(P, lam_p, nw, *, B, N, M, need_ctx, lam_init):
    tq = _pick((N, M), 256)
    tk = _pick((N,), 512, 256)
    nql, nqc = N // tq, (M // tq if need_ctx else 0)
    R = B * N + (B * M if need_ctx else 0)
    hw = 2 * DIFF_DH
    cq, ck, cv = COL_DQ // hw, COL_DK // hw, COL_DV // hw

    def qrow(b, qi):
        return jnp.where(qi < nql, b * nql + qi, B * nql + b * nqc + (qi - nql))

    kern = functools.partial(_diff_kernel, tq=tq, tk=tk, n_lat_q=nql, n_lat_k=N // tk, lam_init=lam_init)
    return pl.pallas_call(
        kern,
        grid=(B, N_DIFF_HEADS, nql + nqc),
        in_specs=[
            pl.BlockSpec((4, DIFF_DH), lambda b, h, qi: (0, 0)),
            pl.BlockSpec((1, hw), lambda b, h, qi: (0, 0)),
            pl.BlockSpec((tq, hw), lambda b, h, qi: (qrow(b, qi), cq + h)),
            pl.BlockSpec((N, hw), lambda b, h, qi: (b, ck + h)),
            pl.BlockSpec((M, hw), lambda b, h, qi: (B * N // M + b, ck + h)),
            pl.BlockSpec((N, hw), lambda b, h, qi: (b, cv + h)),
            pl.BlockSpec((M, hw), lambda b, h, qi: (B * N // M + b, cv + h)),
        ],
        out_specs=pl.BlockSpec((tq, hw), lambda b, h, qi: (qrow(b, qi), h)),
        out_shape=jax.ShapeDtypeStruct((R, DIFF_WIDTH), BF16),
        scratch_shapes=[pltpu.VMEM((2 * tq, LANES), F32)] * 3,
        compiler_params=_cparams("parallel", "parallel", "arbitrary"),
        name="diff_attn",
    )(lam_p, nw, P, P, P, P, P)


def _win_kernel(sink_ref, q_ref, kp_ref, kc_ref, kn_ref, kx_ref, o_ref, *, nb, N, M):
    i = pl.program_id(1)
    wb, kw = WIN_BLOCK, WIN_KV_WIDTH
    q = q_ref[...]
    kcat = jnp.concatenate([kp_ref[:, :kw], kc_ref[:, :kw], kn_ref[:, :kw], kx_ref[:, :kw]], axis=0)
    vcat = jnp.concatenate([kp_ref[:, kw:], kc_ref[:, kw:], kn_ref[:, kw:], kx_ref[:, kw:]], axis=0)
    nk = 3 * wb + M
    rows = WIN_G * wb
    row = lax.broadcasted_iota(jnp.int32, (rows, nk), 0) % wb
    col = lax.broadcasted_iota(jnp.int32, (rows, nk), 1)
    kpos = (i - 1) * wb + col
    n_band = jnp.where(i < nb, N, 0)
    band_ok = (jnp.abs(col - wb - row) <= WINDOW) & (kpos >= 0) & (kpos < n_band)
    valid = band_ok | (col >= 3 * wb)
    lane = lax.broadcasted_iota(jnp.int32, (wb, kw), 1) // WIN_DH
    lane_o = lax.broadcasted_iota(jnp.int32, (rows, kw), 1) // WIN_DH
    grow = lax.broadcasted_iota(jnp.int32, (rows, 1), 0) // wb
    zero = jnp.zeros((wb, kw), BF16)
    acc = jnp.zeros((rows, kw), F32)
    for kv in range(WIN_KV_HEADS):
        qs = jnp.concatenate([jnp.where(lane == kv, q[:, g * kw:(g + 1) * kw], zero) for g in range(WIN_G)], axis=0)
        s = lax.dot_general(qs, kcat, (((1,), (1,)), ((), ())), preferred_element_type=F32)
        s = jnp.where(valid, s, NEG_BIG)
        sink = jnp.zeros((rows, 1), F32)
        for g in range(WIN_G):
            sink = jnp.where(grow == g, sink_ref[kv * WIN_G + g] * LOG2E, sink)
        m = jnp.maximum(jnp.max(s, axis=1, keepdims=True), sink)
        p = jnp.exp2(s - m)
        l = jnp.sum(p, axis=1, keepdims=True) + jnp.exp2(sink - m)
        o = jnp.dot(p.astype(BF16), vcat, preferred_element_type=F32) / l
        acc = jnp.where(lane_o == kv, o, acc)
    for g in range(WIN_G):
        o_ref[:, g * kw:(g + 1) * kw] = acc[g * wb:(g + 1) * wb].astype(BF16)


def _win_call(P, sink, *, B, N, M, need_ctx):
    wb = WIN_BLOCK
    nb, nbc = N // wb, (M // wb if need_ctx else 0)
    R = B * N + (B * M if need_ctx else 0)
    cq, ckv = COL_WQ // PW, COL_WKV // (2 * WIN_KV_WIDTH)

    def qrow(b, i):
        return jnp.where(i < nb, b * nb + i, B * nb + b * nbc + (i - nb))

    def band(b, i, d):
        return b * nb + jnp.clip(i + d, 0, nb - 1)

    kvw = 2 * WIN_KV_WIDTH
    kern = functools.partial(_win_kernel, nb=nb, N=N, M=M)
    grid_spec = pltpu.PrefetchScalarGridSpec(
        num_scalar_prefetch=1,
        grid=(B, nb + nbc),
        in_specs=[
            pl.BlockSpec((wb, PW), lambda b, i, s: (qrow(b, i), cq)),
            pl.BlockSpec((wb, kvw), lambda b, i, s: (band(b, i, -1), ckv)),
            pl.BlockSpec((wb, kvw), lambda b, i, s: (band(b, i, 0), ckv)),
            pl.BlockSpec((wb, kvw), lambda b, i, s: (band(b, i, 1), ckv)),
            pl.BlockSpec((M, kvw), lambda b, i, s: (B * N // M + b, ckv)),
        ],
        out_specs=pl.BlockSpec((wb, PW), lambda b, i, s: (qrow(b, i), 0)),
    )
    return pl.pallas_call(
        kern,
        grid_spec=grid_spec,
        out_shape=jax.ShapeDtypeStruct((R, WIN_WIDTH), BF16),
        compiler_params=_cparams("parallel", "arbitrary"),
        name="win_attn",
    )(sink, P, P, P, P, P)


def _seq_edges(i, R_lat, N, M):
    r0 = i * CONV_ROWS
    lat = r0 < R_lat
    first = jnp.where(lat, r0 % N == 0, (r0 - R_lat) % M == 0)
    last = jnp.where(lat, (r0 + CONV_ROWS) % N == 0, (r0 + CONV_ROWS - R_lat) % M == 0)
    return first, last


def _conv_taps(buf, main_ref, prev_ref, next_ref, w_ref, b_ref, first, last, k):
    buf[0:HALO] = jnp.where(first, 0.0, prev_ref[...].astype(F32))
    buf[HALO:HALO + CONV_ROWS] = main_ref[...].astype(F32)
    buf[HALO + CONV_ROWS:] = jnp.where(last, 0.0, next_ref[...].astype(F32))
    out = b_ref[...]
    for j in range(k):
        out = out + buf[pl.ds(HALO - k // 2 + j, CONV_ROWS), :] * w_ref[j:j + 1, :]
    return out


def _ssdconv_kernel(m_ref, p_ref, n_ref, w_ref, b_ref, o_ref, buf, *, R_lat, N, M):
    first, last = _seq_edges(pl.program_id(0), R_lat, N, M)
    o_ref[...] = _silu(_conv_taps(buf, m_ref, p_ref, n_ref, w_ref, b_ref, first, last, SSD_CONV_W)).astype(BF16)


def _ssdconv_call(P, w, b, *, B, N, M):
    T = P.shape[0]
    cr, W = CONV_ROWS, SSD_XBC
    hb = cr // HALO
    cb = COL_XBC // W
    nh = T // HALO
    kern = functools.partial(_ssdconv_kernel, R_lat=B * N, N=N, M=M)
    return pl.pallas_call(
        kern,
        grid=(T // cr,),
        in_specs=[
            pl.BlockSpec((cr, W), lambda i: (i, cb)),
            pl.BlockSpec((HALO, W), lambda i: (jnp.maximum(i * hb - 1, 0), cb)),
            pl.BlockSpec((HALO, W), lambda i: (jnp.minimum((i + 1) * hb, nh - 1), cb)),
            pl.BlockSpec((SSD_CONV_W, W), lambda i: (0, 0)),
            pl.BlockSpec((1, W), lambda i: (0, 0)),
        ],
        out_specs=pl.BlockSpec((cr, W), lambda i: (i, 0)),
        out_shape=jax.ShapeDtypeStruct((T, W), BF16),
        scratch_shapes=[pltpu.VMEM((cr + 2 * HALO, W), F32)],
        compiler_params=_cparams("parallel"),
        name="ssd_conv",
    )(P, P, P, w, b)


def _ffnconv_kernel(am_ref, ap_ref, an_ref, gm_ref, gp_ref, gn_ref, wa_ref, wg_ref, ba_ref, bg_ref, o_ref,
                    bufa, bufg, *, R_lat, N, M):
    first, last = _seq_edges(pl.program_id(0), R_lat, N, M)
    a = _conv_taps(bufa, am_ref, ap_ref, an_ref, wa_ref, ba_ref, first, last, FFN_CONV_W)
    g = _conv_taps(bufg, gm_ref, gp_ref, gn_ref, wg_ref, bg_ref, first, last, FFN_CONV_W)
    o_ref[...] = (_silu(a) * g).astype(BF16)


def _ffnconv_call(U, w, b, *, R, B, N, M):
    F2 = U.shape[1]
    F = F2 // 2
    cr = CONV_ROWS
    tc = _pick((F,), 1408, 512, 128)
    nc = F // tc
    hb = cr // HALO
    nh = R // HALO
    prev = lambda i: jnp.maximum(i * hb - 1, 0)
    nxt = lambda i: jnp.minimum((i + 1) * hb, nh - 1)
    kern = functools.partial(_ffnconv_kernel, R_lat=B * N, N=N, M=M)
    return pl.pallas_call(
        kern,
        grid=(R // cr, nc),
        in_specs=[
            pl.BlockSpec((cr, tc), lambda i, j: (i, j)),
            pl.BlockSpec((HALO, tc), lambda i, j: (prev(i), j)),
            pl.BlockSpec((HALO, tc), lambda i, j: (nxt(i), j)),
            pl.BlockSpec((cr, tc), lambda i, j: (i, j + nc)),
            pl.BlockSpec((HALO, tc), lambda i, j: (prev(i), j + nc)),
            pl.BlockSpec((HALO, tc), lambda i, j: (nxt(i), j + nc)),
            pl.BlockSpec((FFN_CONV_W, tc), lambda i, j: (0, j)),
            pl.BlockSpec((FFN_CONV_W, tc), lambda i, j: (0, j + nc)),
            pl.BlockSpec((1, tc), lambda i, j: (0, j)),
            pl.BlockSpec((1, tc), lambda i, j: (0, j + nc)),
        ],
        out_specs=pl.BlockSpec((cr, tc), lambda i, j: (i, j)),
        out_shape=jax.ShapeDtypeStruct((R, F), BF16),
        scratch_shapes=[pltpu.VMEM((cr + 2 * HALO, tc), F32)] * 2,
        compiler_params=_cparams("parallel", "parallel"),
        name="ffn_conv_gate",
    )(U, U, U, U, U, U, w, w, b, b)


def _cumsum_rows(a):
    n = a.shape[0]
    tri = (lax.broadcasted_iota(jnp.int32, (n, n), 0) >= lax.broadcasted_iota(jnp.int32, (n, n), 1)).astype(BF16)
    hi = a.astype(BF16)
    r1 = a - hi.astype(F32)
    mid = r1.astype(BF16)
    lo = (r1 - mid.astype(F32)).astype(BF16)
    dot = lambda v: jnp.dot(tri, v, preferred_element_type=F32)
    return dot(hi) + dot(mid) + dot(lo)


def _ssd_chunk(u_ref, dt_ref, bias_ref, alog_ref, s_ref, y_ref, *, reverse, lane0):
    ch, hp = SSD_CHUNK, SSD_HEAD_DIM
    gw = (SSD_HEADS // SSD_GROUPS) * hp
    hpg = SSD_HEADS // SSD_GROUPS
    x = dt_ref[...].astype(F32) + bias_ref[...]
    dt = jnp.maximum(x, 0.0) + jnp.log(1.0 + jnp.exp(-jnp.abs(x)))
    ad = dt * (-jnp.exp(alog_ref[...]))
    acum = _cumsum_rows(ad)
    total = acum[ch - 1:ch, :]
    if reverse:
        cvec = acum - ad
        e_off = jnp.exp(total - cvec)
        e_ws = jnp.exp(cvec)
    else:
        cvec = acum
        e_off = jnp.exp(acum)
        e_ws = jnp.exp(total - acum)
    e_tot = jnp.exp(total)
    cT = cvec.T
    a_xdt, a_xw = dt, dt * e_ws
    li = lax.broadcasted_iota(jnp.int32, (ch, ch), 0)
    si = lax.broadcasted_iota(jnp.int32, (ch, ch), 1)
    tri = (si >= li) if reverse else (li >= si)
    hsel = lax.broadcasted_iota(jnp.int32, (ch, gw), 1) // hp
    hsel1 = lax.broadcasted_iota(jnp.int32, (1, gw), 1) // hp

    def expand(src, g, sel):
        rows = src.shape[0]
        out = jnp.zeros((rows, gw), F32)
        for r in range(hpg):
            j = lane0 + g * hpg + r
            out = jnp.where(sel == r, jnp.broadcast_to(src[:, j:j + 1], (rows, gw)), out)
        return out

    for g in range(SSD_GROUPS):
        bg = u_ref[:, SSD_INNER + g * SSD_STATE:SSD_INNER + (g + 1) * SSD_STATE]
        cg = u_ref[:, SSD_INNER + (SSD_GROUPS + g) * SSD_STATE:SSD_INNER + (SSD_GROUPS + g + 1) * SSD_STATE]
        xs = u_ref[:, g * gw:(g + 1) * gw].astype(F32)
        cb = lax.dot_general(cg, bg, (((1,), (1,)), ((), ())), preferred_element_type=F32)
        xdt = (xs * expand(a_xdt, g, hsel)).astype(BF16)
        xw = (xs * expand(a_xw, g, hsel)).astype(BF16)
        ydiag = jnp.zeros((ch, gw), F32)
        for r in range(hpg):
            j = lane0 + g * hpg + r
            colv = jnp.broadcast_to(cvec[:, j:j + 1], (ch, ch))
            rowv = jnp.broadcast_to(cT[j:j + 1, :], (ch, ch))
            d = (rowv - colv) if reverse else (colv - rowv)
            lm = jnp.exp(jnp.where(tri, d, NEG_BIG))
            yh = jnp.dot((cb * lm).astype(BF16), xdt, preferred_element_type=F32)
            ydiag = jnp.where(hsel == r, yh, ydiag)
        sg = s_ref[:, g * gw:(g + 1) * gw]
        yoff = jnp.dot(cg, sg.astype(BF16), preferred_element_type=F32) * expand(e_off, g, hsel)
        y_ref[:, g * gw:(g + 1) * gw] = (ydiag + yoff).astype(BF16)
        s_ref[:, g * gw:(g + 1) * gw] = (
            sg * expand(e_tot, g, hsel1)
            + lax.dot_general(bg, xw, (((0,), (0,)), ((), ())), preferred_element_type=F32))


def _ssd_kernel(uf_ref, dtf_ref, ub_ref, dtb_ref, bias_ref, alog_ref, yf_ref, yb_ref, sf, sb):
    @pl.when(pl.program_id(1) == 0)
    def _():
        sf[...] = jnp.zeros_like(sf)
        sb[...] = jnp.zeros_like(sb)

    _ssd_chunk(uf_ref, dtf_ref, bias_ref, alog_ref, sf, yf_ref, reverse=False, lane0=0)
    _ssd_chunk(ub_ref, dtb_ref, bias_ref, alog_ref, sb, yb_ref, reverse=True, lane0=SSD_HEADS)


def _ssd_call(U, P, bias, alog, *, B, N, M):
    T = U.shape[0]
    ch = SSD_CHUNK
    ncl, ncc = N // ch, M // ch
    cdt = COL_DT // LANES

    def fwd(b, t):
        return jnp.where(t < ncc, B * ncl + b * ncc + t, b * ncl + (t - ncc))

    def bwd(b, t):
        return jnp.where(t < ncc, B * ncl + b * ncc + (ncc - 1 - t), b * ncl + (ncl - 1 - (t - ncc)))

    return pl.pallas_call(
        _ssd_kernel,
        grid=(B, ncl + ncc),
        in_specs=[
            pl.BlockSpec((ch, SSD_XBC), lambda b, t: (fwd(b, t), 0)),
            pl.BlockSpec((ch, LANES), lambda b, t: (fwd(b, t), cdt)),
            pl.BlockSpec((ch, SSD_XBC), lambda b, t: (bwd(b, t), 0)),
            pl.BlockSpec((ch, LANES), lambda b, t: (bwd(b, t), cdt)),
            pl.BlockSpec((1, LANES), lambda b, t: (0, 0)),
            pl.BlockSpec((1, LANES), lambda b, t: (0, 0)),
        ],
        out_specs=[
            pl.BlockSpec((ch, SSD_INNER), lambda b, t: (fwd(b, t), 0)),
            pl.BlockSpec((ch, SSD_INNER), lambda b, t: (bwd(b, t), 0)),
        ],
        out_shape=[jax.ShapeDtypeStruct((T, SSD_INNER), BF16)] * 2,
        scratch_shapes=[pltpu.VMEM((SSD_STATE, SSD_INNER), F32)] * 2,
        compiler_params=_cparams("parallel", "arbitrary"),
        name="ssd_scan",
    )(U, P, U, P, bias, alog)


def _merge_kernel(yd_ref, yf_ref, yb_ref, xs_ref, z_ref, yw_ref, gd_ref, gs_ref, gw_ref,
                  wd_ref, ws_ref, ww_ref, dsk_ref, sn_ref, o_ref):
    y = yf_ref[...].astype(F32) + yb_ref[...].astype(F32) + dsk_ref[...] * xs_ref[...].astype(F32)
    y = y * _silu(z_ref[...].astype(F32))
    ms = jnp.mean(y * y, axis=-1, keepdims=True)
    s = (y * lax.rsqrt(ms + NORM_EPS) * sn_ref[...]).astype(BF16)
    dot = lambda a, w: jnp.dot(a, w[...], preferred_element_type=F32)
    sig = lambda r: _sigmoid(r[...].astype(F32))
    acc = sig(gd_ref) * dot(yd_ref[...], wd_ref)
    acc = acc + sig(gs_ref) * dot(s, ws_ref)
    acc = acc + sig(gw_ref) * dot(yw_ref[...], ww_ref)
    o_ref[...] = acc.astype(BF16)


def _merge_call(YD, YF, YB, U, P, YW, wd, ws, ww, dsk, sn, *, R, D):
    tm = 256
    W = SSD_INNER
    cg = COL_GL // D
    row = lambda c: pl.BlockSpec((tm, W), lambda i: (i, c))
    gate = lambda c: pl.BlockSpec((tm, D), lambda i: (i, cg + c))
    wspec = pl.BlockSpec((W, D), lambda i: (0, 0))
    vec = pl.BlockSpec((1, W), lambda i: (0, 0))
    return pl.pallas_call(
        _merge_kernel,
        grid=(R // tm,),
        in_specs=[row(0), row(0), row(0), row(0), row(COL_SZ // W), row(0),
                  gate(0), gate(1), gate(2), wspec, wspec, wspec, vec, vec],
        out_specs=pl.BlockSpec((tm, D), lambda i: (i, 0)),
        out_shape=jax.ShapeDtypeStruct((R, D), BF16),
        compiler_params=_cparams("parallel"),
        name="branch_merge",
    )(YD, YF, YB, U, P, YW, P, P, P, wd, ws, ww, dsk, sn)


def _rope_tables(N, rows_ctx):
    hd = DIFF_DH
    pos = jnp.arange(N)
    row = (pos // GRID_W).astype(F32)
    colp = (pos % GRID_W).astype(F32)
    axis_dim = hd // 2
    inv = ROPE_BASE ** (-jnp.arange(0, axis_dim, 2, dtype=F32) / axis_dim)
    ang = jnp.concatenate([row[:, None] * inv, colp[:, None] * inv], axis=-1)
    lane = jnp.arange(LANES)
    cos = jnp.cos(ang)[:, lane % (hd // 2)]
    sin = jnp.sin(ang)[:, lane % (hd // 2)]
    first = (lane % hd) < hd // 2
    lat = jnp.stack([cos, jnp.where(first, -sin, 0.0), jnp.where(first, 0.0, sin)])
    ctx = jnp.stack([jnp.ones((rows_ctx, LANES), F32), jnp.zeros((rows_ctx, LANES), F32),
                     jnp.zeros((rows_ctx, LANES), F32)])
    k_tab = jnp.concatenate([lat, ctx], axis=1)
    q_tab = k_tab * (DIFF_DH ** -0.5 * LOG2E)
    return jnp.stack([q_tab, k_tab])


def _layout_w_in(w):
    D = w.shape[0]
    sizes = (DIFF_WIDTH, DIFF_WIDTH, DIFF_WIDTH, SSD_INNER, SSD_XBC, 2 * SSD_HEADS,
             WIN_WIDTH, WIN_KV_WIDTH, WIN_KV_WIDTH, 3 * D)
    parts, s = [], 0
    for z in sizes:
        parts.append(w[:, s:s + z])
        s += z
    dq, dk, dv, sz, sxbc, sdt, wq, wk, wv, gl = parts
    wq = wq.reshape(D, WIN_KV_HEADS, WIN_G, WIN_DH).transpose(0, 2, 1, 3).reshape(D, WIN_WIDTH)
    pad = jnp.zeros((D, PW - 2 * WIN_KV_WIDTH - 2 * SSD_HEADS), w.dtype)
    return jnp.concatenate([dq, dk, wq, wk, wv, sdt, pad, dv, sz, sxbc, gl], axis=1).astype(BF16)


def kernel(x, c, ctx, c_ctx, w_ada, b_ada, norm_g, w_in, diff_lambda, diff_norm, ssd_conv_w, ssd_conv_b,
           ssd_a_log, ssd_dt_bias, ssd_d, ssd_norm, win_sink, w_br_diff, w_br_ssd, w_br_win, w_out,
           ffn_w_up, ffn_conv_w, ffn_conv_b, ffn_w_down):
    B, N, D = x.shape
    M = ctx.shape[1]
    L = w_ada.shape[0]
    assert D == 2 * PW and P_WIDTH == COL_GL + 3 * D
    assert N % CONV_ROWS == 0 and M % CONV_ROWS == 0 and (B * N) % M == 0 and B + 1 <= 16
    T = B * N + B * M

    X = jnp.concatenate([x.reshape(B * N, D), ctx.reshape(B * M, D)], axis=0)
    cv = jnp.zeros((16, D), F32).at[:B].set(c).at[B].set(c_ctx)
    mod = _mod_call(cv, w_ada, b_ada)

    tm_in = _pick((N, B * M), 1024, 512, 256)
    tab = _rope_tables(N, tm_in)
    kinds = jnp.asarray(COL_KINDS, jnp.int32)
    pad_lanes = LANES - 2 * SSD_HEADS

    for l in range(L):
        need_ctx = l < L - 1
        lam_init = 0.8 - 0.6 * math.exp(-0.3 * l)
        R = T if need_ctx else B * N
        mod3 = mod[l].reshape(16, 1, 6 * D)
        kw = dict(B=B, N=N, M=M)

        P = _inproj_call(X, mod3, norm_g[l, 0:1], _layout_w_in(w_in[l]), tab, kinds, sc_part=1, sh_part=0, **kw)

        YD = _diff_call(P, diff_lambda[l], diff_norm[l].reshape(1, -1), need_ctx=need_ctx, lam_init=lam_init, **kw)
        YW = _win_call(P, win_sink[l], need_ctx=need_ctx, **kw)

        U = _ssdconv_call(P, ssd_conv_w[l], ssd_conv_b[l].reshape(1, -1), **kw)
        bias = jnp.pad(ssd_dt_bias[l].reshape(1, -1), ((0, 0), (0, pad_lanes)))
        alog = jnp.pad(ssd_a_log[l].reshape(1, -1), ((0, 0), (0, pad_lanes)))
        YF, YB = _ssd_call(U, P, bias, alog, **kw)

        ww = w_br_win[l].reshape(WIN_KV_HEADS, WIN_G, WIN_DH, D).transpose(1, 0, 2, 3).reshape(WIN_WIDTH, D)
        dsk = jnp.repeat(ssd_d[l], SSD_HEAD_DIM).reshape(1, -1)
        Z = _merge_call(YD, YF, YB, U, P, YW, w_br_diff[l].astype(BF16), w_br_ssd[l].astype(BF16),
                        ww.astype(BF16), dsk, ssd_norm[l].reshape(1, -1), R=R, D=D)
        X1 = _mnr_call(Z, w_out[l].astype(BF16), X, mod3, norm_g[l, 1:2], R=R, gate_part=2, tk=1024,
                       name="out_proj", **kw)

        UP = _ffnup_call(X1, mod3, norm_g[l, 2:3], ffn_w_up[l].astype(BF16), R=R, sc_part=4, sh_part=3, **kw)
        A = _ffnconv_call(UP, ffn_conv_w[l], ffn_conv_b[l].reshape(1, -1), R=R, **kw)
        X = _mnr_call(A, ffn_w_down[l].astype(BF16), X1, mod3, norm_g[l, 3:4], R=R, gate_part=5,
                      tk=_pick((A.shape[1],), 1408, 512, 128), name="ffn_down", **kw)

    return X[:B * N].reshape(B, N, D)
```

```python
import functools
import math

import jax
import jax.numpy as jnp
from jax import lax
from jax.experimental import pallas as pl
from jax.experimental.pallas import tpu as pltpu

F32 = jnp.float32
BF16 = jnp.bfloat16

GRID_W = 64
ROPE_BASE = 10000.0
NORM_EPS = 1e-6
N_DIFF_HEADS = 8
DIFF_DH = 64
DIFF_WIDTH = N_DIFF_HEADS * 2 * DIFF_DH
SSD_INNER = 1024
SSD_HEAD_DIM = 64
SSD_HEADS = 16
SSD_GROUPS = 4
SSD_STATE = 128
SSD_CONV_W = 5
SSD_CHUNK = 128
SSD_XBC = SSD_INNER + 2 * SSD_GROUPS * SSD_STATE
WIN_HEADS = 16
WIN_KV_HEADS = 4
WIN_G = WIN_HEADS // WIN_KV_HEADS
WIN_DH = 64
WINDOW = 128
WIN_BLOCK = 128
WIN_WIDTH = WIN_HEADS * WIN_DH
WIN_KV_WIDTH = WIN_KV_HEADS * WIN_DH
FFN_CONV_W = 3

LOG2E = 1.4426950408889634
NEG_BIG = -1e30

LANES = 128
HALO = 16
CONV_ROWS = 256
VMEM_LIMIT = 56 * 1024 * 1024

PW = 1024
COL_DQ = 0 * PW
COL_DK = 1 * PW
COL_WQ = 2 * PW
COL_WKV = 3 * PW
COL_DT = COL_WKV + 2 * WIN_KV_WIDTH
COL_DV = 4 * PW
COL_SZ = 5 * PW
COL_XBC = 6 * PW
COL_GL = 8 * PW
P_WIDTH = 14 * PW
KIND_NONE, KIND_Q, KIND_K, KIND_K256 = 0, 1, 2, 3
COL_KINDS = (KIND_Q, KIND_K, KIND_Q, KIND_K256) + (KIND_NONE,) * 10


def _cparams(*sem):
    return pltpu.CompilerParams(dimension_semantics=sem, vmem_limit_bytes=VMEM_LIMIT)


def _sigmoid(x):
    return 1.0 / (1.0 + jnp.exp(-x))


def _silu(x):
    return x * _sigmoid(x)


def _pick(n, *cands):
    for c in cands:
        if all(v % c == 0 for v in n):
            return c
    raise ValueError(f"no block size among {cands} divides {n}")


def _mod_kernel(cv_ref, w_ref, b_ref, o_ref):
    s = _silu(cv_ref[...]).astype(BF16)
    o_ref[0] = jnp.dot(s, w_ref[0].astype(BF16), preferred_element_type=F32) + b_ref[0]


def _mod_call(cv, w_ada, b_ada):
    L, D, W6 = w_ada.shape
    tn = 1024
    return pl.pallas_call(
        _mod_kernel,
        grid=(L, W6 // tn),
        in_specs=[
            pl.BlockSpec((16, D), lambda l, j: (0, 0)),
            pl.BlockSpec((1, D, tn), lambda l, j: (l, 0, j)),
            pl.BlockSpec((1, 1, tn), lambda l, j: (l, 0, j)),
        ],
        out_specs=pl.BlockSpec((1, 16, tn), lambda l, j: (l, 0, j)),
        out_shape=jax.ShapeDtypeStruct((L, 16, W6), F32),
        compiler_params=_cparams("parallel", "parallel"),
        name="adaln_mod",
    )(cv, w_ada, b_ada.reshape(L, 1, W6))


def _norm_mod(x, g, sc, sh):
    ms = jnp.mean(x * x, axis=-1, keepdims=True)
    y = x * lax.rsqrt(ms + NORM_EPS) * g
    return y * (1.0 + sc) + sh


def _rope(y, cos, sa, sb):
    w = y.shape[1]
    rep = w // LANES
    if rep > 1:
        cos, sa, sb = (pltpu.repeat(t, rep, axis=1) for t in (cos, sa, sb))
    up = pltpu.roll(y, w - DIFF_DH // 2, axis=1)
    dn = pltpu.roll(y, DIFF_DH // 2, axis=1)
    return y * cos + up * sa + dn * sb


def _inproj_kernel(kind_ref, x_ref, sc_ref, sh_ref, g_ref, w_ref, tab_ref, o_ref, h_scr):
    j = pl.program_id(1)

    @pl.when(j == 0)
    def _():
        h_scr[...] = _norm_mod(x_ref[...], g_ref[...], sc_ref[0], sh_ref[0]).astype(BF16)

    y = jnp.dot(h_scr[...], w_ref[...], preferred_element_type=F32)
    kind = kind_ref[j]

    @pl.when(kind == KIND_NONE)
    def _():
        o_ref[...] = y.astype(BF16)

    @pl.when(jnp.logical_or(kind == KIND_Q, kind == KIND_K))
    def _():
        o_ref[...] = _rope(y, tab_ref[0, 0], tab_ref[0, 1], tab_ref[0, 2]).astype(BF16)

    @pl.when(kind == KIND_K256)
    def _():
        kw = WIN_KV_WIDTH
        o_ref[:, :kw] = _rope(y[:, :kw], tab_ref[0, 0], tab_ref[0, 1], tab_ref[0, 2]).astype(BF16)
        o_ref[:, kw:] = y[:, kw:].astype(BF16)


def _inproj_call(X, mod3, g, W, tab, kinds, *, B, N, M, sc_part, sh_part):
    T, D = X.shape
    tm = _pick((N, B * M), 1024, 512, 256)
    nlat = N // tm
    grp = lambda i: jnp.minimum((i * tm) // N, B)
    rblk = lambda i: jnp.where(i < B * nlat, i % nlat, nlat)
    ksel = lambda k: jnp.where(jnp.logical_or(k == KIND_K, k == KIND_K256), 1, 0)
    grid_spec = pltpu.PrefetchScalarGridSpec(
        num_scalar_prefetch=1,
        grid=(T // tm, P_WIDTH // PW),
        in_specs=[
            pl.BlockSpec((tm, D), lambda i, j, kr: (i, 0)),
            pl.BlockSpec((1, 1, D), lambda i, j, kr: (grp(i), 0, sc_part)),
            pl.BlockSpec((1, 1, D), lambda i, j, kr: (grp(i), 0, sh_part)),
            pl.BlockSpec((1, D), lambda i, j, kr: (0, 0)),
            pl.BlockSpec((D, PW), lambda i, j, kr: (0, j)),
            pl.BlockSpec((1, 3, tm, LANES), lambda i, j, kr: (ksel(kr[j]), 0, rblk(i), 0)),
        ],
        out_specs=pl.BlockSpec((tm, PW), lambda i, j, kr: (i, j)),
        scratch_shapes=[pltpu.VMEM((tm, D), BF16)],
    )
    return pl.pallas_call(
        _inproj_kernel,
        grid_spec=grid_spec,
        out_shape=jax.ShapeDtypeStruct((T, P_WIDTH), BF16),
        compiler_params=_cparams("parallel", "arbitrary"),
        name="in_proj",
    )(kinds, X, mod3, mod3, g, W, tab)


def _ffnup_kernel(x_ref, sc_ref, sh_ref, g_ref, w_ref, o_ref, h_scr):
    @pl.when(pl.program_id(1) == 0)
    def _():
        h_scr[...] = _norm_mod(x_ref[...], g_ref[...], sc_ref[0], sh_ref[0]).astype(BF16)

    o_ref[...] = jnp.dot(h_scr[...], w_ref[...], preferred_element_type=F32).astype(BF16)


def _ffnup_call(X, mod3, g, W, *, R, B, N, M, sc_part, sh_part):
    D = X.shape[1]
    NO = W.shape[1]
    tm = _pick((N, B * M), 1024, 512, 256)
    tn = 1024
    grp = lambda i: jnp.minimum((i * tm) // N, B)
    return pl.pallas_call(
        _ffnup_kernel,
        grid=(R // tm, NO // tn),
        in_specs=[
            pl.BlockSpec((tm, D), lambda i, j: (i, 0)),
            pl.BlockSpec((1, 1, D), lambda i, j: (grp(i), 0, sc_part)),
            pl.BlockSpec((1, 1, D), lambda i, j: (grp(i), 0, sh_part)),
            pl.BlockSpec((1, D), lambda i, j: (0, 0)),
            pl.BlockSpec((D, tn), lambda i, j: (0, j)),
        ],
        out_specs=pl.BlockSpec((tm, tn), lambda i, j: (i, j)),
        out_shape=jax.ShapeDtypeStruct((R, NO), BF16),
        scratch_shapes=[pltpu.VMEM((tm, D), BF16)],
        compiler_params=_cparams("parallel", "arbitrary"),
        name="ffn_up",
    )(X, mod3, mod3, g, W)


def _mnr_kernel(a_ref, w_ref, x_ref, gate_ref, g_ref, o_ref, acc):
    k = pl.program_id(1)

    @pl.when(k == 0)
    def _():
        acc[...] = jnp.zeros_like(acc)

    acc[...] += jnp.dot(a_ref[...], w_ref[...], preferred_element_type=F32)

    @pl.when(k == pl.num_programs(1) - 1)
    def _():
        y = acc[...]
        ms = jnp.mean(y * y, axis=-1, keepdims=True)
        r = y * lax.rsqrt(ms + NORM_EPS) * g_ref[...]
        o_ref[...] = x_ref[...] + gate_ref[0] * r


def _mnr_call(A, W, X, mod3, g, *, R, B, N, M, gate_part, tk, name):
    K, D = W.shape
    tm = _pick((N, B * M), 512, 256)
    grp = lambda i: jnp.minimum((i * tm) // N, B)
    return pl.pallas_call(
        _mnr_kernel,
        grid=(R // tm, K // tk),
        in_specs=[
            pl.BlockSpec((tm, tk), lambda i, k: (i, k)),
            pl.BlockSpec((tk, D), lambda i, k: (k, 0)),
            pl.BlockSpec((tm, D), lambda i, k: (i, 0)),
            pl.BlockSpec((1, 1, D), lambda i, k: (grp(i), 0, gate_part)),
            pl.BlockSpec((1, D), lambda i, k: (0, 0)),
        ],
        out_specs=pl.BlockSpec((tm, D), lambda i, k: (i, 0)),
        out_shape=jax.ShapeDtypeStruct((R, D), F32),
        scratch_shapes=[pltpu.VMEM((tm, D), F32)],
        compiler_params=_cparams("parallel", "arbitrary"),
        name=name,
    )(A, W, X, mod3, g)


def _diff_kernel(lam_ref, nw_ref, q_ref, kl_ref, kc_ref, vl_ref, vc_ref, o_ref, vt_scr, m_scr, acc_scr,
                 *, tq, tk, n_lat_q, lam_init):
    qi = pl.program_id(2)
    N, M = kl_ref.shape[0], kc_ref.shape[0]
    hd = 2 * DIFF_DH
    n_lat_k = N // tk

    @pl.when(qi == 0)
    def _():
        for c in range(n_lat_k):
            vt_scr[0:hd, c * tk:(c + 1) * tk] = vl_ref[c * tk:(c + 1) * tk, :].astype(F32).T.astype(BF16)
        vt_scr[0:hd, N:N + M] = vc_ref[...].astype(F32).T.astype(BF16)
        vt_scr[hd:, :] = jnp.ones((vt_scr.shape[0] - hd, N + M), BF16)

    q = q_ref[...]
    lane = lax.broadcasted_iota(jnp.int32, q.shape, 1)
    zero = jnp.zeros_like(q)
    qs = jnp.concatenate([jnp.where(lane < DIFF_DH, q, zero), jnp.where(lane >= DIFF_DH, q, zero)], axis=0)

    m_scr[...] = jnp.full_like(m_scr, NEG_BIG)
    acc_scr[...] = jnp.zeros_like(acc_scr)

    def scores(kc):
        return lax.dot_general(kc, qs, (((1,), (1,)), ((), ())), preferred_element_type=F32)

    def accumulate(s, vt):
        m_prev = m_scr[...]
        m_new = jnp.maximum(m_prev, jnp.max(s, axis=0, keepdims=True))
        alpha = jnp.exp2(m_prev - m_new)
        p = jnp.exp2(s - m_new).astype(BF16)
        acc_scr[...] = alpha * acc_scr[...] + jnp.dot(vt, p, preferred_element_type=F32)
        m_scr[...] = m_new

    @pl.when(qi < n_lat_q)
    def _():
        s = scores(kl_ref[0:tk, :])
        for c in range(n_lat_k):
            s_next = scores(kl_ref[(c + 1) * tk:(c + 2) * tk, :] if c + 1 < n_lat_k else kc_ref[...])
            accumulate(s, vt_scr[:, c * tk:(c + 1) * tk])
            s = s_next
        accumulate(s, vt_scr[:, N:N + M])

    @pl.when(qi >= n_lat_q)
    def _():
        accumulate(scores(kc_ref[...]), vt_scr[:, N:N + M])

    lp = lam_ref[...]
    lam = (jnp.exp(jnp.sum(lp[0:1] * lp[1:2], axis=1, keepdims=True))
           - jnp.exp(jnp.sum(lp[2:3] * lp[3:4], axis=1, keepdims=True)) + lam_init)
    acc = acc_scr[...]
    o = acc[:hd, :tq] / acc[hd:hd + 1, :tq] - lam * (acc[:hd, tq:] / acc[hd:hd + 1, tq:])
    ms = jnp.mean(o * o, axis=0, keepdims=True)
    r = o * (lax.rsqrt(ms + NORM_EPS) * (1.0 - lam_init))
    o_ref[...] = (r.T * nw_ref[...]).astype(BF16)


def _diff_call(P, lam_p, nw, *, B, N, M, need_ctx, lam_init):
    tq = _pick((N, M), 256)
    tk = _pick((N,), 512, 256)
    nql, nqc = N // tq, (M // tq if need_ctx else 0)
    R = B * N + (B * M if need_ctx else 0)
    hw = 2 * DIFF_DH
    cq, ck, cv = COL_DQ // hw, COL_DK // hw, COL_DV // hw

    def qrow(b, qi):
        return jnp.where(qi < nql, b * nql + qi, B * nql + b * nqc + (qi - nql))

    kern = functools.partial(_diff_kernel, tq=tq, tk=tk, n_lat_q=nql, lam_init=lam_init)
    return pl.pallas_call(
        kern,
        grid=(B, N_DIFF_HEADS, nql + nqc),
        in_specs=[
            pl.BlockSpec((4, DIFF_DH), lambda b, h, qi: (0, 0)),
            pl.BlockSpec((1, hw), lambda b, h, qi: (0, 0)),
            pl.BlockSpec((tq, hw), lambda b, h, qi: (qrow(b, qi), cq + h)),
            pl.BlockSpec((N, hw), lambda b, h, qi: (b, ck + h)),
            pl.BlockSpec((M, hw), lambda b, h, qi: (B * N // M + b, ck + h)),
            pl.BlockSpec((N, hw), lambda b, h, qi: (b, cv + h)),
            pl.BlockSpec((M, hw), lambda b, h, qi: (B * N // M + b, cv + h)),
        ],
        out_specs=pl.BlockSpec((tq, hw), lambda b, h, qi: (qrow(b, qi), h)),
        out_shape=jax.ShapeDtypeStruct((R, DIFF_WIDTH), BF16),
        scratch_shapes=[pltpu.VMEM((hw + HALO, N + M), BF16), pltpu.VMEM((1, 2 * tq), F32),
                        pltpu.VMEM((hw + HALO, 2 * tq), F32)],
        compiler_params=_cparams("parallel", "parallel", "arbitrary"),
        name="diff_attn",
    )(lam_p, nw, P, P, P, P, P)


def _win_kernel(sink_ref, q_ref, kp_ref, kc_ref, kn_ref, kx_ref, o_ref, *, nb, N, M):
    i = pl.program_id(1)
    wb, kw = WIN_BLOCK, WIN_KV_WIDTH
    q = q_ref[...]
    kcat = jnp.concatenate([kp_ref[:, :kw], kc_ref[:, :kw], kn_ref[:, :kw], kx_ref[:, :kw]], axis=0)
    vcat = jnp.concatenate([kp_ref[:, kw:], kc_ref[:, kw:], kn_ref[:, kw:], kx_ref[:, kw:]], axis=0)
    nk = 3 * wb + M
    rows = WIN_G * wb
    row = lax.broadcasted_iota(jnp.int32, (rows, nk), 0) % wb
    col = lax.broadcasted_iota(jnp.int32, (rows, nk), 1)
    kpos = (i - 1) * wb + col
    n_band = jnp.where(i < nb, N, 0)
    band_ok = (jnp.abs(col - wb - row) <= WINDOW) & (kpos >= 0) & (kpos < n_band)
    valid = band_ok | (col >= 3 * wb)
    lane = lax.broadcasted_iota(jnp.int32, (wb, kw), 1) // WIN_DH
    lane_o = lax.broadcasted_iota(jnp.int32, (rows, kw), 1) // WIN_DH
    grow = lax.broadcasted_iota(jnp.int32, (rows, 1), 0) // wb
    zero = jnp.zeros((wb, kw), BF16)
    acc = jnp.zeros((rows, kw), F32)
    for kv in range(WIN_KV_HEADS):
        qs = jnp.concatenate([jnp.where(lane == kv, q[:, g * kw:(g + 1) * kw], zero) for g in range(WIN_G)], axis=0)
        s = lax.dot_general(qs, kcat, (((1,), (1,)), ((), ())), preferred_element_type=F32)
        s = jnp.where(valid, s, NEG_BIG)
        sink = jnp.zeros((rows, 1), F32)
        for g in range(WIN_G):
            sink = jnp.where(grow == g, sink_ref[kv * WIN_G + g] * LOG2E, sink)
        m = jnp.maximum(jnp.max(s, axis=1, keepdims=True), sink)
        p = jnp.exp2(s - m)
        l = jnp.sum(p, axis=1, keepdims=True) + jnp.exp2(sink - m)
        o = jnp.dot(p.astype(BF16), vcat, preferred_element_type=F32) / l
        acc = jnp.where(lane_o == kv, o, acc)
    for g in range(WIN_G):
        o_ref[:, g * kw:(g + 1) * kw] = acc[g * wb:(g + 1) * wb].astype(BF16)


def _win_call(P, sink, *, B, N, M, need_ctx):
    wb = WIN_BLOCK
    nb, nbc = N // wb, (M // wb if need_ctx else 0)
    R = B * N + (B * M if need_ctx else 0)
    cq, ckv = COL_WQ // PW, COL_WKV // (2 * WIN_KV_WIDTH)

    def qrow(b, i):
        return jnp.where(i < nb, b * nb + i, B * nb + b * nbc + (i - nb))

    def band(b, i, d):
        return b * nb + jnp.clip(i + d, 0, nb - 1)

    kvw = 2 * WIN_KV_WIDTH
    kern = functools.partial(_win_kernel, nb=nb, N=N, M=M)
    grid_spec = pltpu.PrefetchScalarGridSpec(
        num_scalar_prefetch=1,
        grid=(B, nb + nbc),
        in_specs=[
            pl.BlockSpec((wb, PW), lambda b, i, s: (qrow(b, i), cq)),
            pl.BlockSpec((wb, kvw), lambda b, i, s: (band(b, i, -1), ckv)),
            pl.BlockSpec((wb, kvw), lambda b, i, s: (band(b, i, 0), ckv)),
            pl.BlockSpec((wb, kvw), lambda b, i, s: (band(b, i, 1), ckv)),
            pl.BlockSpec((M, kvw), lambda b, i, s: (B * N // M + b, ckv)),
        ],
        out_specs=pl.BlockSpec((wb, PW), lambda b, i, s: (qrow(b, i), 0)),
    )
    return pl.pallas_call(
        kern,
        grid_spec=grid_spec,
        out_shape=jax.ShapeDtypeStruct((R, WIN_WIDTH), BF16),
        compiler_params=_cparams("parallel", "arbitrary"),
        name="win_attn",
    )(sink, P, P, P, P, P)


def _seq_edges(i, R_lat, N, M):
    r0 = i * CONV_ROWS
    lat = r0 < R_lat
    first = jnp.where(lat, r0 % N == 0, (r0 - R_lat) % M == 0)
    last = jnp.where(lat, (r0 + CONV_ROWS) % N == 0, (r0 + CONV_ROWS - R_lat) % M == 0)
    return first, last


def _conv_taps(buf, main_ref, prev_ref, next_ref, w_ref, b_ref, first, last, k):
    buf[0:HALO] = jnp.where(first, 0.0, prev_ref[...].astype(F32))
    buf[HALO:HALO + CONV_ROWS] = main_ref[...].astype(F32)
    buf[HALO + CONV_ROWS:] = jnp.where(last, 0.0, next_ref[...].astype(F32))
    out = b_ref[...]
    for j in range(k):
        out = out + buf[pl.ds(HALO - k // 2 + j, CONV_ROWS), :] * w_ref[j:j + 1, :]
    return out


def _ssdconv_kernel(m_ref, p_ref, n_ref, w_ref, b_ref, o_ref, buf, *, R_lat, N, M):
    first, last = _seq_edges(pl.program_id(0), R_lat, N, M)
    o_ref[...] = _silu(_conv_taps(buf, m_ref, p_ref, n_ref, w_ref, b_ref, first, last, SSD_CONV_W)).astype(BF16)


def _ssdconv_call(P, w, b, *, B, N, M):
    T = P.shape[0]
    cr, W = CONV_ROWS, SSD_XBC
    hb = cr // HALO
    cb = COL_XBC // W
    nh = T // HALO
    kern = functools.partial(_ssdconv_kernel, R_lat=B * N, N=N, M=M)
    return pl.pallas_call(
        kern,
        grid=(T // cr,),
        in_specs=[
            pl.BlockSpec((cr, W), lambda i: (i, cb)),
            pl.BlockSpec((HALO, W), lambda i: (jnp.maximum(i * hb - 1, 0), cb)),
            pl.BlockSpec((HALO, W), lambda i: (jnp.minimum((i + 1) * hb, nh - 1), cb)),
            pl.BlockSpec((SSD_CONV_W, W), lambda i: (0, 0)),
            pl.BlockSpec((1, W), lambda i: (0, 0)),
        ],
        out_specs=pl.BlockSpec((cr, W), lambda i: (i, 0)),
        out_shape=jax.ShapeDtypeStruct((T, W), BF16),
        scratch_shapes=[pltpu.VMEM((cr + 2 * HALO, W), F32)],
        compiler_params=_cparams("parallel"),
        name="ssd_conv",
    )(P, P, P, w, b)


def _ffnconv_kernel(am_ref, ap_ref, an_ref, gm_ref, gp_ref, gn_ref, wa_ref, wg_ref, ba_ref, bg_ref, o_ref,
                    bufa, bufg, *, R_lat, N, M):
    first, last = _seq_edges(pl.program_id(0), R_lat, N, M)
    a = _conv_taps(bufa, am_ref, ap_ref, an_ref, wa_ref, ba_ref, first, last, FFN_CONV_W)
    g = _conv_taps(bufg, gm_ref, gp_ref, gn_ref, wg_ref, bg_ref, first, last, FFN_CONV_W)
    o_ref[...] = (_silu(a) * g).astype(BF16)


def _ffnconv_call(U, w, b, *, R, B, N, M):
    F2 = U.shape[1]
    F = F2 // 2
    cr = CONV_ROWS
    tc = _pick((F,), 1408, 512, 128)
    nc = F // tc
    hb = cr // HALO
    nh = R // HALO
    prev = lambda i: jnp.maximum(i * hb - 1, 0)
    nxt = lambda i: jnp.minimum((i + 1) * hb, nh - 1)
    kern = functools.partial(_ffnconv_kernel, R_lat=B * N, N=N, M=M)
    return pl.pallas_call(
        kern,
        grid=(R // cr, nc),
        in_specs=[
            pl.BlockSpec((cr, tc), lambda i, j: (i, j)),
            pl.BlockSpec((HALO, tc), lambda i, j: (prev(i), j)),
            pl.BlockSpec((HALO, tc), lambda i, j: (nxt(i), j)),
            pl.BlockSpec((cr, tc), lambda i, j: (i, j + nc)),
            pl.BlockSpec((HALO, tc), lambda i, j: (prev(i), j + nc)),
            pl.BlockSpec((HALO, tc), lambda i, j: (nxt(i), j + nc)),
            pl.BlockSpec((FFN_CONV_W, tc), lambda i, j: (0, j)),
            pl.BlockSpec((FFN_CONV_W, tc), lambda i, j: (0, j + nc)),
            pl.BlockSpec((1, tc), lambda i, j: (0, j)),
            pl.BlockSpec((1, tc), lambda i, j: (0, j + nc)),
        ],
        out_specs=pl.BlockSpec((cr, tc), lambda i, j: (i, j)),
        out_shape=jax.ShapeDtypeStruct((R, F), BF16),
        scratch_shapes=[pltpu.VMEM((cr + 2 * HALO, tc), F32)] * 2,
        compiler_params=_cparams("parallel", "parallel"),
        name="ffn_conv_gate",
    )(U, U, U, U, U, U, w, w, b, b)


def _cumsum_rows(a):
    n = a.shape[0]
    tri = (lax.broadcasted_iota(jnp.int32, (n, n), 0) >= lax.broadcasted_iota(jnp.int32, (n, n), 1)).astype(BF16)
    hi = a.astype(BF16)
    r1 = a - hi.astype(F32)
    mid = r1.astype(BF16)
    lo = (r1 - mid.astype(F32)).astype(BF16)
    dot = lambda v: jnp.dot(tri, v, preferred_element_type=F32)
    return dot(hi) + dot(mid) + dot(lo)


def _ssd_chunk(u_ref, dt_ref, bias_ref, alog_ref, s_ref, y_ref, *, reverse, lane0):
    ch, hp = SSD_CHUNK, SSD_HEAD_DIM
    gw = (SSD_HEADS // SSD_GROUPS) * hp
    hpg = SSD_HEADS // SSD_GROUPS
    x = dt_ref[...].astype(F32) + bias_ref[...]
    dt = jnp.maximum(x, 0.0) + jnp.log(1.0 + jnp.exp(-jnp.abs(x)))
    ad = dt * (-jnp.exp(alog_ref[...]))
    acum = _cumsum_rows(ad)
    total = acum[ch - 1:ch, :]
    if reverse:
        cvec = acum - ad
        e_off = jnp.exp(total - cvec)
        e_ws = jnp.exp(cvec)
    else:
        cvec = acum
        e_off = jnp.exp(acum)
        e_ws = jnp.exp(total - acum)
    e_tot = jnp.exp(total)
    cT = cvec.T
    a_xdt, a_xw = dt, dt * e_ws
    li = lax.broadcasted_iota(jnp.int32, (ch, ch), 0)
    si = lax.broadcasted_iota(jnp.int32, (ch, ch), 1)
    tri = (si >= li) if reverse else (li >= si)
    hsel = lax.broadcasted_iota(jnp.int32, (ch, gw), 1) // hp
    hsel1 = lax.broadcasted_iota(jnp.int32, (1, gw), 1) // hp

    def expand(src, g, sel):
        rows = src.shape[0]
        out = jnp.zeros((rows, gw), F32)
        for r in range(hpg):
            j = lane0 + g * hpg + r
            out = jnp.where(sel == r, jnp.broadcast_to(src[:, j:j + 1], (rows, gw)), out)
        return out

    for g in range(SSD_GROUPS):
        bg = u_ref[:, SSD_INNER + g * SSD_STATE:SSD_INNER + (g + 1) * SSD_STATE]
        cg = u_ref[:, SSD_INNER + (SSD_GROUPS + g) * SSD_STATE:SSD_INNER + (SSD_GROUPS + g + 1) * SSD_STATE]
        xs = u_ref[:, g * gw:(g + 1) * gw].astype(F32)
        cb = lax.dot_general(cg, bg, (((1,), (1,)), ((), ())), preferred_element_type=F32)
        xdt = (xs * expand(a_xdt, g, hsel)).astype(BF16)
        xw = (xs * expand(a_xw, g, hsel)).astype(BF16)
        ydiag = jnp.zeros((ch, gw), F32)
        for r in range(hpg):
            j = lane0 + g * hpg + r
            colv = jnp.broadcast_to(cvec[:, j:j + 1], (ch, ch))
            rowv = jnp.broadcast_to(cT[j:j + 1, :], (ch, ch))
            d = (rowv - colv) if reverse else (colv - rowv)
            lm = jnp.exp(jnp.where(tri, d, NEG_BIG))
            yh = jnp.dot((cb * lm).astype(BF16), xdt, preferred_element_type=F32)
            ydiag = jnp.where(hsel == r, yh, ydiag)
        sg = s_ref[:, g * gw:(g + 1) * gw]
        yoff = jnp.dot(cg, sg.astype(BF16), preferred_element_type=F32) * expand(e_off, g, hsel)
        y_ref[:, g * gw:(g + 1) * gw] = (ydiag + yoff).astype(BF16)
        s_ref[:, g * gw:(g + 1) * gw] = (
            sg * expand(e_tot, g, hsel1)
            + lax.dot_general(bg, xw, (((0,), (0,)), ((), ())), preferred_element_type=F32))


def _ssd_kernel(uf_ref, dtf_ref, ub_ref, dtb_ref, bias_ref, alog_ref, yf_ref, yb_ref, sf, sb):
    @pl.when(pl.program_id(1) == 0)
    def _():
        sf[...] = jnp.zeros_like(sf)
        sb[...] = jnp.zeros_like(sb)

    _ssd_chunk(uf_ref, dtf_ref, bias_ref, alog_ref, sf, yf_ref, reverse=False, lane0=0)
    _ssd_chunk(ub_ref, dtb_ref, bias_ref, alog_ref, sb, yb_ref, reverse=True, lane0=SSD_HEADS)


def _ssd_call(U, P, bias, alog, *, B, N, M):
    T = U.shape[0]
    ch = SSD_CHUNK
    ncl, ncc = N // ch, M // ch
    cdt = COL_DT // LANES

    def fwd(b, t):
        return jnp.where(t < ncc, B * ncl + b * ncc + t, b * ncl + (t - ncc))

    def bwd(b, t):
        return jnp.where(t < ncc, B * ncl + b * ncc + (ncc - 1 - t), b * ncl + (ncl - 1 - (t - ncc)))

    return pl.pallas_call(
        _ssd_kernel,
        grid=(B, ncl + ncc),
        in_specs=[
            pl.BlockSpec((ch, SSD_XBC), lambda b, t: (fwd(b, t), 0)),
            pl.BlockSpec((ch, LANES), lambda b, t: (fwd(b, t), cdt)),
            pl.BlockSpec((ch, SSD_XBC), lambda b, t: (bwd(b, t), 0)),
            pl.BlockSpec((ch, LANES), lambda b, t: (bwd(b, t), cdt)),
            pl.BlockSpec((1, LANES), lambda b, t: (0, 0)),
            pl.BlockSpec((1, LANES), lambda b, t: (0, 0)),
        ],
        out_specs=[
            pl.BlockSpec((ch, SSD_INNER), lambda b, t: (fwd(b, t), 0)),
            pl.BlockSpec((ch, SSD_INNER), lambda b, t: (bwd(b, t), 0)),
        ],
        out_shape=[jax.ShapeDtypeStruct((T, SSD_INNER), BF16)] * 2,
        scratch_shapes=[pltpu.VMEM((SSD_STATE, SSD_INNER), F32)] * 2,
        compiler_params=_cparams("parallel", "arbitrary"),
        name="ssd_scan",
    )(U, P, U, P, bias, alog)


def _merge_kernel(yd_ref, yf_ref, yb_ref, xs_ref, z_ref, yw_ref, gd_ref, gs_ref, gw_ref,
                  wd_ref, ws_ref, ww_ref, dsk_ref, sn_ref, o_ref):
    y = yf_ref[...].astype(F32) + yb_ref[...].astype(F32) + dsk_ref[...] * xs_ref[...].astype(F32)
    y = y * _silu(z_ref[...].astype(F32))
    ms = jnp.mean(y * y, axis=-1, keepdims=True)
    s = (y * lax.rsqrt(ms + NORM_EPS) * sn_ref[...]).astype(BF16)
    dot = lambda a, w: jnp.dot(a, w[...], preferred_element_type=F32)
    sig = lambda r: _sigmoid(r[...].astype(F32))
    acc = sig(gd_ref) * dot(yd_ref[...], wd_ref)
    acc = acc + sig(gs_ref) * dot(s, ws_ref)
    acc = acc + sig(gw_ref) * dot(yw_ref[...], ww_ref)
    o_ref[...] = acc.astype(BF16)


def _merge_call(YD, YF, YB, U, P, YW, wd, ws, ww, dsk, sn, *, R, D):
    tm = 256
    W = SSD_INNER
    cg = COL_GL // D
    row = lambda c: pl.BlockSpec((tm, W), lambda i: (i, c))
    gate = lambda c: pl.BlockSpec((tm, D), lambda i: (i, cg + c))
    wspec = pl.BlockSpec((W, D), lambda i: (0, 0))
    vec = pl.BlockSpec((1, W), lambda i: (0, 0))
    return pl.pallas_call(
        _merge_kernel,
        grid=(R // tm,),
        in_specs=[row(0), row(0), row(0), row(0), row(COL_SZ // W), row(0),
                  gate(0), gate(1), gate(2), wspec, wspec, wspec, vec, vec],
        out_specs=pl.BlockSpec((tm, D), lambda i: (i, 0)),
        out_shape=jax.ShapeDtypeStruct((R, D), BF16),
        compiler_params=_cparams("parallel"),
        name="branch_merge",
    )(YD, YF, YB, U, P, YW, P, P, P, wd, ws, ww, dsk, sn)


def _rope_tables(N, rows_ctx):
    hd = DIFF_DH
    pos = jnp.arange(N)
    row = (pos // GRID_W).astype(F32)
    colp = (pos % GRID_W).astype(F32)
    axis_dim = hd // 2
    inv = ROPE_BASE ** (-jnp.arange(0, axis_dim, 2, dtype=F32) / axis_dim)
    ang = jnp.concatenate([row[:, None] * inv, colp[:, None] * inv], axis=-1)
    lane = jnp.arange(LANES)
    cos = jnp.cos(ang)[:, lane % (hd // 2)]
    sin = jnp.sin(ang)[:, lane % (hd // 2)]
    first = (lane % hd) < hd // 2
    lat = jnp.stack([cos, jnp.where(first, -sin, 0.0), jnp.where(first, 0.0, sin)])
    ctx = jnp.stack([jnp.ones((rows_ctx, LANES), F32), jnp.zeros((rows_ctx, LANES), F32),
                     jnp.zeros((rows_ctx, LANES), F32)])
    k_tab = jnp.concatenate([lat, ctx], axis=1)
    q_tab = k_tab * (DIFF_DH ** -0.5 * LOG2E)
    return jnp.stack([q_tab, k_tab])


def _layout_w_in(w):
    D = w.shape[0]
    sizes = (DIFF_WIDTH, DIFF_WIDTH, DIFF_WIDTH, SSD_INNER, SSD_XBC, 2 * SSD_HEADS,
             WIN_WIDTH, WIN_KV_WIDTH, WIN_KV_WIDTH, 3 * D)
    parts, s = [], 0
    for z in sizes:
        parts.append(w[:, s:s + z])
        s += z
    dq, dk, dv, sz, sxbc, sdt, wq, wk, wv, gl = parts
    wq = wq.reshape(D, WIN_KV_HEADS, WIN_G, WIN_DH).transpose(0, 2, 1, 3).reshape(D, WIN_WIDTH)
    pad = jnp.zeros((D, PW - 2 * WIN_KV_WIDTH - 2 * SSD_HEADS), w.dtype)
    return jnp.concatenate([dq, dk, wq, wk, wv, sdt, pad, dv, sz, sxbc, gl], axis=1).astype(BF16)


def kernel(x, c, ctx, c_ctx, w_ada, b_ada, norm_g, w_in, diff_lambda, diff_norm, ssd_conv_w, ssd_conv_b,
           ssd_a_log, ssd_dt_bias, ssd_d, ssd_norm, win_sink, w_br_diff, w_br_ssd, w_br_win, w_out,
           ffn_w_up, ffn_conv_w, ffn_conv_b, ffn_w_down):
    B, N, D = x.shape
    M = ctx.shape[1]
    L = w_ada.shape[0]
    assert D == 2 * PW and P_WIDTH == COL_GL + 3 * D
    assert N % CONV_ROWS == 0 and M % CONV_ROWS == 0 and (B * N) % M == 0 and B + 1 <= 16
    T = B * N + B * M

    X = jnp.concatenate([x.reshape(B * N, D), ctx.reshape(B * M, D)], axis=0)
    cv = jnp.zeros((16, D), F32).at[:B].set(c).at[B].set(c_ctx)
    mod = _mod_call(cv, w_ada, b_ada)

    tm_in = _pick((N, B * M), 1024, 512, 256)
    tab = _rope_tables(N, tm_in)
    kinds = jnp.asarray(COL_KINDS, jnp.int32)
    pad_lanes = LANES - 2 * SSD_HEADS

    for l in range(L):
        need_ctx = l < L - 1
        lam_init = 0.8 - 0.6 * math.exp(-0.3 * l)
        R = T if need_ctx else B * N
        mod3 = mod[l].reshape(16, 1, 6 * D)
        kw = dict(B=B, N=N, M=M)

        P = _inproj_call(X, mod3, norm_g[l, 0:1], _layout_w_in(w_in[l]), tab, kinds, sc_part=1, sh_part=0, **kw)

        YD = _diff_call(P, diff_lambda[l], diff_norm[l].reshape(1, -1), need_ctx=need_ctx, lam_init=lam_init, **kw)
        YW = _win_call(P, win_sink[l], need_ctx=need_ctx, **kw)

        U = _ssdconv_call(P, ssd_conv_w[l], ssd_conv_b[l].reshape(1, -1), **kw)
        bias = jnp.pad(ssd_dt_bias[l].reshape(1, -1), ((0, 0), (0, pad_lanes)))
        alog = jnp.pad(ssd_a_log[l].reshape(1, -1), ((0, 0), (0, pad_lanes)))
        YF, YB = _ssd_call(U, P, bias, alog, **kw)

        ww = w_br_win[l].reshape(WIN_KV_HEADS, WIN_G, WIN_DH, D).transpose(1, 0, 2, 3).reshape(WIN_WIDTH, D)
        dsk = jnp.repeat(ssd_d[l], SSD_HEAD_DIM).reshape(1, -1)
        Z = _merge_call(YD, YF, YB, U, P, YW, w_br_diff[l].astype(BF16), w_br_ssd[l].astype(BF16),
                        ww.astype(BF16), dsk, ssd_norm[l].reshape(1, -1), R=R, D=D)
        X1 = _mnr_call(Z, w_out[l].astype(BF16), X, mod3, norm_g[l, 1:2], R=R, gate_part=2, tk=1024,
                       name="out_proj", **kw)

        UP = _ffnup_call(X1, mod3, norm_g[l, 2:3], ffn_w_up[l].astype(BF16), R=R, sc_part=4, sh_part=3, **kw)
        A = _ffnconv_call(UP, ffn_conv_w[l], ffn_conv_b[l].reshape(1, -1), R=R, **kw)
        X = _mnr_call(A, ffn_w_down[l].astype(BF16), X1, mod3, norm_g[l, 3:4], R=R, gate_part=5,
                      tk=_pick((A.shape[1],), 1408, 512, 128), name="ffn_down", **kw)

    return X[:B * N].reshape(B, N, D)
```

```python
import functools
import math

import jax
import jax.numpy as jnp
from jax import lax
from jax.experimental import pallas as pl
from jax.experimental.pallas import tpu as pltpu

F32 = jnp.float32
BF16 = jnp.bfloat16

GRID_W = 64
ROPE_BASE = 10000.0
NORM_EPS = 1e-6
N_DIFF_HEADS = 8
DIFF_DH = 64
DIFF_WIDTH = N_DIFF_HEADS * 2 * DIFF_DH
SSD_INNER = 1024
SSD_HEAD_DIM = 64
SSD_HEADS = 16
SSD_GROUPS = 4
SSD_STATE = 128
SSD_CONV_W = 5
SSD_CHUNK = 128
SSD_XBC = SSD_INNER + 2 * SSD_GROUPS * SSD_STATE
WIN_HEADS = 16
WIN_KV_HEADS = 4
WIN_G = WIN_HEADS // WIN_KV_HEADS
WIN_DH = 64
WINDOW = 128
WIN_BLOCK = 128
WIN_WIDTH = WIN_HEADS * WIN_DH
WIN_KV_WIDTH = WIN_KV_HEADS * WIN_DH
FFN_CONV_W = 3

LOG2E = 1.4426950408889634
NEG_BIG = -1e30

LANES = 128
HALO = 16
CONV_ROWS = 256
VMEM_LIMIT = 56 * 1024 * 1024

PW = 1024
COL_DQ = 0 * PW
COL_DK = 1 * PW
COL_WQ = 2 * PW
COL_WKV = 3 * PW
COL_DT = COL_WKV + 2 * WIN_KV_WIDTH
COL_DV = 4 * PW
COL_SZ = 5 * PW
COL_XBC = 6 * PW
COL_GL = 8 * PW
P_WIDTH = 14 * PW
KIND_NONE, KIND_Q, KIND_K, KIND_K256 = 0, 1, 2, 3
COL_KINDS = (KIND_Q, KIND_K, KIND_Q, KIND_K256) + (KIND_NONE,) * 10


def _cparams(*sem):
    return pltpu.CompilerParams(dimension_semantics=sem, vmem_limit_bytes=VMEM_LIMIT)


def _sigmoid(x):
    return 1.0 / (1.0 + jnp.exp(-x))


def _silu(x):
    return x * _sigmoid(x)


def _pick(n, *cands):
    for c in cands:
        if all(v % c == 0 for v in n):
            return c
    raise ValueError(f"no block size among {cands} divides {n}")


def _mod_kernel(cv_ref, w_ref, b_ref, o_ref):
    s = _silu(cv_ref[...]).astype(BF16)
    o_ref[0] = jnp.dot(s, w_ref[0].astype(BF16), preferred_element_type=F32) + b_ref[0]


def _mod_call(cv, w_ada, b_ada):
    L, D, W6 = w_ada.shape
    tn = 1024
    return pl.pallas_call(
        _mod_kernel,
        grid=(L, W6 // tn),
        in_specs=[
            pl.BlockSpec((16, D), lambda l, j: (0, 0)),
            pl.BlockSpec((1, D, tn), lambda l, j: (l, 0, j)),
            pl.BlockSpec((1, 1, tn), lambda l, j: (l, 0, j)),
        ],
        out_specs=pl.BlockSpec((1, 16, tn), lambda l, j: (l, 0, j)),
        out_shape=jax.ShapeDtypeStruct((L, 16, W6), F32),
        compiler_params=_cparams("parallel", "parallel"),
        name="adaln_mod",
    )(cv, w_ada, b_ada.reshape(L, 1, W6))


def _norm_mod(x, g, sc, sh):
    ms = jnp.mean(x * x, axis=-1, keepdims=True)
    y = x * lax.rsqrt(ms + NORM_EPS) * g
    return y * (1.0 + sc) + sh


def _rope(y, cos, sa, sb):
    w = y.shape[1]
    rep = w // LANES
    if rep > 1:
        cos, sa, sb = (jnp.tile(t, (1, rep)) for t in (cos, sa, sb))
    up = pltpu.roll(y, w - DIFF_DH // 2, axis=1)
    dn = pltpu.roll(y, DIFF_DH // 2, axis=1)
    return y * cos + up * sa + dn * sb


def _inproj_kernel(kind_ref, x_ref, sc_ref, sh_ref, g_ref, w_ref, tab_ref, o_ref, h_scr):
    j = pl.program_id(1)

    @pl.when(j == 0)
    def _():
        h_scr[...] = _norm_mod(x_ref[...], g_ref[...], sc_ref[0], sh_ref[0]).astype(BF16)

    y = jnp.dot(h_scr[...], w_ref[...], preferred_element_type=F32)
    kind = kind_ref[j]

    @pl.when(kind == KIND_NONE)
    def _():
        o_ref[...] = y.astype(BF16)

    @pl.when(jnp.logical_or(kind == KIND_Q, kind == KIND_K))
    def _():
        o_ref[...] = _rope(y, tab_ref[0, 0], tab_ref[0, 1], tab_ref[0, 2]).astype(BF16)

    @pl.when(kind == KIND_K256)
    def _():
        kw = WIN_KV_WIDTH
        o_ref[:, :kw] = _rope(y[:, :kw], tab_ref[0, 0], tab_ref[0, 1], tab_ref[0, 2]).astype(BF16)
        o_ref[:, kw:] = y[:, kw:].astype(BF16)


def _inproj_call(X, mod3, g, W, tab, kinds, *, B, N, M, sc_part, sh_part):
    T, D = X.shape
    tm = _pick((N, B * M), 1024, 512, 256)
    nlat = N // tm
    grp = lambda i: jnp.minimum((i * tm) // N, B)
    rblk = lambda i: jnp.where(i < B * nlat, i % nlat, nlat)
    ksel = lambda k: jnp.where(jnp.logical_or(k == KIND_K, k == KIND_K256), 1, 0)
    grid_spec = pltpu.PrefetchScalarGridSpec(
        num_scalar_prefetch=1,
        grid=(T // tm, P_WIDTH // PW),
        in_specs=[
            pl.BlockSpec((tm, D), lambda i, j, kr: (i, 0)),
            pl.BlockSpec((1, 1, D), lambda i, j, kr: (grp(i), 0, sc_part)),
            pl.BlockSpec((1, 1, D), lambda i, j, kr: (grp(i), 0, sh_part)),
            pl.BlockSpec((1, D), lambda i, j, kr: (0, 0)),
            pl.BlockSpec((D, PW), lambda i, j, kr: (0, j)),
            pl.BlockSpec((1, 3, tm, LANES), lambda i, j, kr: (ksel(kr[j]), 0, rblk(i), 0)),
        ],
        out_specs=pl.BlockSpec((tm, PW), lambda i, j, kr: (i, j)),
        scratch_shapes=[pltpu.VMEM((tm, D), BF16)],
    )
    return pl.pallas_call(
        _inproj_kernel,
        grid_spec=grid_spec,
        out_shape=jax.ShapeDtypeStruct((T, P_WIDTH), BF16),
        compiler_params=_cparams("parallel", "arbitrary"),
        name="in_proj",
    )(kinds, X, mod3, mod3, g, W, tab)


def _ffnup_kernel(x_ref, xp_ref, xn_ref, sc_ref, sh_ref, g_ref, wa_ref, wg_ref, cwa_ref, cwg_ref, ba_ref, bg_ref,
                  edge_ref, o_ref, h_scr, bufa, bufg):
    tm = x_ref.shape[0]

    @pl.when(pl.program_id(1) == 0)
    def _():
        nm = lambda r: _norm_mod(r[...], g_ref[...], sc_ref[0], sh_ref[0]).astype(BF16)
        h_scr[0:HALO] = nm(xp_ref)
        h_scr[HALO:HALO + tm] = nm(x_ref)
        h_scr[HALO + tm:] = nm(xn_ref)

    rep = o_ref.shape[1] // LANES
    prev_ok = jnp.tile(edge_ref[0], (1, rep))
    next_ok = jnp.tile(edge_ref[1], (1, rep))

    def conv(buf, w_ref, cw_ref, b_ref):
        buf[...] = jnp.dot(h_scr[...], w_ref[...], preferred_element_type=F32)
        taps = [buf[pl.ds(HALO - 1 + j, tm), :] for j in range(FFN_CONV_W)]
        out = b_ref[...] + (taps[0] * prev_ok) * cw_ref[0:1, :]
        out = out + taps[1] * cw_ref[1:2, :]
        return out + (taps[2] * next_ok) * cw_ref[2:3, :]

    a = conv(bufa, wa_ref, cwa_ref, ba_ref)
    g = conv(bufg, wg_ref, cwg_ref, bg_ref)
    o_ref[...] = (_silu(a) * g).astype(BF16)


def _ffnup_call(X, mod3, g, W, cw, cb, edge, *, R, B, N, M, sc_part, sh_part):
    D = X.shape[1]
    F = W.shape[1] // 2
    tm = _pick((N, B * M), 1024, 512, 256)
    tn = _pick((F,), 512, 128)
    nc = F // tn
    hb = tm // HALO
    nh = R // HALO
    grp = lambda i: jnp.minimum((i * tm) // N, B)
    col = lambda shape, off: pl.BlockSpec(shape, lambda i, j: (0, j + off))
    return pl.pallas_call(
        _ffnup_kernel,
        grid=(R // tm, nc),
        in_specs=[
            pl.BlockSpec((tm, D), lambda i, j: (i, 0)),
            pl.BlockSpec((HALO, D), lambda i, j: (jnp.maximum(i * hb - 1, 0), 0)),
            pl.BlockSpec((HALO, D), lambda i, j: (jnp.minimum((i + 1) * hb, nh - 1), 0)),
            pl.BlockSpec((1, 1, D), lambda i, j: (grp(i), 0, sc_part)),
            pl.BlockSpec((1, 1, D), lambda i, j: (grp(i), 0, sh_part)),
            pl.BlockSpec((1, D), lambda i, j: (0, 0)),
            col((D, tn), 0), col((D, tn), nc),
            col((FFN_CONV_W, tn), 0), col((FFN_CONV_W, tn), nc),
            col((1, tn), 0), col((1, tn), nc),
            pl.BlockSpec((2, tm, LANES), lambda i, j: (0, i, 0)),
        ],
        out_specs=pl.BlockSpec((tm, tn), lambda i, j: (i, j)),
        out_shape=jax.ShapeDtypeStruct((R, F), BF16),
        scratch_shapes=[pltpu.VMEM((tm + 2 * HALO, D), BF16), pltpu.VMEM((tm + 2 * HALO, tn), F32),
                        pltpu.VMEM((tm + 2 * HALO, tn), F32)],
        compiler_params=_cparams("parallel", "arbitrary"),
        name="ffn_up_conv_gate",
    )(X, X, X, mod3, mod3, g, W, W, cw, cw, cb, cb, edge)


def _mnr_kernel(a_ref, w_ref, x_ref, gate_ref, g_ref, o_ref, acc):
    k = pl.program_id(1)

    @pl.when(k == 0)
    def _():
        acc[...] = jnp.zeros_like(acc)

    acc[...] += jnp.dot(a_ref[...], w_ref[...], preferred_element_type=F32)

    @pl.when(k == pl.num_programs(1) - 1)
    def _():
        y = acc[...]
        ms = jnp.mean(y * y, axis=-1, keepdims=True)
        r = y * lax.rsqrt(ms + NORM_EPS) * g_ref[...]
        o_ref[...] = x_ref[...] + gate_ref[0] * r


def _mnr_call(A, W, X, mod3, g, *, R, B, N, M, gate_part, tk, name):
    K, D = W.shape
    tm = _pick((N, B * M), 512, 256)
    grp = lambda i: jnp.minimum((i * tm) // N, B)
    return pl.pallas_call(
        _mnr_kernel,
        grid=(R // tm, K // tk),
        in_specs=[
            pl.BlockSpec((tm, tk), lambda i, k: (i, k)),
            pl.BlockSpec((tk, D), lambda i, k: (k, 0)),
            pl.BlockSpec((tm, D), lambda i, k: (i, 0)),
            pl.BlockSpec((1, 1, D), lambda i, k: (grp(i), 0, gate_part)),
            pl.BlockSpec((1, D), lambda i, k: (0, 0)),
        ],
        out_specs=pl.BlockSpec((tm, D), lambda i, k: (i, 0)),
        out_shape=jax.ShapeDtypeStruct((R, D), F32),
        scratch_shapes=[pltpu.VMEM((tm, D), F32)],
        compiler_params=_cparams("parallel", "arbitrary"),
        name=name,
    )(A, W, X, mod3, g)


def _diff_kernel(lam_ref, nw_ref, q_ref, *refs, tq, tk, with_lat, lam_init):
    if with_lat:
        kl_ref, kc_ref, vl_ref, vc_ref, o_ref, vt_scr, m_scr, acc_scr = refs
        N = kl_ref.shape[0]
    else:
        kc_ref, vc_ref, o_ref, vt_scr, m_scr, acc_scr = refs
        N = 0
    M = kc_ref.shape[0]
    hd = 2 * DIFF_DH
    n_lat_k = N // tk

    @pl.when(pl.program_id(2) == 0)
    def _():
        for c in range(n_lat_k):
            vt_scr[0:hd, c * tk:(c + 1) * tk] = vl_ref[c * tk:(c + 1) * tk, :].astype(F32).T.astype(BF16)
        vt_scr[0:hd, N:N + M] = vc_ref[...].astype(F32).T.astype(BF16)
        vt_scr[hd:, :] = jnp.ones((vt_scr.shape[0] - hd, N + M), BF16)

    q = q_ref[...]
    lane = lax.broadcasted_iota(jnp.int32, q.shape, 1)
    zero = jnp.zeros_like(q)
    qs = jnp.concatenate([jnp.where(lane < DIFF_DH, q, zero), jnp.where(lane >= DIFF_DH, q, zero)], axis=0)

    m_scr[...] = jnp.full_like(m_scr, NEG_BIG)
    acc_scr[...] = jnp.zeros_like(acc_scr)

    def scores(kc):
        return lax.dot_general(kc, qs, (((1,), (1,)), ((), ())), preferred_element_type=F32)

    def accumulate(s, vt):
        m_prev = m_scr[...]
        m_new = jnp.maximum(m_prev, jnp.max(s, axis=0, keepdims=True))
        alpha = jnp.exp2(m_prev - m_new)
        p = jnp.exp2(s - m_new).astype(BF16)
        acc_scr[...] = alpha * acc_scr[...] + jnp.dot(vt, p, preferred_element_type=F32)
        m_scr[...] = m_new

    s = scores(kl_ref[0:tk, :] if with_lat else kc_ref[...])
    for c in range(n_lat_k):
        s_next = scores(kl_ref[(c + 1) * tk:(c + 2) * tk, :] if c + 1 < n_lat_k else kc_ref[...])
        accumulate(s, vt_scr[:, c * tk:(c + 1) * tk])
        s = s_next
    accumulate(s, vt_scr[:, N:N + M])

    lp = lam_ref[...]
    lam = (jnp.exp(jnp.sum(lp[0:1] * lp[1:2], axis=1, keepdims=True))
           - jnp.exp(jnp.sum(lp[2:3] * lp[3:4], axis=1, keepdims=True)) + lam_init)
    acc = acc_scr[...]
    o = acc[:hd, :tq] / acc[hd:hd + 1, :tq] - lam * (acc[:hd, tq:] / acc[hd:hd + 1, tq:])
    ms = jnp.mean(o * o, axis=0, keepdims=True)
    r = o * (lax.rsqrt(ms + NORM_EPS) * (1.0 - lam_init))
    o_ref[...] = (r.T * nw_ref[...]).astype(BF16)


def _diff_call(P, lam_p, nw, *, B, N, M, with_lat, lam_init):
    tk = _pick((N,), 256)
    hw = 2 * DIFF_DH
    cq, ck, cv = COL_DQ // hw, COL_DK // hw, COL_DV // hw
    ctx_blk = B * N // M
    small = lambda: pl.BlockSpec((4, DIFF_DH), lambda b, h, qi: (0, 0))
    lat = lambda col: pl.BlockSpec((N, hw), lambda b, h, qi: (b, col + h))
    ctx = lambda col: pl.BlockSpec((M, hw), lambda b, h, qi: (ctx_blk + b, col + h))
    if with_lat:
        tq = _pick((N,), 512, 256)
        nq, q0, keys = N // tq, 0, N + M
        kv_specs = [lat(ck), ctx(ck), lat(cv), ctx(cv)]
    else:
        tq = M
        nq, q0, keys = 1, B * N // tq, M
        kv_specs = [ctx(ck), ctx(cv)]
    kern = functools.partial(_diff_kernel, tq=tq, tk=tk, with_lat=with_lat, lam_init=lam_init)
    return pl.pallas_call(
        kern,
        grid=(B, N_DIFF_HEADS, nq),
        in_specs=[
            small(),
            pl.BlockSpec((1, hw), lambda b, h, qi: (0, 0)),
            pl.BlockSpec((tq, hw), lambda b, h, qi: (q0 + b * nq + qi, cq + h)),
        ] + kv_specs,
        out_specs=pl.BlockSpec((tq, hw), lambda b, h, qi: (b * nq + qi, h)),
        out_shape=jax.ShapeDtypeStruct((B * nq * tq, DIFF_WIDTH), BF16),
        scratch_shapes=[pltpu.VMEM((hw + HALO, keys), BF16), pltpu.VMEM((1, 2 * tq), F32),
                        pltpu.VMEM((hw + HALO, 2 * tq), F32)],
        compiler_params=_cparams("parallel", "parallel", "arbitrary"),
        name="diff_attn" if with_lat else "diff_attn_ctx",
    )(lam_p, nw, P, *([P] * len(kv_specs)))


def _win_kernel(sink_ref, q_ref, kp_ref, kc_ref, kn_ref, kx_ref, o_ref, *, nb, N, M):
    i = pl.program_id(1)
    wb, kw = WIN_BLOCK, WIN_KV_WIDTH
    q = q_ref[...]
    kcat = jnp.concatenate([kp_ref[:, :kw], kc_ref[:, :kw], kn_ref[:, :kw], kx_ref[:, :kw]], axis=0)
    vcat = jnp.concatenate([kp_ref[:, kw:], kc_ref[:, kw:], kn_ref[:, kw:], kx_ref[:, kw:]], axis=0)
    nk = 3 * wb + M
    rows = WIN_G * wb
    row = lax.broadcasted_iota(jnp.int32, (rows, nk), 0) % wb
    col = lax.broadcasted_iota(jnp.int32, (rows, nk), 1)
    kpos = (i - 1) * wb + col
    n_band = jnp.where(i < nb, N, 0)
    band_ok = (jnp.abs(col - wb - row) <= WINDOW) & (kpos >= 0) & (kpos < n_band)
    valid = band_ok | (col >= 3 * wb)
    lane = lax.broadcasted_iota(jnp.int32, (wb, kw), 1) // WIN_DH
    lane_o = lax.broadcasted_iota(jnp.int32, (rows, kw), 1) // WIN_DH
    grow = lax.broadcasted_iota(jnp.int32, (rows, 1), 0) // wb
    zero = jnp.zeros((wb, kw), BF16)
    acc = jnp.zeros((rows, kw), F32)
    for kv in range(WIN_KV_HEADS):
        qs = jnp.concatenate([jnp.where(lane == kv, q[:, g * kw:(g + 1) * kw], zero) for g in range(WIN_G)], axis=0)
        s = lax.dot_general(qs, kcat, (((1,), (1,)), ((), ())), preferred_element_type=F32)
        s = jnp.where(valid, s, NEG_BIG)
        sink = jnp.zeros((rows, 1), F32)
        for g in range(WIN_G):
            sink = jnp.where(grow == g, sink_ref[kv * WIN_G + g] * LOG2E, sink)
        m = jnp.maximum(jnp.max(s, axis=1, keepdims=True), sink)
        p = jnp.exp2(s - m)
        l = jnp.sum(p, axis=1, keepdims=True) + jnp.exp2(sink - m)
        o = jnp.dot(p.astype(BF16), vcat, preferred_element_type=F32) / l
        acc = jnp.where(lane_o == kv, o, acc)
    for g in range(WIN_G):
        o_ref[:, g * kw:(g + 1) * kw] = acc[g * wb:(g + 1) * wb].astype(BF16)


def _win_call(P, sink, *, B, N, M, need_ctx):
    wb = WIN_BLOCK
    nb, nbc = N // wb, (M // wb if need_ctx else 0)
    R = B * N + (B * M if need_ctx else 0)
    cq, ckv = COL_WQ // PW, COL_WKV // (2 * WIN_KV_WIDTH)

    def qrow(b, i):
        return jnp.where(i < nb, b * nb + i, B * nb + b * nbc + (i - nb))

    def band(b, i, d):
        return b * nb + jnp.clip(i + d, 0, nb - 1)

    kvw = 2 * WIN_KV_WIDTH
    kern = functools.partial(_win_kernel, nb=nb, N=N, M=M)
    grid_spec = pltpu.PrefetchScalarGridSpec(
        num_scalar_prefetch=1,
        grid=(B, nb + nbc),
        in_specs=[
            pl.BlockSpec((wb, PW), lambda b, i, s: (qrow(b, i), cq)),
            pl.BlockSpec((wb, kvw), lambda b, i, s: (band(b, i, -1), ckv)),
            pl.BlockSpec((wb, kvw), lambda b, i, s: (band(b, i, 0), ckv)),
            pl.BlockSpec((wb, kvw), lambda b, i, s: (band(b, i, 1), ckv)),
            pl.BlockSpec((M, kvw), lambda b, i, s: (B * N // M + b, ckv)),
        ],
        out_specs=pl.BlockSpec((wb, PW), lambda b, i, s: (qrow(b, i), 0)),
    )
    return pl.pallas_call(
        kern,
        grid_spec=grid_spec,
        out_shape=jax.ShapeDtypeStruct((R, WIN_WIDTH), BF16),
        compiler_params=_cparams("parallel", "arbitrary"),
        name="win_attn",
    )(sink, P, P, P, P, P)


def _seq_edges(i, R_lat, N, M):
    r0 = i * CONV_ROWS
    lat = r0 < R_lat
    first = jnp.where(lat, r0 % N == 0, (r0 - R_lat) % M == 0)
    last = jnp.where(lat, (r0 + CONV_ROWS) % N == 0, (r0 + CONV_ROWS - R_lat) % M == 0)
    return first, last


def _conv_taps(buf, main_ref, prev_ref, next_ref, w_ref, b_ref, first, last, k):
    buf[0:HALO] = jnp.where(first, 0.0, prev_ref[...].astype(F32))
    buf[HALO:HALO + CONV_ROWS] = main_ref[...].astype(F32)
    buf[HALO + CONV_ROWS:] = jnp.where(last, 0.0, next_ref[...].astype(F32))
    out = b_ref[...]
    for j in range(k):
        out = out + buf[pl.ds(HALO - k // 2 + j, CONV_ROWS), :] * w_ref[j:j + 1, :]
    return out


def _ssdconv_kernel(m_ref, p_ref, n_ref, w_ref, b_ref, o_ref, buf, *, R_lat, N, M):
    first, last = _seq_edges(pl.program_id(0), R_lat, N, M)
    o_ref[...] = _silu(_conv_taps(buf, m_ref, p_ref, n_ref, w_ref, b_ref, first, last, SSD_CONV_W)).astype(BF16)


def _ssdconv_call(P, w, b, *, B, N, M):
    T = P.shape[0]
    cr, W = CONV_ROWS, SSD_XBC
    hb = cr // HALO
    cb = COL_XBC // W
    nh = T // HALO
    kern = functools.partial(_ssdconv_kernel, R_lat=B * N, N=N, M=M)
    return pl.pallas_call(
        kern,
        grid=(T // cr,),
        in_specs=[
            pl.BlockSpec((cr, W), lambda i: (i, cb)),
            pl.BlockSpec((HALO, W), lambda i: (jnp.maximum(i * hb - 1, 0), cb)),
            pl.BlockSpec((HALO, W), lambda i: (jnp.minimum((i + 1) * hb, nh - 1), cb)),
            pl.BlockSpec((SSD_CONV_W, W), lambda i: (0, 0)),
            pl.BlockSpec((1, W), lambda i: (0, 0)),
        ],
        out_specs=pl.BlockSpec((cr, W), lambda i: (i, 0)),
        out_shape=jax.ShapeDtypeStruct((T, W), BF16),
        scratch_shapes=[pltpu.VMEM((cr + 2 * HALO, W), F32)],
        compiler_params=_cparams("parallel"),
        name="ssd_conv",
    )(P, P, P, w, b)


def _cumsum_rows(a):
    n = a.shape[0]
    tri = (lax.broadcasted_iota(jnp.int32, (n, n), 0) >= lax.broadcasted_iota(jnp.int32, (n, n), 1)).astype(BF16)
    hi = a.astype(BF16)
    r1 = a - hi.astype(F32)
    mid = r1.astype(BF16)
    lo = (r1 - mid.astype(F32)).astype(BF16)
    dot = lambda v: jnp.dot(tri, v, preferred_element_type=F32)
    return dot(hi) + dot(mid) + dot(lo)


def _ssd_chunk(u_ref, dt_ref, bias_ref, alog_ref, s_ref, y_ref, *, reverse, lane0):
    ch, hp = SSD_CHUNK, SSD_HEAD_DIM
    gw = (SSD_HEADS // SSD_GROUPS) * hp
    hpg = SSD_HEADS // SSD_GROUPS
    x = dt_ref[...].astype(F32) + bias_ref[...]
    dt = jnp.maximum(x, 0.0) + jnp.log(1.0 + jnp.exp(-jnp.abs(x)))
    ad = dt * (-jnp.exp(alog_ref[...]))
    acum = _cumsum_rows(ad)
    total = acum[ch - 1:ch, :]
    if reverse:
        cvec = acum - ad
        e_off = jnp.exp(total - cvec)
        e_ws = jnp.exp(cvec)
    else:
        cvec = acum
        e_off = jnp.exp(acum)
        e_ws = jnp.exp(total - acum)
    e_tot = jnp.exp(total)
    cT = cvec.T
    a_xdt, a_xw = dt, dt * e_ws
    li = lax.broadcasted_iota(jnp.int32, (ch, ch), 0)
    si = lax.broadcasted_iota(jnp.int32, (ch, ch), 1)
    tri = (si >= li) if reverse else (li >= si)
    hsel = lax.broadcasted_iota(jnp.int32, (ch, gw), 1) // hp
    hsel1 = lax.broadcasted_iota(jnp.int32, (1, gw), 1) // hp

    def expand(src, g, sel):
        rows = src.shape[0]
        out = jnp.zeros((rows, gw), F32)
        for r in range(hpg):
            j = lane0 + g * hpg + r
            out = jnp.where(sel == r, jnp.broadcast_to(src[:, j:j + 1], (rows, gw)), out)
        return out

    for g in range(SSD_GROUPS):
        bg = u_ref[:, SSD_INNER + g * SSD_STATE:SSD_INNER + (g + 1) * SSD_STATE]
        cg = u_ref[:, SSD_INNER + (SSD_GROUPS + g) * SSD_STATE:SSD_INNER + (SSD_GROUPS + g + 1) * SSD_STATE]
        xs = u_ref[:, g * gw:(g + 1) * gw].astype(F32)
        cb = lax.dot_general(cg, bg, (((1,), (1,)), ((), ())), preferred_element_type=F32)
        xdt = (xs * expand(a_xdt, g, hsel)).astype(BF16)
        xw = (xs * expand(a_xw, g, hsel)).astype(BF16)
        ydiag = jnp.zeros((ch, gw), F32)
        for r in range(hpg):
            j = lane0 + g * hpg + r
            colv = jnp.broadcast_to(cvec[:, j:j + 1], (ch, ch))
            rowv = jnp.broadcast_to(cT[j:j + 1, :], (ch, ch))
            d = (rowv - colv) if reverse else (colv - rowv)
            lm = jnp.exp(jnp.where(tri, d, NEG_BIG))
            yh = jnp.dot((cb * lm).astype(BF16), xdt, preferred_element_type=F32)
            ydiag = jnp.where(hsel == r, yh, ydiag)
        sg = s_ref[:, g * gw:(g + 1) * gw]
        yoff = jnp.dot(cg, sg.astype(BF16), preferred_element_type=F32) * expand(e_off, g, hsel)
        y_ref[:, g * gw:(g + 1) * gw] = (ydiag + yoff).astype(BF16)
        s_ref[:, g * gw:(g + 1) * gw] = (
            sg * expand(e_tot, g, hsel1)
            + lax.dot_general(bg, xw, (((0,), (0,)), ((), ())), preferred_element_type=F32))


def _ssd_kernel(uf_ref, dtf_ref, ub_ref, dtb_ref, bias_ref, alog_ref, yf_ref, yb_ref, sf, sb):
    @pl.when(pl.program_id(1) == 0)
    def _():
        sf[...] = jnp.zeros_like(sf)
        sb[...] = jnp.zeros_like(sb)

    _ssd_chunk(uf_ref, dtf_ref, bias_ref, alog_ref, sf, yf_ref, reverse=False, lane0=0)
    _ssd_chunk(ub_ref, dtb_ref, bias_ref, alog_ref, sb, yb_ref, reverse=True, lane0=SSD_HEADS)


def _ssd_call(U, P, bias, alog, *, B, N, M):
    T = U.shape[0]
    ch = SSD_CHUNK
    ncl, ncc = N // ch, M // ch
    cdt = COL_DT // LANES

    def fwd(b, t):
        return jnp.where(t < ncc, B * ncl + b * ncc + t, b * ncl + (t - ncc))

    def bwd(b, t):
        return jnp.where(t < ncc, B * ncl + b * ncc + (ncc - 1 - t), b * ncl + (ncl - 1 - (t - ncc)))

    return pl.pallas_call(
        _ssd_kernel,
        grid=(B, ncl + ncc),
        in_specs=[
            pl.BlockSpec((ch, SSD_XBC), lambda b, t: (fwd(b, t), 0)),
            pl.BlockSpec((ch, LANES), lambda b, t: (fwd(b, t), cdt)),
            pl.BlockSpec((ch, SSD_XBC), lambda b, t: (bwd(b, t), 0)),
            pl.BlockSpec((ch, LANES), lambda b, t: (bwd(b, t), cdt)),
            pl.BlockSpec((1, LANES), lambda b, t: (0, 0)),
            pl.BlockSpec((1, LANES), lambda b, t: (0, 0)),
        ],
        out_specs=[
            pl.BlockSpec((ch, SSD_INNER), lambda b, t: (fwd(b, t), 0)),
            pl.BlockSpec((ch, SSD_INNER), lambda b, t: (bwd(b, t), 0)),
        ],
        out_shape=[jax.ShapeDtypeStruct((T, SSD_INNER), BF16)] * 2,
        scratch_shapes=[pltpu.VMEM((SSD_STATE, SSD_INNER), F32)] * 2,
        compiler_params=_cparams("parallel", "arbitrary"),
        name="ssd_scan",
    )(U, P, U, P, bias, alog)


def _merge_kernel(yd_ref, yf_ref, yb_ref, xs_ref, z_ref, yw_ref, gd_ref, gs_ref, gw_ref,
                  wd_ref, ws_ref, ww_ref, dsk_ref, sn_ref, o_ref):
    y = yf_ref[...].astype(F32) + yb_ref[...].astype(F32) + dsk_ref[...] * xs_ref[...].astype(F32)
    y = y * _silu(z_ref[...].astype(F32))
    ms = jnp.mean(y * y, axis=-1, keepdims=True)
    s = (y * lax.rsqrt(ms + NORM_EPS) * sn_ref[...]).astype(BF16)
    dot = lambda a, w: jnp.dot(a, w[...], preferred_element_type=F32)
    sig = lambda r: _sigmoid(r[...].astype(F32))
    acc = sig(gd_ref) * dot(yd_ref[...], wd_ref)
    acc = acc + sig(gs_ref) * dot(s, ws_ref)
    acc = acc + sig(gw_ref) * dot(yw_ref[...], ww_ref)
    o_ref[...] = acc.astype(BF16)


def _merge_call(YD, YF, YB, U, P, YW, wd, ws, ww, dsk, sn, *, R, D):
    tm = 256
    W = SSD_INNER
    cg = COL_GL // D
    row = lambda c: pl.BlockSpec((tm, W), lambda i: (i, c))
    gate = lambda c: pl.BlockSpec((tm, D), lambda i: (i, cg + c))
    wspec = pl.BlockSpec((W, D), lambda i: (0, 0))
    vec = pl.BlockSpec((1, W), lambda i: (0, 0))
    return pl.pallas_call(
        _merge_kernel,
        grid=(R // tm,),
        in_specs=[row(0), row(0), row(0), row(0), row(COL_SZ // W), row(0),
                  gate(0), gate(1), gate(2), wspec, wspec, wspec, vec, vec],
        out_specs=pl.BlockSpec((tm, D), lambda i: (i, 0)),
        out_shape=jax.ShapeDtypeStruct((R, D), BF16),
        compiler_params=_cparams("parallel"),
        name="branch_merge",
    )(YD, YF, YB, U, P, YW, P, P, P, wd, ws, ww, dsk, sn)


def _rope_tables(N, rows_ctx):
    hd = DIFF_DH
    pos = jnp.arange(N)
    row = (pos // GRID_W).astype(F32)
    colp = (pos % GRID_W).astype(F32)
    axis_dim = hd // 2
    inv = ROPE_BASE ** (-jnp.arange(0, axis_dim, 2, dtype=F32) / axis_dim)
    ang = jnp.concatenate([row[:, None] * inv, colp[:, None] * inv], axis=-1)
    lane = jnp.arange(LANES)
    cos = jnp.cos(ang)[:, lane % (hd // 2)]
    sin = jnp.sin(ang)[:, lane % (hd // 2)]
    first = (lane % hd) < hd // 2
    lat = jnp.stack([cos, jnp.where(first, -sin, 0.0), jnp.where(first, 0.0, sin)])
    ctx = jnp.stack([jnp.ones((rows_ctx, LANES), F32), jnp.zeros((rows_ctx, LANES), F32),
                     jnp.zeros((rows_ctx, LANES), F32)])
    k_tab = jnp.concatenate([lat, ctx], axis=1)
    q_tab = k_tab * (DIFF_DH ** -0.5 * LOG2E)
    return jnp.stack([q_tab, k_tab])


def _edge_table(B, N, M):
    pos = jnp.concatenate([jnp.tile(jnp.arange(N), B), jnp.tile(jnp.arange(M), B)])
    last = jnp.concatenate([jnp.full((B * N,), N - 1), jnp.full((B * M,), M - 1)])
    t = jnp.stack([pos != 0, pos != last]).astype(F32)
    return jnp.broadcast_to(t[:, :, None], t.shape + (LANES,))


def _layout_w_in(w):
    D = w.shape[0]
    sizes = (DIFF_WIDTH, DIFF_WIDTH, DIFF_WIDTH, SSD_INNER, SSD_XBC, 2 * SSD_HEADS,
             WIN_WIDTH, WIN_KV_WIDTH, WIN_KV_WIDTH, 3 * D)
    parts, s = [], 0
    for z in sizes:
        parts.append(w[:, s:s + z])
        s += z
    dq, dk, dv, sz, sxbc, sdt, wq, wk, wv, gl = parts
    wq = wq.reshape(D, WIN_KV_HEADS, WIN_G, WIN_DH).transpose(0, 2, 1, 3).reshape(D, WIN_WIDTH)
    pad = jnp.zeros((D, PW - 2 * WIN_KV_WIDTH - 2 * SSD_HEADS), w.dtype)
    return jnp.concatenate([dq, dk, wq, wk, wv, sdt, pad, dv, sz, sxbc, gl], axis=1).astype(BF16)


def kernel(x, c, ctx, c_ctx, w_ada, b_ada, norm_g, w_in, diff_lambda, diff_norm, ssd_conv_w, ssd_conv_b,
           ssd_a_log, ssd_dt_bias, ssd_d, ssd_norm, win_sink, w_br_diff, w_br_ssd, w_br_win, w_out,
           ffn_w_up, ffn_conv_w, ffn_conv_b, ffn_w_down):
    B, N, D = x.shape
    M = ctx.shape[1]
    L = w_ada.shape[0]
    assert D == 2 * PW and P_WIDTH == COL_GL + 3 * D
    assert N % CONV_ROWS == 0 and M % CONV_ROWS == 0 and (B * N) % M == 0 and B + 1 <= 16
    T = B * N + B * M

    X = jnp.concatenate([x.reshape(B * N, D), ctx.reshape(B * M, D)], axis=0)
    cv = jnp.zeros((16, D), F32).at[:B].set(c).at[B].set(c_ctx)
    mod = _mod_call(cv, w_ada, b_ada)

    tm_in = _pick((N, B * M), 1024, 512, 256)
    tab = _rope_tables(N, tm_in)
    kinds = jnp.asarray(COL_KINDS, jnp.int32)
    edge = _edge_table(B, N, M)
    pad_lanes = LANES - 2 * SSD_HEADS

    for l in range(L):
        need_ctx = l < L - 1
        lam_init = 0.8 - 0.6 * math.exp(-0.3 * l)
        R = T if need_ctx else B * N
        mod3 = mod[l].reshape(16, 1, 6 * D)
        kw = dict(B=B, N=N, M=M)

        P = _inproj_call(X, mod3, norm_g[l, 0:1], _layout_w_in(w_in[l]), tab, kinds, sc_part=1, sh_part=0, **kw)

        dargs = (P, diff_lambda[l], diff_norm[l].reshape(1, -1))
        YD = _diff_call(*dargs, with_lat=True, lam_init=lam_init, **kw)
        if need_ctx:
            YD = jnp.concatenate([YD, _diff_call(*dargs, with_lat=False, lam_init=lam_init, **kw)], axis=0)
        YW = _win_call(P, win_sink[l], need_ctx=need_ctx, **kw)

        U = _ssdconv_call(P, ssd_conv_w[l], ssd_conv_b[l].reshape(1, -1), **kw)
        bias = jnp.pad(ssd_dt_bias[l].reshape(1, -1), ((0, 0), (0, pad_lanes)))
        alog = jnp.pad(ssd_a_log[l].reshape(1, -1), ((0, 0), (0, pad_lanes)))
        YF, YB = _ssd_call(U, P, bias, alog, **kw)

        ww = w_br_win[l].reshape(WIN_KV_HEADS, WIN_G, WIN_DH, D).transpose(1, 0, 2, 3).reshape(WIN_WIDTH, D)
        dsk = jnp.repeat(ssd_d[l], SSD_HEAD_DIM).reshape(1, -1)
        Z = _merge_call(YD, YF, YB, U, P, YW, w_br_diff[l].astype(BF16), w_br_ssd[l].astype(BF16),
                        ww.astype(BF16), dsk, ssd_norm[l].reshape(1, -1), R=R, D=D)
        X1 = _mnr_call(Z, w_out[l].astype(BF16), X, mod3, norm_g[l, 1:2], R=R, gate_part=2, tk=1024,
                       name="out_proj", **kw)

        A = _ffnup_call(X1, mod3, norm_g[l, 2:3], ffn_w_up[l].astype(BF16), ffn_conv_w[l],
                        ffn_conv_b[l].reshape(1, -1), edge, R=R, sc_part=4, sh_part=3, **kw)
        X = _mnr_call(A, ffn_w_down[l].astype(BF16), X1, mod3, norm_g[l, 3:4], R=R, gate_part=5,
                      tk=_pick((A.shape[1],), 1408, 512, 128), name="ffn_down", **kw)

    return X[:B * N].reshape(B, N, D)
```

```python
import functools
import math

import jax
import jax.numpy as jnp
from jax import lax
from jax.experimental import pallas as pl
from jax.experimental.pallas import tpu as pltpu

F32 = jnp.float32
BF16 = jnp.bfloat16

GRID_W = 64
ROPE_BASE = 10000.0
NORM_EPS = 1e-6
N_DIFF_HEADS = 8
DIFF_DH = 64
DIFF_WIDTH = N_DIFF_HEADS * 2 * DIFF_DH
SSD_INNER = 1024
SSD_HEAD_DIM = 64
SSD_HEADS = 16
SSD_GROUPS = 4
SSD_STATE = 128
SSD_CONV_W = 5
SSD_CHUNK = 128
SSD_XBC = SSD_INNER + 2 * SSD_GROUPS * SSD_STATE
WIN_HEADS = 16
WIN_KV_HEADS = 4
WIN_G = WIN_HEADS // WIN_KV_HEADS
WIN_DH = 64
WINDOW = 128
WIN_BLOCK = 128
WIN_WIDTH = WIN_HEADS * WIN_DH
WIN_KV_WIDTH = WIN_KV_HEADS * WIN_DH
FFN_CONV_W = 3

LOG2E = 1.4426950408889634
NEG_BIG = -1e30

LANES = 128
HALO = 16
QCOLS = 256
DIFF_LOOKAHEAD = 3
WIN_LOOKAHEAD = 2
CONV_ROWS = 256
VMEM_LIMIT = 56 * 1024 * 1024

PW = 1024
COL_DQ = 0 * PW
COL_DK = 1 * PW
COL_WQ = 2 * PW
COL_WKV = 3 * PW
COL_DT = COL_WKV + 2 * WIN_KV_WIDTH
COL_DV = 4 * PW
COL_SZ = 5 * PW
COL_XBC = 6 * PW
COL_GL = 8 * PW
P_WIDTH = 14 * PW
KIND_NONE, KIND_Q, KIND_K, KIND_K256 = 0, 1, 2, 3
COL_KINDS = (KIND_Q, KIND_K, KIND_Q, KIND_K256) + (KIND_NONE,) * 10


def _cparams(*sem):
    return pltpu.CompilerParams(dimension_semantics=sem, vmem_limit_bytes=VMEM_LIMIT)


def _sigmoid(x):
    return 1.0 / (1.0 + jnp.exp(-x))


def _silu(x):
    return x * _sigmoid(x)


def _pick(n, *cands):
    for c in cands:
        if all(v % c == 0 for v in n):
            return c
    raise ValueError(f"no block size among {cands} divides {n}")


def _mod_kernel(cv_ref, w_ref, b_ref, o_ref):
    s = _silu(cv_ref[...]).astype(BF16)
    o_ref[0] = jnp.dot(s, w_ref[0].astype(BF16), preferred_element_type=F32) + b_ref[0]


def _mod_call(cv, w_ada, b_ada):
    L, D, W6 = w_ada.shape
    tn = 1024
    return pl.pallas_call(
        _mod_kernel,
        grid=(L, W6 // tn),
        in_specs=[
            pl.BlockSpec((16, D), lambda l, j: (0, 0)),
            pl.BlockSpec((1, D, tn), lambda l, j: (l, 0, j)),
            pl.BlockSpec((1, 1, tn), lambda l, j: (l, 0, j)),
        ],
        out_specs=pl.BlockSpec((1, 16, tn), lambda l, j: (l, 0, j)),
        out_shape=jax.ShapeDtypeStruct((L, 16, W6), F32),
        compiler_params=_cparams("parallel", "parallel"),
        name="adaln_mod",
    )(cv, w_ada, b_ada.reshape(L, 1, W6))


def _norm_mod(x, g, sc, sh):
    ms = jnp.mean(x * x, axis=-1, keepdims=True)
    y = x * lax.rsqrt(ms + NORM_EPS) * g
    return y * (1.0 + sc) + sh


def _rope(y, cos, sa, sb):
    w = y.shape[1]
    rep = w // LANES
    if rep > 1:
        cos, sa, sb = (jnp.tile(t, (1, rep)) for t in (cos, sa, sb))
    up = pltpu.roll(y, w - DIFF_DH // 2, axis=1)
    dn = pltpu.roll(y, DIFF_DH // 2, axis=1)
    return y * cos + up * sa + dn * sb


def _inproj_kernel(kind_ref, x_ref, sc_ref, sh_ref, g_ref, w_ref, tab_ref, o_ref, h_scr):
    j = pl.program_id(1)

    @pl.when(j == 0)
    def _():
        h_scr[...] = _norm_mod(x_ref[...], g_ref[...], sc_ref[0], sh_ref[0]).astype(BF16)

    y = jnp.dot(h_scr[...], w_ref[...], preferred_element_type=F32)
    kind = kind_ref[j]

    @pl.when(kind == KIND_NONE)
    def _():
        o_ref[...] = y.astype(BF16)

    @pl.when(jnp.logical_or(kind == KIND_Q, kind == KIND_K))
    def _():
        o_ref[...] = _rope(y, tab_ref[0, 0], tab_ref[0, 1], tab_ref[0, 2]).astype(BF16)

    @pl.when(kind == KIND_K256)
    def _():
        kw = WIN_KV_WIDTH
        o_ref[:, :kw] = _rope(y[:, :kw], tab_ref[0, 0], tab_ref[0, 1], tab_ref[0, 2]).astype(BF16)
        o_ref[:, kw:] = y[:, kw:].astype(BF16)


def _inproj_call(X, mod3, g, W, tab, kinds, *, B, N, M, sc_part, sh_part):
    T, D = X.shape
    tm = _pick((N, B * M), 1024, 512, 256)
    nlat = N // tm
    grp = lambda i: jnp.minimum((i * tm) // N, B)
    rblk = lambda i: jnp.where(i < B * nlat, i % nlat, nlat)
    ksel = lambda k: jnp.where(jnp.logical_or(k == KIND_K, k == KIND_K256), 1, 0)
    grid_spec = pltpu.PrefetchScalarGridSpec(
        num_scalar_prefetch=1,
        grid=(T // tm, P_WIDTH // PW),
        in_specs=[
            pl.BlockSpec((tm, D), lambda i, j, kr: (i, 0)),
            pl.BlockSpec((1, 1, D), lambda i, j, kr: (grp(i), 0, sc_part)),
            pl.BlockSpec((1, 1, D), lambda i, j, kr: (grp(i), 0, sh_part)),
            pl.BlockSpec((1, D), lambda i, j, kr: (0, 0)),
            pl.BlockSpec((D, PW), lambda i, j, kr: (0, j)),
            pl.BlockSpec((1, 3, tm, LANES), lambda i, j, kr: (ksel(kr[j]), 0, rblk(i), 0)),
        ],
        out_specs=pl.BlockSpec((tm, PW), lambda i, j, kr: (i, j)),
        scratch_shapes=[pltpu.VMEM((tm, D), BF16)],
    )
    return pl.pallas_call(
        _inproj_kernel,
        grid_spec=grid_spec,
        out_shape=jax.ShapeDtypeStruct((T, P_WIDTH), BF16),
        compiler_params=_cparams("parallel", "arbitrary"),
        name="in_proj",
    )(kinds, X, mod3, mod3, g, W, tab)


def _ffnup_kernel(x_ref, xp_ref, xn_ref, sc_ref, sh_ref, g_ref, wa_ref, wg_ref, cwa_ref, cwg_ref, ba_ref, bg_ref,
                  edge_ref, o_ref, h_scr, bufa, bufg):
    tm = x_ref.shape[0]

    @pl.when(pl.program_id(1) == 0)
    def _():
        nm = lambda r: _norm_mod(r[...], g_ref[...], sc_ref[0], sh_ref[0]).astype(BF16)
        h_scr[0:HALO] = nm(xp_ref)
        h_scr[HALO:HALO + tm] = nm(x_ref)
        h_scr[HALO + tm:] = nm(xn_ref)

    rep = o_ref.shape[1] // LANES
    prev_ok = jnp.tile(edge_ref[0], (1, rep))
    next_ok = jnp.tile(edge_ref[1], (1, rep))

    def conv(buf, w_ref, cw_ref, b_ref):
        buf[...] = jnp.dot(h_scr[...], w_ref[...], preferred_element_type=F32)
        taps = [buf[pl.ds(HALO - 1 + j, tm), :] for j in range(FFN_CONV_W)]
        out = b_ref[...] + (taps[0] * prev_ok) * cw_ref[0:1, :]
        out = out + taps[1] * cw_ref[1:2, :]
        return out + (taps[2] * next_ok) * cw_ref[2:3, :]

    a = conv(bufa, wa_ref, cwa_ref, ba_ref)
    g = conv(bufg, wg_ref, cwg_ref, bg_ref)
    o_ref[...] = (_silu(a) * g).astype(BF16)


def _ffnup_call(X, mod3, g, W, cw, cb, edge, *, R, B, N, M, sc_part, sh_part):
    D = X.shape[1]
    F = W.shape[1] // 2
    tm = _pick((N, B * M), 1024, 512, 256)
    tn = _pick((F,), 512, 128)
    nc = F // tn
    hb = tm // HALO
    nh = R // HALO
    grp = lambda i: jnp.minimum((i * tm) // N, B)
    col = lambda shape, off: pl.BlockSpec(shape, lambda i, j: (0, j + off))
    return pl.pallas_call(
        _ffnup_kernel,
        grid=(R // tm, nc),
        in_specs=[
            pl.BlockSpec((tm, D), lambda i, j: (i, 0)),
            pl.BlockSpec((HALO, D), lambda i, j: (jnp.maximum(i * hb - 1, 0), 0)),
            pl.BlockSpec((HALO, D), lambda i, j: (jnp.minimum((i + 1) * hb, nh - 1), 0)),
            pl.BlockSpec((1, 1, D), lambda i, j: (grp(i), 0, sc_part)),
            pl.BlockSpec((1, 1, D), lambda i, j: (grp(i), 0, sh_part)),
            pl.BlockSpec((1, D), lambda i, j: (0, 0)),
            col((D, tn), 0), col((D, tn), nc),
            col((FFN_CONV_W, tn), 0), col((FFN_CONV_W, tn), nc),
            col((1, tn), 0), col((1, tn), nc),
            pl.BlockSpec((2, tm, LANES), lambda i, j: (0, i, 0)),
        ],
        out_specs=pl.BlockSpec((tm, tn), lambda i, j: (i, j)),
        out_shape=jax.ShapeDtypeStruct((R, F), BF16),
        scratch_shapes=[pltpu.VMEM((tm + 2 * HALO, D), BF16), pltpu.VMEM((tm + 2 * HALO, tn), F32),
                        pltpu.VMEM((tm + 2 * HALO, tn), F32)],
        compiler_params=_cparams("parallel", "arbitrary"),
        name="ffn_up_conv_gate",
    )(X, X, X, mod3, mod3, g, W, W, cw, cw, cb, cb, edge)


def _mnr_kernel(a_ref, w_ref, x_ref, gate_ref, g_ref, o_ref, acc):
    k = pl.program_id(1)

    @pl.when(k == 0)
    def _():
        acc[...] = jnp.zeros_like(acc)

    acc[...] += jnp.dot(a_ref[...], w_ref[...], preferred_element_type=F32)

    @pl.when(k == pl.num_programs(1) - 1)
    def _():
        y = acc[...]
        ms = jnp.mean(y * y, axis=-1, keepdims=True)
        r = y * lax.rsqrt(ms + NORM_EPS) * g_ref[...]
        o_ref[...] = x_ref[...] + gate_ref[0] * r


def _mnr_call(A, W, X, mod3, g, *, R, B, N, M, gate_part, tk, name):
    K, D = W.shape
    tm = _pick((N, B * M), 512, 256)
    grp = lambda i: jnp.minimum((i * tm) // N, B)
    return pl.pallas_call(
        _mnr_kernel,
        grid=(R // tm, K // tk),
        in_specs=[
            pl.BlockSpec((tm, tk), lambda i, k: (i, k)),
            pl.BlockSpec((tk, D), lambda i, k: (k, 0)),
            pl.BlockSpec((tm, D), lambda i, k: (i, 0)),
            pl.BlockSpec((1, 1, D), lambda i, k: (grp(i), 0, gate_part)),
            pl.BlockSpec((1, D), lambda i, k: (0, 0)),
        ],
        out_specs=pl.BlockSpec((tm, D), lambda i, k: (i, 0)),
        out_shape=jax.ShapeDtypeStruct((R, D), F32),
        scratch_shapes=[pltpu.VMEM((tm, D), F32)],
        compiler_params=_cparams("parallel", "arbitrary"),
        name=name,
    )(A, W, X, mod3, g)


def _diff_kernel(lam_ref, nw_ref, q_ref, *refs, tq, tk, with_lat, lam_init):
    if with_lat:
        kl_ref, kc_ref, vl_ref, vc_ref, o_ref, vt_scr, m_scr, acc_scr = refs
        N = kl_ref.shape[0]
    else:
        kc_ref, vc_ref, o_ref, vt_scr, m_scr, acc_scr = refs
        N = 0
    M = kc_ref.shape[0]
    hd = 2 * DIFF_DH
    n_lat_k = N // tk

    @pl.when(pl.program_id(2) == 0)
    def _():
        for c in range(n_lat_k):
            vt_scr[0:hd, c * tk:(c + 1) * tk] = vl_ref[c * tk:(c + 1) * tk, :].astype(F32).T.astype(BF16)
        vt_scr[0:hd, N:N + M] = vc_ref[...].astype(F32).T.astype(BF16)
        vt_scr[hd:, :] = jnp.ones((vt_scr.shape[0] - hd, N + M), BF16)

    q = q_ref[...]
    lane = lax.broadcasted_iota(jnp.int32, q.shape, 1)
    zero = jnp.zeros_like(q)
    qs = jnp.concatenate([jnp.where(lane < DIFF_DH, q, zero), jnp.where(lane >= DIFF_DH, q, zero)], axis=0)

    m_scr[...] = jnp.full_like(m_scr, NEG_BIG)
    acc_scr[...] = jnp.zeros_like(acc_scr)

    qcols = min(QCOLS, 2 * tq)
    chunks = [(c * tk, tk) for c in range(n_lat_k)] + [(N, M)]
    items = [(k0, kn, t * qcols) for (k0, kn) in chunks for t in range(2 * tq // qcols)]

    def keys(k0, kn):
        return kl_ref[k0:k0 + kn, :] if k0 < N else kc_ref[...]

    def scores(k0, kn, c0):
        return lax.dot_general(keys(k0, kn), qs[c0:c0 + qcols], (((1,), (1,)), ((), ())),
                               preferred_element_type=F32)

    def accumulate(s, k0, kn, c0):
        cols = slice(c0, c0 + qcols)
        m_prev = m_scr[:, cols]
        m_new = jnp.maximum(m_prev, jnp.max(s, axis=0, keepdims=True))
        alpha = jnp.exp2(m_prev - m_new)
        p = jnp.exp2(s - m_new).astype(BF16)
        acc_scr[:, cols] = alpha * acc_scr[:, cols] + jnp.dot(vt_scr[:, k0:k0 + kn], p, preferred_element_type=F32)
        m_scr[:, cols] = m_new

    ahead = min(DIFF_LOOKAHEAD, len(items))
    pending = [scores(*it) for it in items[:ahead]]
    for n, item in enumerate(items):
        if n + ahead < len(items):
            pending.append(scores(*items[n + ahead]))
        accumulate(pending.pop(0), *item)

    lp = lam_ref[...]
    lam = (jnp.exp(jnp.sum(lp[0:1] * lp[1:2], axis=1, keepdims=True))
           - jnp.exp(jnp.sum(lp[2:3] * lp[3:4], axis=1, keepdims=True)) + lam_init)
    acc = acc_scr[...]
    o = acc[:hd, :tq] / acc[hd:hd + 1, :tq] - lam * (acc[:hd, tq:] / acc[hd:hd + 1, tq:])
    ms = jnp.mean(o * o, axis=0, keepdims=True)
    r = o * (lax.rsqrt(ms + NORM_EPS) * (1.0 - lam_init))
    o_ref[...] = (r.T * nw_ref[...]).astype(BF16)


def _diff_call(P, lam_p, nw, *, B, N, M, with_lat, lam_init):
    tk = _pick((N,), 512, 256)
    hw = 2 * DIFF_DH
    cq, ck, cv = COL_DQ // hw, COL_DK // hw, COL_DV // hw
    ctx_blk = B * N // M
    small = lambda: pl.BlockSpec((4, DIFF_DH), lambda b, h, qi: (0, 0))
    lat = lambda col: pl.BlockSpec((N, hw), lambda b, h, qi: (b, col + h))
    ctx = lambda col: pl.BlockSpec((M, hw), lambda b, h, qi: (ctx_blk + b, col + h))
    if with_lat:
        tq = _pick((N,), 512, 256)
        nq, q0, keys = N // tq, 0, N + M
        kv_specs = [lat(ck), ctx(ck), lat(cv), ctx(cv)]
    else:
        tq = M
        nq, q0, keys = 1, B * N // tq, M
        kv_specs = [ctx(ck), ctx(cv)]
    kern = functools.partial(_diff_kernel, tq=tq, tk=tk, with_lat=with_lat, lam_init=lam_init)
    return pl.pallas_call(
        kern,
        grid=(B, N_DIFF_HEADS, nq),
        in_specs=[
            small(),
            pl.BlockSpec((1, hw), lambda b, h, qi: (0, 0)),
            pl.BlockSpec((tq, hw), lambda b, h, qi: (q0 + b * nq + qi, cq + h)),
        ] + kv_specs,
        out_specs=pl.BlockSpec((tq, hw), lambda b, h, qi: (b * nq + qi, h)),
        out_shape=jax.ShapeDtypeStruct((B * nq * tq, DIFF_WIDTH), BF16),
        scratch_shapes=[pltpu.VMEM((hw + HALO, keys), BF16), pltpu.VMEM((1, 2 * tq), F32),
                        pltpu.VMEM((hw + HALO, 2 * tq), F32)],
        compiler_params=_cparams("parallel", "parallel", "arbitrary"),
        name="diff_attn" if with_lat else "diff_attn_ctx",
    )(lam_p, nw, P, *([P] * len(kv_specs)))


def _win_kernel(sink_ref, q_ref, kp_ref, kc_ref, kn_ref, kx_ref, o_ref, *, nb, N, M):
    i = pl.program_id(1)
    wb, kw = WIN_BLOCK, WIN_KV_WIDTH
    q = q_ref[...]
    kcat = jnp.concatenate([kp_ref[:, :kw], kc_ref[:, :kw], kn_ref[:, :kw], kx_ref[:, :kw]], axis=0)
    vcat = jnp.concatenate([kp_ref[:, kw:], kc_ref[:, kw:], kn_ref[:, kw:], kx_ref[:, kw:]], axis=0)
    nk = 3 * wb + M
    rows = WIN_G * wb
    row = lax.broadcasted_iota(jnp.int32, (rows, nk), 0) % wb
    col = lax.broadcasted_iota(jnp.int32, (rows, nk), 1)
    kpos = (i - 1) * wb + col
    n_band = jnp.where(i < nb, N, 0)
    band_ok = (jnp.abs(col - wb - row) <= WINDOW) & (kpos >= 0) & (kpos < n_band)
    valid = band_ok | (col >= 3 * wb)
    lane = lax.broadcasted_iota(jnp.int32, (wb, kw), 1) // WIN_DH
    lane_o = lax.broadcasted_iota(jnp.int32, (rows, kw), 1) // WIN_DH
    grow = lax.broadcasted_iota(jnp.int32, (rows, 1), 0) // wb
    zero = jnp.zeros((wb, kw), BF16)
    acc = jnp.zeros((rows, kw), F32)

    def scores(kv):
        qs = jnp.concatenate([jnp.where(lane == kv, q[:, g * kw:(g + 1) * kw], zero) for g in range(WIN_G)], axis=0)
        return lax.dot_general(qs, kcat, (((1,), (1,)), ((), ())), preferred_element_type=F32)

    pending = [scores(kv) for kv in range(WIN_LOOKAHEAD)]
    for kv in range(WIN_KV_HEADS):
        if kv + WIN_LOOKAHEAD < WIN_KV_HEADS:
            pending.append(scores(kv + WIN_LOOKAHEAD))
        s = jnp.where(valid, pending.pop(0), NEG_BIG)
        sink = jnp.zeros((rows, 1), F32)
        for g in range(WIN_G):
            sink = jnp.where(grow == g, sink_ref[kv * WIN_G + g] * LOG2E, sink)
        m = jnp.maximum(jnp.max(s, axis=1, keepdims=True), sink)
        p = jnp.exp2(s - m)
        l = jnp.sum(p, axis=1, keepdims=True) + jnp.exp2(sink - m)
        o = jnp.dot(p.astype(BF16), vcat, preferred_element_type=F32) / l
        acc = jnp.where(lane_o == kv, o, acc)
    for g in range(WIN_G):
        o_ref[:, g * kw:(g + 1) * kw] = acc[g * wb:(g + 1) * wb].astype(BF16)


def _win_call(P, sink, *, B, N, M, need_ctx):
    wb = WIN_BLOCK
    nb, nbc = N // wb, (M // wb if need_ctx else 0)
    R = B * N + (B * M if need_ctx else 0)
    cq, ckv = COL_WQ // PW, COL_WKV // (2 * WIN_KV_WIDTH)

    def qrow(b, i):
        return jnp.where(i < nb, b * nb + i, B * nb + b * nbc + (i - nb))

    def band(b, i, d):
        return b * nb + jnp.clip(i + d, 0, nb - 1)

    kvw = 2 * WIN_KV_WIDTH
    kern = functools.partial(_win_kernel, nb=nb, N=N, M=M)
    grid_spec = pltpu.PrefetchScalarGridSpec(
        num_scalar_prefetch=1,
        grid=(B, nb + nbc),
        in_specs=[
            pl.BlockSpec((wb, PW), lambda b, i, s: (qrow(b, i), cq)),
            pl.BlockSpec((wb, kvw), lambda b, i, s: (band(b, i, -1), ckv)),
            pl.BlockSpec((wb, kvw), lambda b, i, s: (band(b, i, 0), ckv)),
            pl.BlockSpec((wb, kvw), lambda b, i, s: (band(b, i, 1), ckv)),
            pl.BlockSpec((M, kvw), lambda b, i, s: (B * N // M + b, ckv)),
        ],
        out_specs=pl.BlockSpec((wb, PW), lambda b, i, s: (qrow(b, i), 0)),
    )
    return pl.pallas_call(
        kern,
        grid_spec=grid_spec,
        out_shape=jax.ShapeDtypeStruct((R, WIN_WIDTH), BF16),
        compiler_params=_cparams("parallel", "arbitrary"),
        name="win_attn",
    )(sink, P, P, P, P, P)


def _seq_edges(i, R_lat, N, M):
    r0 = i * CONV_ROWS
    lat = r0 < R_lat
    first = jnp.where(lat, r0 % N == 0, (r0 - R_lat) % M == 0)
    last = jnp.where(lat, (r0 + CONV_ROWS) % N == 0, (r0 + CONV_ROWS - R_lat) % M == 0)
    return first, last


def _conv_taps(buf, main_ref, prev_ref, next_ref, w_ref, b_ref, first, last, k):
    buf[0:HALO] = jnp.where(first, 0.0, prev_ref[...].astype(F32))
    buf[HALO:HALO + CONV_ROWS] = main_ref[...].astype(F32)
    buf[HALO + CONV_ROWS:] = jnp.where(last, 0.0, next_ref[...].astype(F32))
    out = b_ref[...]
    for j in range(k):
        out = out + buf[pl.ds(HALO - k // 2 + j, CONV_ROWS), :] * w_ref[j:j + 1, :]
    return out


def _ssdconv_kernel(m_ref, p_ref, n_ref, w_ref, b_ref, o_ref, buf, *, R_lat, N, M):
    first, last = _seq_edges(pl.program_id(0), R_lat, N, M)
    o_ref[...] = _silu(_conv_taps(buf, m_ref, p_ref, n_ref, w_ref, b_ref, first, last, SSD_CONV_W)).astype(BF16)


def _ssdconv_call(P, w, b, *, B, N, M):
    T = P.shape[0]
    cr, W = CONV_ROWS, SSD_XBC
    hb = cr // HALO
    cb = COL_XBC // W
    nh = T // HALO
    kern = functools.partial(_ssdconv_kernel, R_lat=B * N, N=N, M=M)
    return pl.pallas_call(
        kern,
        grid=(T // cr,),
        in_specs=[
            pl.BlockSpec((cr, W), lambda i: (i, cb)),
            pl.BlockSpec((HALO, W), lambda i: (jnp.maximum(i * hb - 1, 0), cb)),
            pl.BlockSpec((HALO, W), lambda i: (jnp.minimum((i + 1) * hb, nh - 1), cb)),
            pl.BlockSpec((SSD_CONV_W, W), lambda i: (0, 0)),
            pl.BlockSpec((1, W), lambda i: (0, 0)),
        ],
        out_specs=pl.BlockSpec((cr, W), lambda i: (i, 0)),
        out_shape=jax.ShapeDtypeStruct((T, W), BF16),
        scratch_shapes=[pltpu.VMEM((cr + 2 * HALO, W), F32)],
        compiler_params=_cparams("parallel"),
        name="ssd_conv",
    )(P, P, P, w, b)


def _cumsum_rows(a):
    n = a.shape[0]
    tri = (lax.broadcasted_iota(jnp.int32, (n, n), 0) >= lax.broadcasted_iota(jnp.int32, (n, n), 1)).astype(BF16)
    hi = a.astype(BF16)
    r1 = a - hi.astype(F32)
    mid = r1.astype(BF16)
    lo = (r1 - mid.astype(F32)).astype(BF16)
    dot = lambda v: jnp.dot(tri, v, preferred_element_type=F32)
    return dot(hi) + dot(mid) + dot(lo)


def _ssd_prep(dt_ref, bias_ref, alog_ref, e_ref, *, reverse):
    ch = SSD_CHUNK
    x = dt_ref[...].astype(F32) + bias_ref[...]
    dt = jnp.maximum(x, 0.0) + jnp.log(1.0 + jnp.exp(-jnp.abs(x)))
    ad = dt * (-jnp.exp(alog_ref[...]))
    acum = _cumsum_rows(ad)
    total = acum[ch - 1:ch, :]
    if reverse:
        cvec = acum - ad
        e_off = jnp.exp(total - cvec)
        e_ws = jnp.exp(cvec)
    else:
        cvec = acum
        e_off = jnp.exp(acum)
        e_ws = jnp.exp(total - acum)
    tot_rows = jnp.broadcast_to(jnp.exp(total), (HALO, LANES))
    scales = jnp.dot(jnp.concatenate([dt * e_ws, e_off, tot_rows], axis=0).astype(BF16), e_ref[...],
                     preferred_element_type=F32)
    return cvec, cvec.T, dt.T, scales


def _ssd_group(g, prep, u_ref, s_ref, y_ref, *, reverse, lane0):
    cvec, cT, dtT, scales = prep
    ch, hp = SSD_CHUNK, SSD_HEAD_DIM
    hpg = SSD_HEADS // SSD_GROUPS
    gw = hpg * hp
    li = lax.broadcasted_iota(jnp.int32, (ch, ch), 0)
    si = lax.broadcasted_iota(jnp.int32, (ch, ch), 1)
    tri = (si >= li) if reverse else (li >= si)
    hsel = lax.broadcasted_iota(jnp.int32, (ch, gw), 1) // hp
    lanes = slice(g * gw, (g + 1) * gw)
    bg = u_ref[:, SSD_INNER + g * SSD_STATE:SSD_INNER + (g + 1) * SSD_STATE]
    cg = u_ref[:, SSD_INNER + (SSD_GROUPS + g) * SSD_STATE:SSD_INNER + (SSD_GROUPS + g + 1) * SSD_STATE]
    xs = u_ref[:, lanes]
    cb = lax.dot_general(cg, bg, (((1,), (1,)), ((), ())), preferred_element_type=F32)
    sg = s_ref[:, lanes]
    yoff = jnp.dot(cg, sg.astype(BF16), preferred_element_type=F32) * scales[ch:2 * ch, lanes]
    xw = (xs.astype(F32) * scales[0:ch, lanes]).astype(BF16)
    s_ref[:, lanes] = (sg * scales[2 * ch:2 * ch + 1, lanes]
                       + lax.dot_general(bg, xw, (((0,), (0,)), ((), ())), preferred_element_type=F32))
    ydiag = jnp.zeros((ch, gw), F32)
    for r in range(hpg):
        j = lane0 + g * hpg + r
        colv = jnp.broadcast_to(cvec[:, j:j + 1], (ch, ch))
        rowv = jnp.broadcast_to(cT[j:j + 1, :], (ch, ch))
        d = (rowv - colv) if reverse else (colv - rowv)
        lm = jnp.exp(jnp.where(tri, d, NEG_BIG)) * dtT[j:j + 1, :]
        yh = jnp.dot((cb * lm).astype(BF16), xs, preferred_element_type=F32)
        ydiag = jnp.where(hsel == r, yh, ydiag)
    y_ref[:, lanes] = (ydiag + yoff).astype(BF16)


def _ssd_kernel(uf_ref, dtf_ref, ub_ref, dtb_ref, bias_ref, alog_ref, e_ref, yf_ref, yb_ref, sf, sb):
    @pl.when(pl.program_id(1) == 0)
    def _():
        sf[...] = jnp.zeros_like(sf)
        sb[...] = jnp.zeros_like(sb)

    pf = _ssd_prep(dtf_ref, bias_ref, alog_ref, e_ref.at[0], reverse=False)
    pb = _ssd_prep(dtb_ref, bias_ref, alog_ref, e_ref.at[1], reverse=True)
    for g in range(SSD_GROUPS):
        _ssd_group(g, pf, uf_ref, sf, yf_ref, reverse=False, lane0=0)
        _ssd_group(g, pb, ub_ref, sb, yb_ref, reverse=True, lane0=SSD_HEADS)


def _ssd_call(U, P, bias, alog, spread, *, B, N, M):
    T = U.shape[0]
    ch = SSD_CHUNK
    ncl, ncc = N // ch, M // ch
    cdt = COL_DT // LANES

    def fwd(b, t):
        return jnp.where(t < ncc, B * ncl + b * ncc + t, b * ncl + (t - ncc))

    def bwd(b, t):
        return jnp.where(t < ncc, B * ncl + b * ncc + (ncc - 1 - t), b * ncl + (ncl - 1 - (t - ncc)))

    return pl.pallas_call(
        _ssd_kernel,
        grid=(B, ncl + ncc),
        in_specs=[
            pl.BlockSpec((ch, SSD_XBC), lambda b, t: (fwd(b, t), 0)),
            pl.BlockSpec((ch, LANES), lambda b, t: (fwd(b, t), cdt)),
            pl.BlockSpec((ch, SSD_XBC), lambda b, t: (bwd(b, t), 0)),
            pl.BlockSpec((ch, LANES), lambda b, t: (bwd(b, t), cdt)),
            pl.BlockSpec((1, LANES), lambda b, t: (0, 0)),
            pl.BlockSpec((1, LANES), lambda b, t: (0, 0)),
            pl.BlockSpec((2, LANES, SSD_INNER), lambda b, t: (0, 0, 0)),
        ],
        out_specs=[
            pl.BlockSpec((ch, SSD_INNER), lambda b, t: (fwd(b, t), 0)),
            pl.BlockSpec((ch, SSD_INNER), lambda b, t: (bwd(b, t), 0)),
        ],
        out_shape=[jax.ShapeDtypeStruct((T, SSD_INNER), BF16)] * 2,
        scratch_shapes=[pltpu.VMEM((SSD_STATE, SSD_INNER), F32)] * 2,
        compiler_params=_cparams("parallel", "arbitrary"),
        name="ssd_scan",
    )(U, P, U, P, bias, alog, spread)


def _merge_kernel(yd_ref, yf_ref, yb_ref, xs_ref, z_ref, yw_ref, gd_ref, gs_ref, gw_ref,
                  wd_ref, ws_ref, ww_ref, dsk_ref, sn_ref, o_ref):
    y = yf_ref[...].astype(F32) + yb_ref[...].astype(F32) + dsk_ref[...] * xs_ref[...].astype(F32)
    y = y * _silu(z_ref[...].astype(F32))
    ms = jnp.mean(y * y, axis=-1, keepdims=True)
    s = (y * lax.rsqrt(ms + NORM_EPS) * sn_ref[...]).astype(BF16)
    dot = lambda a, w: jnp.dot(a, w[...], preferred_element_type=F32)
    sig = lambda r: _sigmoid(r[...].astype(F32))
    acc = sig(gd_ref) * dot(yd_ref[...], wd_ref)
    acc = acc + sig(gs_ref) * dot(s, ws_ref)
    acc = acc + sig(gw_ref) * dot(yw_ref[...], ww_ref)
    o_ref[...] = acc.astype(BF16)


def _merge_call(YD, YF, YB, U, P, YW, wd, ws, ww, dsk, sn, *, R, D):
    tm = 256
    W = SSD_INNER
    cg = COL_GL // D
    row = lambda c: pl.BlockSpec((tm, W), lambda i: (i, c))
    gate = lambda c: pl.BlockSpec((tm, D), lambda i: (i, cg + c))
    wspec = pl.BlockSpec((W, D), lambda i: (0, 0))
    vec = pl.BlockSpec((1, W), lambda i: (0, 0))
    return pl.pallas_call(
        _merge_kernel,
        grid=(R // tm,),
        in_specs=[row(0), row(0), row(0), row(0), row(COL_SZ // W), row(0),
                  gate(0), gate(1), gate(2), wspec, wspec, wspec, vec, vec],
        out_specs=pl.BlockSpec((tm, D), lambda i: (i, 0)),
        out_shape=jax.ShapeDtypeStruct((R, D), BF16),
        compiler_params=_cparams("parallel"),
        name="branch_merge",
    )(YD, YF, YB, U, P, YW, P, P, P, wd, ws, ww, dsk, sn)


def _rope_tables(N, rows_ctx):
    hd = DIFF_DH
    pos = jnp.arange(N)
    row = (pos // GRID_W).astype(F32)
    colp = (pos % GRID_W).astype(F32)
    axis_dim = hd // 2
    inv = ROPE_BASE ** (-jnp.arange(0, axis_dim, 2, dtype=F32) / axis_dim)
    ang = jnp.concatenate([row[:, None] * inv, colp[:, None] * inv], axis=-1)
    lane = jnp.arange(LANES)
    cos = jnp.cos(ang)[:, lane % (hd // 2)]
    sin = jnp.sin(ang)[:, lane % (hd // 2)]
    first = (lane % hd) < hd // 2
    lat = jnp.stack([cos, jnp.where(first, -sin, 0.0), jnp.where(first, 0.0, sin)])
    ctx = jnp.stack([jnp.ones((rows_ctx, LANES), F32), jnp.zeros((rows_ctx, LANES), F32),
                     jnp.zeros((rows_ctx, LANES), F32)])
    k_tab = jnp.concatenate([lat, ctx], axis=1)
    q_tab = k_tab * (DIFF_DH ** -0.5 * LOG2E)
    return jnp.stack([q_tab, k_tab])


def _edge_table(B, N, M):
    pos = jnp.concatenate([jnp.tile(jnp.arange(N), B), jnp.tile(jnp.arange(M), B)])
    last = jnp.concatenate([jnp.full((B * N,), N - 1), jnp.full((B * M,), M - 1)])
    t = jnp.stack([pos != 0, pos != last]).astype(F32)
    return jnp.broadcast_to(t[:, :, None], t.shape + (LANES,))


def _layout_w_in(w):
    D = w.shape[0]
    sizes = (DIFF_WIDTH, DIFF_WIDTH, DIFF_WIDTH, SSD_INNER, SSD_XBC, 2 * SSD_HEADS,
             WIN_WIDTH, WIN_KV_WIDTH, WIN_KV_WIDTH, 3 * D)
    parts, s = [], 0
    for z in sizes:
        parts.append(w[:, s:s + z])
        s += z
    dq, dk, dv, sz, sxbc, sdt, wq, wk, wv, gl = parts
    wq = wq.reshape(D, WIN_KV_HEADS, WIN_G, WIN_DH).transpose(0, 2, 1, 3).reshape(D, WIN_WIDTH)
    pad = jnp.zeros((D, PW - 2 * WIN_KV_WIDTH - 2 * SSD_HEADS), w.dtype)
    return jnp.concatenate([dq, dk, wq, wk, wv, sdt, pad, dv, sz, sxbc, gl], axis=1).astype(BF16)


def kernel(x, c, ctx, c_ctx, w_ada, b_ada, norm_g, w_in, diff_lambda, diff_norm, ssd_conv_w, ssd_conv_b,
           ssd_a_log, ssd_dt_bias, ssd_d, ssd_norm, win_sink, w_br_diff, w_br_ssd, w_br_win, w_out,
           ffn_w_up, ffn_conv_w, ffn_conv_b, ffn_w_down):
    B, N, D = x.shape
    M = ctx.shape[1]
    L = w_ada.shape[0]
    assert D == 2 * PW and P_WIDTH == COL_GL + 3 * D
    assert N % CONV_ROWS == 0 and M % CONV_ROWS == 0 and (B * N) % M == 0 and B + 1 <= 16
    T = B * N + B * M

    X = jnp.concatenate([x.reshape(B * N, D), ctx.reshape(B * M, D)], axis=0)
    cv = jnp.zeros((16, D), F32).at[:B].set(c).at[B].set(c_ctx)
    mod = _mod_call(cv, w_ada, b_ada)

    tm_in = _pick((N, B * M), 1024, 512, 256)
    tab = _rope_tables(N, tm_in)
    kinds = jnp.asarray(COL_KINDS, jnp.int32)
    edge = _edge_table(B, N, M)
    head_of_lane = jnp.arange(SSD_INNER) // SSD_HEAD_DIM
    spread = jnp.stack([(jnp.arange(LANES)[:, None] == head_of_lane[None, :] + d * SSD_HEADS)
                        for d in range(2)]).astype(BF16)
    pad_lanes = LANES - 2 * SSD_HEADS

    for l in range(L):
        need_ctx = l < L - 1
        lam_init = 0.8 - 0.6 * math.exp(-0.3 * l)
        R = T if need_ctx else B * N
        mod3 = mod[l].reshape(16, 1, 6 * D)
        kw = dict(B=B, N=N, M=M)

        P = _inproj_call(X, mod3, norm_g[l, 0:1], _layout_w_in(w_in[l]), tab, kinds, sc_part=1, sh_part=0, **kw)

        dargs = (P, diff_lambda[l], diff_norm[l].reshape(1, -1))
        YD = _diff_call(*dargs, with_lat=True, lam_init=lam_init, **kw)
        if need_ctx:
            YD = jnp.concatenate([YD, _diff_call(*dargs, with_lat=False, lam_init=lam_init, **kw)], axis=0)
        YW = _win_call(P, win_sink[l], need_ctx=need_ctx, **kw)

        U = _ssdconv_call(P, ssd_conv_w[l], ssd_conv_b[l].reshape(1, -1), **kw)
        bias = jnp.pad(ssd_dt_bias[l].reshape(1, -1), ((0, 0), (0, pad_lanes)))
        alog = jnp.pad(ssd_a_log[l].reshape(1, -1), ((0, 0), (0, pad_lanes)))
        YF, YB = _ssd_call(U, P, bias, alog, spread, **kw)

        ww = w_br_win[l].reshape(WIN_KV_HEADS, WIN_G, WIN_DH, D).transpose(1, 0, 2, 3).reshape(WIN_WIDTH, D)
        dsk = jnp.repeat(ssd_d[l], SSD_HEAD_DIM).reshape(1, -1)
        Z = _merge_call(YD, YF, YB, U, P, YW, w_br_diff[l].astype(BF16), w_br_ssd[l].astype(BF16),
                        ww.astype(BF16), dsk, ssd_norm[l].reshape(1, -1), R=R, D=D)
        X1 = _mnr_call(Z, w_out[l].astype(BF16), X, mod3, norm_g[l, 1:2], R=R, gate_part=2, tk=1024,
                       name="out_proj", **kw)

        A = _ffnup_call(X1, mod3, norm_g[l, 2:3], ffn_w_up[l].astype(BF16), ffn_conv_w[l],
                        ffn_conv_b[l].reshape(1, -1), edge, R=R, sc_part=4, sh_part=3, **kw)
        X = _mnr_call(A, ffn_w_down[l].astype(BF16), X1, mod3, norm_g[l, 3:4], R=R, gate_part=5,
                      tk=_pick((A.shape[1],), 1408, 512, 128), name="ffn_down", **kw)

    return X[:B * N].reshape(B, N, D)
```

```python
import functools
import math

import jax
import jax.numpy as jnp
from jax import lax
from jax.experimental import pallas as pl
from jax.experimental.pallas import tpu as pltpu

F32 = jnp.float32
BF16 = jnp.bfloat16

GRID_W = 64
ROPE_BASE = 10000.0
NORM_EPS = 1e-6
N_DIFF_HEADS = 8
DIFF_DH = 64
DIFF_WIDTH = N_DIFF_HEADS * 2 * DIFF_DH
SSD_INNER = 1024
SSD_HEAD_DIM = 64
SSD_HEADS = 16
SSD_GROUPS = 4
SSD_STATE = 128
SSD_CONV_W = 5
SSD_CHUNK = 128
SSD_XBC = SSD_INNER + 2 * SSD_GROUPS * SSD_STATE
WIN_HEADS = 16
WIN_KV_HEADS = 4
WIN_G = WIN_HEADS // WIN_KV_HEADS
WIN_DH = 64
WINDOW = 128
WIN_BLOCK = 128
WIN_WIDTH = WIN_HEADS * WIN_DH
WIN_KV_WIDTH = WIN_KV_HEADS * WIN_DH
FFN_CONV_W = 3

LOG2E = 1.4426950408889634
NEG_BIG = -1e30

LANES = 128
HALO = 16
QCOLS = 256
DIFF_LOOKAHEAD = 3
WIN_LOOKAHEAD = 2
CONV_ROWS = 256
VMEM_LIMIT = 56 * 1024 * 1024

PW = 1024
COL_DQ = 0 * PW
COL_DK = 1 * PW
COL_WQ = 2 * PW
COL_WKV = 3 * PW
COL_DT = COL_WKV + 2 * WIN_KV_WIDTH
COL_DV = 4 * PW
COL_SZ = 5 * PW
COL_XBC = 6 * PW
COL_GL = 8 * PW
P_WIDTH = 14 * PW
KIND_NONE, KIND_Q, KIND_K, KIND_K256 = 0, 1, 2, 3
COL_KINDS = (KIND_Q, KIND_K, KIND_Q, KIND_K256) + (KIND_NONE,) * 10


def _cparams(*sem):
    return pltpu.CompilerParams(dimension_semantics=sem, vmem_limit_bytes=VMEM_LIMIT)


def _sigmoid(x):
    return 1.0 / (1.0 + jnp.exp(-x))


def _silu(x):
    return x * _sigmoid(x)


def _pick(n, *cands):
    for c in cands:
        if all(v % c == 0 for v in n):
            return c
    raise ValueError(f"no block size among {cands} divides {n}")


def _mod_kernel(cv_ref, w_ref, b_ref, o_ref):
    s = _silu(cv_ref[...]).astype(BF16)
    o_ref[0] = jnp.dot(s, w_ref[0].astype(BF16), preferred_element_type=F32) + b_ref[0]


def _mod_call(cv, w_ada, b_ada):
    L, D, W6 = w_ada.shape
    tn = 1024
    return pl.pallas_call(
        _mod_kernel,
        grid=(L, W6 // tn),
        in_specs=[
            pl.BlockSpec((16, D), lambda l, j: (0, 0)),
            pl.BlockSpec((1, D, tn), lambda l, j: (l, 0, j)),
            pl.BlockSpec((1, 1, tn), lambda l, j: (l, 0, j)),
        ],
        out_specs=pl.BlockSpec((1, 16, tn), lambda l, j: (l, 0, j)),
        out_shape=jax.ShapeDtypeStruct((L, 16, W6), F32),
        compiler_params=_cparams("parallel", "parallel"),
        name="adaln_mod",
    )(cv, w_ada, b_ada.reshape(L, 1, W6))


def _norm_mod(x, g, sc, sh):
    ms = jnp.mean(x * x, axis=-1, keepdims=True)
    y = x * lax.rsqrt(ms + NORM_EPS) * g
    return y * (1.0 + sc) + sh


def _rope(y, cos, sa, sb):
    w = y.shape[1]
    rep = w // LANES
    if rep > 1:
        cos, sa, sb = (jnp.tile(t, (1, rep)) for t in (cos, sa, sb))
    up = pltpu.roll(y, w - DIFF_DH // 2, axis=1)
    dn = pltpu.roll(y, DIFF_DH // 2, axis=1)
    return y * cos + up * sa + dn * sb


def _inproj_kernel(kind_ref, x_ref, sc_ref, sh_ref, g_ref, w_ref, tab_ref, o_ref, h_scr):
    j = pl.program_id(1)

    @pl.when(j == 0)
    def _():
        h_scr[...] = _norm_mod(x_ref[...], g_ref[...], sc_ref[0], sh_ref[0]).astype(BF16)

    dot = lambda: jnp.dot(h_scr[...], w_ref[...], preferred_element_type=F32)
    kind = kind_ref[j]

    @pl.when(kind == KIND_NONE)
    def _():
        o_ref[...] = dot().astype(BF16)

    @pl.when(jnp.logical_or(kind == KIND_Q, kind == KIND_K))
    def _():
        o_ref[...] = _rope(dot(), tab_ref[0, 0], tab_ref[0, 1], tab_ref[0, 2]).astype(BF16)

    @pl.when(kind == KIND_K256)
    def _():
        kw = WIN_KV_WIDTH
        y = dot()
        o_ref[:, :kw] = _rope(y[:, :kw], tab_ref[0, 0], tab_ref[0, 1], tab_ref[0, 2]).astype(BF16)
        o_ref[:, kw:] = y[:, kw:].astype(BF16)


def _inproj_call(X, mod3, g, W, tab, kinds, *, layer, B, N, M, sc_part, sh_part):
    T, D = X.shape
    tm = _pick((N, B * M), 1024, 512, 256)
    nlat = N // tm
    grp = lambda i: jnp.minimum((i * tm) // N, B)
    rblk = lambda i: jnp.where(i < B * nlat, i % nlat, nlat)
    ksel = lambda k: jnp.where(jnp.logical_or(k == KIND_K, k == KIND_K256), 1, 0)
    grid_spec = pltpu.PrefetchScalarGridSpec(
        num_scalar_prefetch=1,
        grid=(T // tm, P_WIDTH // PW),
        in_specs=[
            pl.BlockSpec((tm, D), lambda i, j, kr: (i, 0)),
            pl.BlockSpec((1, 1, D), lambda i, j, kr: (grp(i), 0, sc_part)),
            pl.BlockSpec((1, 1, D), lambda i, j, kr: (grp(i), 0, sh_part)),
            pl.BlockSpec((1, D), lambda i, j, kr: (0, 0)),
            pl.BlockSpec((None, D, PW), lambda i, j, kr: (layer, 0, j)),
            pl.BlockSpec((1, 3, tm, LANES), lambda i, j, kr: (ksel(kr[j]), 0, rblk(i), 0)),
        ],
        out_specs=pl.BlockSpec((tm, PW), lambda i, j, kr: (i, j)),
        scratch_shapes=[pltpu.VMEM((tm, D), BF16)],
    )
    return pl.pallas_call(
        _inproj_kernel,
        grid_spec=grid_spec,
        out_shape=jax.ShapeDtypeStruct((T, P_WIDTH), BF16),
        compiler_params=_cparams("parallel", "arbitrary"),
        name="in_proj",
    )(kinds, X, mod3, mod3, g, W, tab)


def _ffnup_kernel(x_ref, xp_ref, xn_ref, sc_ref, sh_ref, g_ref, wa_ref, wg_ref, cwa_ref, cwg_ref, ba_ref, bg_ref,
                  edge_ref, o_ref, h_scr, bufa, bufg):
    tm = x_ref.shape[0]

    @pl.when(pl.program_id(1) == 0)
    def _():
        nm = lambda r: _norm_mod(r[...], g_ref[...], sc_ref[0], sh_ref[0]).astype(BF16)
        h_scr[0:HALO] = nm(xp_ref)
        h_scr[HALO:HALO + tm] = nm(x_ref)
        h_scr[HALO + tm:] = nm(xn_ref)

    rep = o_ref.shape[1] // LANES
    prev_ok = jnp.tile(edge_ref[0], (1, rep))
    next_ok = jnp.tile(edge_ref[1], (1, rep))

    def conv(buf, w_ref, cw_ref, b_ref):
        buf[...] = jnp.dot(h_scr[...], w_ref[...], preferred_element_type=F32)
        taps = [buf[pl.ds(HALO - 1 + j, tm), :] for j in range(FFN_CONV_W)]
        out = b_ref[...] + (taps[0] * prev_ok) * cw_ref[0:1, :]
        out = out + taps[1] * cw_ref[1:2, :]
        return out + (taps[2] * next_ok) * cw_ref[2:3, :]

    a = conv(bufa, wa_ref, cwa_ref, ba_ref)
    g = conv(bufg, wg_ref, cwg_ref, bg_ref)
    o_ref[...] = (_silu(a) * g).astype(BF16)


def _ffnup_call(X, mod3, g, W, cw, cb, edge, *, layer, R, B, N, M, sc_part, sh_part):
    D = X.shape[1]
    F = W.shape[2] // 2
    tm = _pick((N, B * M), 1024, 512, 256)
    tn = _pick((F,), 512, 128)
    nc = F // tn
    hb = tm // HALO
    nh = R // HALO
    grp = lambda i: jnp.minimum((i * tm) // N, B)
    col = lambda shape, off: pl.BlockSpec(shape, lambda i, j: (0, j + off))
    return pl.pallas_call(
        _ffnup_kernel,
        grid=(R // tm, nc),
        in_specs=[
            pl.BlockSpec((tm, D), lambda i, j: (i, 0)),
            pl.BlockSpec((HALO, D), lambda i, j: (jnp.maximum(i * hb - 1, 0), 0)),
            pl.BlockSpec((HALO, D), lambda i, j: (jnp.minimum((i + 1) * hb, nh - 1), 0)),
            pl.BlockSpec((1, 1, D), lambda i, j: (grp(i), 0, sc_part)),
            pl.BlockSpec((1, 1, D), lambda i, j: (grp(i), 0, sh_part)),
            pl.BlockSpec((1, D), lambda i, j: (0, 0)),
            pl.BlockSpec((None, D, tn), lambda i, j: (layer, 0, j)),
            pl.BlockSpec((None, D, tn), lambda i, j: (layer, 0, j + nc)),
            col((FFN_CONV_W, tn), 0), col((FFN_CONV_W, tn), nc),
            col((1, tn), 0), col((1, tn), nc),
            pl.BlockSpec((2, tm, LANES), lambda i, j: (0, i, 0)),
        ],
        out_specs=pl.BlockSpec((tm, tn), lambda i, j: (i, j)),
        out_shape=jax.ShapeDtypeStruct((R, F), BF16),
        scratch_shapes=[pltpu.VMEM((tm + 2 * HALO, D), BF16), pltpu.VMEM((tm + 2 * HALO, tn), F32),
                        pltpu.VMEM((tm + 2 * HALO, tn), F32)],
        compiler_params=_cparams("parallel", "arbitrary"),
        name="ffn_up_conv_gate",
    )(X, X, X, mod3, mod3, g, W, W, cw, cw, cb, cb, edge)


def _mnr_kernel(a_ref, w_ref, x_ref, gate_ref, g_ref, o_ref, acc):
    k = pl.program_id(1)

    @pl.when(k == 0)
    def _():
        acc[...] = jnp.zeros_like(acc)

    acc[...] += jnp.dot(a_ref[...], w_ref[...], preferred_element_type=F32)

    @pl.when(k == pl.num_programs(1) - 1)
    def _():
        y = acc[...]
        ms = jnp.mean(y * y, axis=-1, keepdims=True)
        r = y * lax.rsqrt(ms + NORM_EPS) * g_ref[...]
        o_ref[...] = x_ref[...] + gate_ref[0] * r


def _mnr_call(A, W, X, mod3, g, *, layer, R, B, N, M, gate_part, tk, name):
    K, D = W.shape[1:]
    tm = _pick((N, B * M), 512, 256)
    grp = lambda i: jnp.minimum((i * tm) // N, B)
    return pl.pallas_call(
        _mnr_kernel,
        grid=(R // tm, K // tk),
        in_specs=[
            pl.BlockSpec((tm, tk), lambda i, k: (i, k)),
            pl.BlockSpec((None, tk, D), lambda i, k: (layer, k, 0)),
            pl.BlockSpec((tm, D), lambda i, k: (i, 0)),
            pl.BlockSpec((1, 1, D), lambda i, k: (grp(i), 0, gate_part)),
            pl.BlockSpec((1, D), lambda i, k: (0, 0)),
        ],
        out_specs=pl.BlockSpec((tm, D), lambda i, k: (i, 0)),
        out_shape=jax.ShapeDtypeStruct((R, D), F32),
        scratch_shapes=[pltpu.VMEM((tm, D), F32)],
        compiler_params=_cparams("parallel", "arbitrary"),
        name=name,
    )(A, W, X, mod3, g)


def _diff_kernel(lam_ref, nw_ref, q_ref, *refs, tq, tk, with_lat, lam_init):
    if with_lat:
        kl_ref, kc_ref, vl_ref, vc_ref, o_ref, vt_scr, m_scr, acc_scr = refs
        N = kl_ref.shape[0]
    else:
        kc_ref, vc_ref, o_ref, vt_scr, m_scr, acc_scr = refs
        N = 0
    M = kc_ref.shape[0]
    hd = 2 * DIFF_DH
    n_lat_k = N // tk

    @pl.when(pl.program_id(2) == 0)
    def _():
        for c in range(n_lat_k):
            vt_scr[0:hd, c * tk:(c + 1) * tk] = vl_ref[c * tk:(c + 1) * tk, :].astype(F32).T.astype(BF16)
        vt_scr[0:hd, N:N + M] = vc_ref[...].astype(F32).T.astype(BF16)
        vt_scr[hd:, :] = jnp.ones((vt_scr.shape[0] - hd, N + M), BF16)

    q = q_ref[...]
    lane = lax.broadcasted_iota(jnp.int32, q.shape, 1)
    zero = jnp.zeros_like(q)
    qs = jnp.concatenate([jnp.where(lane < DIFF_DH, q, zero), jnp.where(lane >= DIFF_DH, q, zero)], axis=0)

    m_scr[...] = jnp.full_like(m_scr, NEG_BIG)
    acc_scr[...] = jnp.zeros_like(acc_scr)

    qcols = min(QCOLS, 2 * tq)
    chunks = [(c * tk, tk) for c in range(n_lat_k)] + [(N, M)]
    items = [(k0, kn, t * qcols) for (k0, kn) in chunks for t in range(2 * tq // qcols)]

    def keys(k0, kn):
        return kl_ref[k0:k0 + kn, :] if k0 < N else kc_ref[...]

    def scores(k0, kn, c0):
        return lax.dot_general(keys(k0, kn), qs[c0:c0 + qcols], (((1,), (1,)), ((), ())),
                               preferred_element_type=F32)

    def accumulate(s, k0, kn, c0):
        cols = slice(c0, c0 + qcols)
        m_prev = m_scr[:, cols]
        m_new = jnp.maximum(m_prev, jnp.max(s, axis=0, keepdims=True))
        alpha = jnp.exp2(m_prev - m_new)
        p = jnp.exp2(s - m_new).astype(BF16)
        acc_scr[:, cols] = alpha * acc_scr[:, cols] + jnp.dot(vt_scr[:, k0:k0 + kn], p, preferred_element_type=F32)
        m_scr[:, cols] = m_new

    ahead = min(DIFF_LOOKAHEAD, len(items))
    pending = [scores(*it) for it in items[:ahead]]
    for n, item in enumerate(items):
        if n + ahead < len(items):
            pending.append(scores(*items[n + ahead]))
        accumulate(pending.pop(0), *item)

    lp = lam_ref[...]
    lam = (jnp.exp(jnp.sum(lp[0:1] * lp[1:2], axis=1, keepdims=True))
           - jnp.exp(jnp.sum(lp[2:3] * lp[3:4], axis=1, keepdims=True)) + lam_init)
    acc = acc_scr[...]
    o = acc[:hd, :tq] / acc[hd:hd + 1, :tq] - lam * (acc[:hd, tq:] / acc[hd:hd + 1, tq:])
    ms = jnp.mean(o * o, axis=0, keepdims=True)
    r = o * (lax.rsqrt(ms + NORM_EPS) * (1.0 - lam_init))
    o_ref[...] = (r.T * nw_ref[...]).astype(BF16)


def _diff_call(P, lam_p, nw, *, B, N, M, with_lat, lam_init):
    tk = _pick((N,), 512, 256)
    hw = 2 * DIFF_DH
    cq, ck, cv = COL_DQ // hw, COL_DK // hw, COL_DV // hw
    ctx_blk = B * N // M
    small = lambda: pl.BlockSpec((4, DIFF_DH), lambda b, h, qi: (0, 0))
    lat = lambda col: pl.BlockSpec((N, hw), lambda b, h, qi: (b, col + h))
    ctx = lambda col: pl.BlockSpec((M, hw), lambda b, h, qi: (ctx_blk + b, col + h))
    if with_lat:
        tq = _pick((N,), 512, 256)
        nq, q0, keys = N // tq, 0, N + M
        kv_specs = [lat(ck), ctx(ck), lat(cv), ctx(cv)]
    else:
        tq = M
        nq, q0, keys = 1, B * N // tq, M
        kv_specs = [ctx(ck), ctx(cv)]
    kern = functools.partial(_diff_kernel, tq=tq, tk=tk, with_lat=with_lat, lam_init=lam_init)
    return pl.pallas_call(
        kern,
        grid=(B, N_DIFF_HEADS, nq),
        in_specs=[
            small(),
            pl.BlockSpec((1, hw), lambda b, h, qi: (0, 0)),
            pl.BlockSpec((tq, hw), lambda b, h, qi: (q0 + b * nq + qi, cq + h)),
        ] + kv_specs,
        out_specs=pl.BlockSpec((tq, hw), lambda b, h, qi: (b * nq + qi, h)),
        out_shape=jax.ShapeDtypeStruct((B * nq * tq, DIFF_WIDTH), BF16),
        scratch_shapes=[pltpu.VMEM((hw + HALO, keys), BF16), pltpu.VMEM((1, 2 * tq), F32),
                        pltpu.VMEM((hw + HALO, 2 * tq), F32)],
        compiler_params=_cparams("parallel", "parallel", "arbitrary"),
        name="diff_attn" if with_lat else "diff_attn_ctx",
    )(lam_p, nw, P, *([P] * len(kv_specs)))


def _win_kernel(sink_ref, q_ref, kp_ref, kc_ref, kn_ref, kx_ref, o_ref, *, nb, N, M):
    i = pl.program_id(1)
    wb, kw = WIN_BLOCK, WIN_KV_WIDTH
    q = q_ref[...]
    kcat = jnp.concatenate([kp_ref[:, :kw], kc_ref[:, :kw], kn_ref[:, :kw], kx_ref[:, :kw]], axis=0)
    vcat = jnp.concatenate([kp_ref[:, kw:], kc_ref[:, kw:], kn_ref[:, kw:], kx_ref[:, kw:]], axis=0)
    nk = 3 * wb + M
    rows = WIN_G * wb
    row = lax.broadcasted_iota(jnp.int32, (rows, nk), 0) % wb
    col = lax.broadcasted_iota(jnp.int32, (rows, nk), 1)
    kpos = (i - 1) * wb + col
    n_band = jnp.where(i < nb, N, 0)
    band_ok = (jnp.abs(col - wb - row) <= WINDOW) & (kpos >= 0) & (kpos < n_band)
    valid = band_ok | (col >= 3 * wb)
    lane = lax.broadcasted_iota(jnp.int32, (wb, kw), 1) // WIN_DH
    lane_o = lax.broadcasted_iota(jnp.int32, (rows, kw), 1) // WIN_DH
    grow = lax.broadcasted_iota(jnp.int32, (rows, 1), 0) // wb
    zero = jnp.zeros((wb, kw), BF16)
    acc = jnp.zeros((rows, kw), F32)

    def scores(kv):
        qs = jnp.concatenate([jnp.where(lane == kv, q[:, g * kw:(g + 1) * kw], zero) for g in range(WIN_G)], axis=0)
        return lax.dot_general(qs, kcat, (((1,), (1,)), ((), ())), preferred_element_type=F32)

    pending = [scores(kv) for kv in range(WIN_LOOKAHEAD)]
    for kv in range(WIN_KV_HEADS):
        if kv + WIN_LOOKAHEAD < WIN_KV_HEADS:
            pending.append(scores(kv + WIN_LOOKAHEAD))
        s = jnp.where(valid, pending.pop(0), NEG_BIG)
        sink = jnp.zeros((rows, 1), F32)
        for g in range(WIN_G):
            sink = jnp.where(grow == g, sink_ref[kv * WIN_G + g] * LOG2E, sink)
        m = jnp.maximum(jnp.max(s, axis=1, keepdims=True), sink)
        p = jnp.exp2(s - m)
        l = jnp.sum(p, axis=1, keepdims=True) + jnp.exp2(sink - m)
        o = jnp.dot(p.astype(BF16), vcat, preferred_element_type=F32) / l
        acc = jnp.where(lane_o == kv, o, acc)
    for g in range(WIN_G):
        o_ref[:, g * kw:(g + 1) * kw] = acc[g * wb:(g + 1) * wb].astype(BF16)


def _win_call(P, sink, *, B, N, M, need_ctx):
    wb = WIN_BLOCK
    nb, nbc = N // wb, (M // wb if need_ctx else 0)
    R = B * N + (B * M if need_ctx else 0)
    cq, ckv = COL_WQ // PW, COL_WKV // (2 * WIN_KV_WIDTH)

    def qrow(b, i):
        return jnp.where(i < nb, b * nb + i, B * nb + b * nbc + (i - nb))

    def band(b, i, d):
        return b * nb + jnp.clip(i + d, 0, nb - 1)

    kvw = 2 * WIN_KV_WIDTH
    kern = functools.partial(_win_kernel, nb=nb, N=N, M=M)
    grid_spec = pltpu.PrefetchScalarGridSpec(
        num_scalar_prefetch=1,
        grid=(B, nb + nbc),
        in_specs=[
            pl.BlockSpec((wb, PW), lambda b, i, s: (qrow(b, i), cq)),
            pl.BlockSpec((wb, kvw), lambda b, i, s: (band(b, i, -1), ckv)),
            pl.BlockSpec((wb, kvw), lambda b, i, s: (band(b, i, 0), ckv)),
            pl.BlockSpec((wb, kvw), lambda b, i, s: (band(b, i, 1), ckv)),
            pl.BlockSpec((M, kvw), lambda b, i, s: (B * N // M + b, ckv)),
        ],
        out_specs=pl.BlockSpec((wb, PW), lambda b, i, s: (qrow(b, i), 0)),
    )
    return pl.pallas_call(
        kern,
        grid_spec=grid_spec,
        out_shape=jax.ShapeDtypeStruct((R, WIN_WIDTH), BF16),
        compiler_params=_cparams("parallel", "arbitrary"),
        name="win_attn",
    )(sink, P, P, P, P, P)


def _seq_edges(i, R_lat, N, M):
    r0 = i * CONV_ROWS
    lat = r0 < R_lat
    first = jnp.where(lat, r0 % N == 0, (r0 - R_lat) % M == 0)
    last = jnp.where(lat, (r0 + CONV_ROWS) % N == 0, (r0 + CONV_ROWS - R_lat) % M == 0)
    return first, last


def _conv_taps(buf, main_ref, prev_ref, next_ref, w_ref, b_ref, first, last, k):
    buf[0:HALO] = jnp.where(first, 0.0, prev_ref[...].astype(F32))
    buf[HALO:HALO + CONV_ROWS] = main_ref[...].astype(F32)
    buf[HALO + CONV_ROWS:] = jnp.where(last, 0.0, next_ref[...].astype(F32))
    out = b_ref[...]
    for j in range(k):
        out = out + buf[pl.ds(HALO - k // 2 + j, CONV_ROWS), :] * w_ref[j:j + 1, :]
    return out


def _ssdconv_kernel(m_ref, p_ref, n_ref, w_ref, b_ref, o_ref, buf, *, R_lat, N, M):
    first, last = _seq_edges(pl.program_id(0), R_lat, N, M)
    o_ref[...] = _silu(_conv_taps(buf, m_ref, p_ref, n_ref, w_ref, b_ref, first, last, SSD_CONV_W)).astype(BF16)


def _ssdconv_call(P, w, b, *, B, N, M):
    T = P.shape[0]
    cr, W = CONV_ROWS, SSD_XBC
    hb = cr // HALO
    cb = COL_XBC // W
    nh = T // HALO
    kern = functools.partial(_ssdconv_kernel, R_lat=B * N, N=N, M=M)
    return pl.pallas_call(
        kern,
        grid=(T // cr,),
        in_specs=[
            pl.BlockSpec((cr, W), lambda i: (i, cb)),
            pl.BlockSpec((HALO, W), lambda i: (jnp.maximum(i * hb - 1, 0), cb)),
            pl.BlockSpec((HALO, W), lambda i: (jnp.minimum((i + 1) * hb, nh - 1), cb)),
            pl.BlockSpec((SSD_CONV_W, W), lambda i: (0, 0)),
            pl.BlockSpec((1, W), lambda i: (0, 0)),
        ],
        out_specs=pl.BlockSpec((cr, W), lambda i: (i, 0)),
        out_shape=jax.ShapeDtypeStruct((T, W), BF16),
        scratch_shapes=[pltpu.VMEM((cr + 2 * HALO, W), F32)],
        compiler_params=_cparams("parallel"),
        name="ssd_conv",
    )(P, P, P, w, b)


def _cumsum_rows(a):
    n = a.shape[0]
    tri = (lax.broadcasted_iota(jnp.int32, (n, n), 0) >= lax.broadcasted_iota(jnp.int32, (n, n), 1)).astype(BF16)
    hi = a.astype(BF16)
    r1 = a - hi.astype(F32)
    mid = r1.astype(BF16)
    lo = (r1 - mid.astype(F32)).astype(BF16)
    dot = lambda v: jnp.dot(tri, v, preferred_element_type=F32)
    return dot(hi) + dot(mid) + dot(lo)


def _ssd_prep(dt_ref, bias_ref, alog_ref, e_ref, *, reverse):
    ch = SSD_CHUNK
    x = dt_ref[...].astype(F32) + bias_ref[...]
    dt = jnp.maximum(x, 0.0) + jnp.log(1.0 + jnp.exp(-jnp.abs(x)))
    ad = dt * (-jnp.exp(alog_ref[...]))
    acum = _cumsum_rows(ad)
    total = acum[ch - 1:ch, :]
    if reverse:
        cvec = acum - ad
        e_off = jnp.exp(total - cvec)
        e_ws = jnp.exp(cvec)
    else:
        cvec = acum
        e_off = jnp.exp(acum)
        e_ws = jnp.exp(total - acum)
    tot_rows = jnp.broadcast_to(jnp.exp(total), (HALO, LANES))
    scales = jnp.dot(jnp.concatenate([dt * e_ws, e_off, tot_rows], axis=0).astype(BF16), e_ref[...],
                     preferred_element_type=F32)
    return cvec, cvec.T, dt.T, scales


def _ssd_group(g, prep, u_ref, s_ref, y_ref, *, reverse, lane0):
    cvec, cT, dtT, scales = prep
    ch, hp = SSD_CHUNK, SSD_HEAD_DIM
    hpg = SSD_HEADS // SSD_GROUPS
    gw = hpg * hp
    li = lax.broadcasted_iota(jnp.int32, (ch, ch), 0)
    si = lax.broadcasted_iota(jnp.int32, (ch, ch), 1)
    tri = (si >= li) if reverse else (li >= si)
    hsel = lax.broadcasted_iota(jnp.int32, (ch, gw), 1) // hp
    lanes = slice(g * gw, (g + 1) * gw)
    bg = u_ref[:, SSD_INNER + g * SSD_STATE:SSD_INNER + (g + 1) * SSD_STATE]
    cg = u_ref[:, SSD_INNER + (SSD_GROUPS + g) * SSD_STATE:SSD_INNER + (SSD_GROUPS + g + 1) * SSD_STATE]
    xs = u_ref[:, lanes]
    cb = lax.dot_general(cg, bg, (((1,), (1,)), ((), ())), preferred_element_type=F32)
    sg = s_ref[:, lanes]
    yoff = jnp.dot(cg, sg.astype(BF16), preferred_element_type=F32) * scales[ch:2 * ch, lanes]
    xw = (xs.astype(F32) * scales[0:ch, lanes]).astype(BF16)
    s_ref[:, lanes] = (sg * scales[2 * ch:2 * ch + 1, lanes]
                       + lax.dot_general(bg, xw, (((0,), (0,)), ((), ())), preferred_element_type=F32))
    ydiag = jnp.zeros((ch, gw), F32)
    for r in range(hpg):
        j = lane0 + g * hpg + r
        colv = jnp.broadcast_to(cvec[:, j:j + 1], (ch, ch))
        rowv = jnp.broadcast_to(cT[j:j + 1, :], (ch, ch))
        d = (rowv - colv) if reverse else (colv - rowv)
        lm = jnp.exp(jnp.where(tri, d, NEG_BIG)) * dtT[j:j + 1, :]
        yh = jnp.dot((cb * lm).astype(BF16), xs, preferred_element_type=F32)
        ydiag = jnp.where(hsel == r, yh, ydiag)
    y_ref[:, lanes] = (ydiag + yoff).astype(BF16)


def _ssd_kernel(uf_ref, dtf_ref, ub_ref, dtb_ref, bias_ref, alog_ref, e_ref, yf_ref, yb_ref, sf, sb):
    @pl.when(pl.program_id(1) == 0)
    def _():
        sf[...] = jnp.zeros_like(sf)
        sb[...] = jnp.zeros_like(sb)

    pf = _ssd_prep(dtf_ref, bias_ref, alog_ref, e_ref.at[0], reverse=False)
    pb = _ssd_prep(dtb_ref, bias_ref, alog_ref, e_ref.at[1], reverse=True)
    for g in range(SSD_GROUPS):
        _ssd_group(g, pf, uf_ref, sf, yf_ref, reverse=False, lane0=0)
        _ssd_group(g, pb, ub_ref, sb, yb_ref, reverse=True, lane0=SSD_HEADS)


def _ssd_call(U, P, bias, alog, spread, *, B, N, M):
    T = U.shape[0]
    ch = SSD_CHUNK
    ncl, ncc = N // ch, M // ch
    cdt = COL_DT // LANES

    def fwd(b, t):
        return jnp.where(t < ncc, B * ncl + b * ncc + t, b * ncl + (t - ncc))

    def bwd(b, t):
        return jnp.where(t < ncc, B * ncl + b * ncc + (ncc - 1 - t), b * ncl + (ncl - 1 - (t - ncc)))

    return pl.pallas_call(
        _ssd_kernel,
        grid=(B, ncl + ncc),
        in_specs=[
            pl.BlockSpec((ch, SSD_XBC), lambda b, t: (fwd(b, t), 0)),
            pl.BlockSpec((ch, LANES), lambda b, t: (fwd(b, t), cdt)),
            pl.BlockSpec((ch, SSD_XBC), lambda b, t: (bwd(b, t), 0)),
            pl.BlockSpec((ch, LANES), lambda b, t: (bwd(b, t), cdt)),
            pl.BlockSpec((1, LANES), lambda b, t: (0, 0)),
            pl.BlockSpec((1, LANES), lambda b, t: (0, 0)),
            pl.BlockSpec((2, LANES, SSD_INNER), lambda b, t: (0, 0, 0)),
        ],
        out_specs=[
            pl.BlockSpec((ch, SSD_INNER), lambda b, t: (fwd(b, t), 0)),
            pl.BlockSpec((ch, SSD_INNER), lambda b, t: (bwd(b, t), 0)),
        ],
        out_shape=[jax.ShapeDtypeStruct((T, SSD_INNER), BF16)] * 2,
        scratch_shapes=[pltpu.VMEM((SSD_STATE, SSD_INNER), F32)] * 2,
        compiler_params=_cparams("parallel", "arbitrary"),
        name="ssd_scan",
    )(U, P, U, P, bias, alog, spread)


def _merge_kernel(yd_ref, yf_ref, yb_ref, xs_ref, z_ref, yw_ref, gd_ref, gs_ref, gw_ref,
                  wd_ref, ws_ref, ww_ref, dsk_ref, sn_ref, o_ref):
    y = yf_ref[...].astype(F32) + yb_ref[...].astype(F32) + dsk_ref[...] * xs_ref[...].astype(F32)
    y = y * _silu(z_ref[...].astype(F32))
    ms = jnp.mean(y * y, axis=-1, keepdims=True)
    s = (y * lax.rsqrt(ms + NORM_EPS) * sn_ref[...]).astype(BF16)
    dot = lambda a, w: jnp.dot(a, w[...], preferred_element_type=F32)
    sig = lambda r: _sigmoid(r[...].astype(F32))
    acc = sig(gd_ref) * dot(yd_ref[...], wd_ref)
    acc = acc + sig(gs_ref) * dot(s, ws_ref)
    acc = acc + sig(gw_ref) * dot(yw_ref[...], ww_ref)
    o_ref[...] = acc.astype(BF16)


def _merge_call(YD, YF, YB, U, P, YW, wd, ws, ww, dsk, sn, *, layer, R, D):
    tm = 256
    W = SSD_INNER
    cg = COL_GL // D
    row = lambda c: pl.BlockSpec((tm, W), lambda i: (i, c))
    gate = lambda c: pl.BlockSpec((tm, D), lambda i: (i, cg + c))
    wspec = pl.BlockSpec((None, W, D), lambda i: (layer, 0, 0))
    vec = pl.BlockSpec((1, W), lambda i: (0, 0))
    return pl.pallas_call(
        _merge_kernel,
        grid=(R // tm,),
        in_specs=[row(0), row(0), row(0), row(0), row(COL_SZ // W), row(0),
                  gate(0), gate(1), gate(2), wspec, wspec, wspec, vec, vec],
        out_specs=pl.BlockSpec((tm, D), lambda i: (i, 0)),
        out_shape=jax.ShapeDtypeStruct((R, D), BF16),
        compiler_params=_cparams("parallel"),
        name="branch_merge",
    )(YD, YF, YB, U, P, YW, P, P, P, wd, ws, ww, dsk, sn)


def _rope_tables(N, rows_ctx):
    hd = DIFF_DH
    pos = jnp.arange(N)
    row = (pos // GRID_W).astype(F32)
    colp = (pos % GRID_W).astype(F32)
    axis_dim = hd // 2
    inv = ROPE_BASE ** (-jnp.arange(0, axis_dim, 2, dtype=F32) / axis_dim)
    ang = jnp.concatenate([row[:, None] * inv, colp[:, None] * inv], axis=-1)
    lane = jnp.arange(LANES)
    cos = jnp.cos(ang)[:, lane % (hd // 2)]
    sin = jnp.sin(ang)[:, lane % (hd // 2)]
    first = (lane % hd) < hd // 2
    lat = jnp.stack([cos, jnp.where(first, -sin, 0.0), jnp.where(first, 0.0, sin)])
    ctx = jnp.stack([jnp.ones((rows_ctx, LANES), F32), jnp.zeros((rows_ctx, LANES), F32),
                     jnp.zeros((rows_ctx, LANES), F32)])
    k_tab = jnp.concatenate([lat, ctx], axis=1)
    q_tab = k_tab * (DIFF_DH ** -0.5 * LOG2E)
    return jnp.stack([q_tab, k_tab])


def _edge_table(B, N, M):
    pos = jnp.concatenate([jnp.tile(jnp.arange(N), B), jnp.tile(jnp.arange(M), B)])
    last = jnp.concatenate([jnp.full((B * N,), N - 1), jnp.full((B * M,), M - 1)])
    t = jnp.stack([pos != 0, pos != last]).astype(F32)
    return jnp.broadcast_to(t[:, :, None], t.shape + (LANES,))


def _layout_w_in(w):
    L, D = w.shape[:2]
    sizes = (DIFF_WIDTH, DIFF_WIDTH, DIFF_WIDTH, SSD_INNER, SSD_XBC, 2 * SSD_HEADS,
             WIN_WIDTH, WIN_KV_WIDTH, WIN_KV_WIDTH, 3 * D)
    parts, s = [], 0
    for z in sizes:
        parts.append(w[..., s:s + z])
        s += z
    dq, dk, dv, sz, sxbc, sdt, wq, wk, wv, gl = parts
    wq = wq.reshape(L, D, WIN_KV_HEADS, WIN_G, WIN_DH).transpose(0, 1, 3, 2, 4).reshape(L, D, WIN_WIDTH)
    pad = jnp.zeros((L, D, PW - 2 * WIN_KV_WIDTH - 2 * SSD_HEADS), w.dtype)
    return jnp.concatenate([dq, dk, wq, wk, wv, sdt, pad, dv, sz, sxbc, gl], axis=-1).astype(BF16)


def kernel(x, c, ctx, c_ctx, w_ada, b_ada, norm_g, w_in, diff_lambda, diff_norm, ssd_conv_w, ssd_conv_b,
           ssd_a_log, ssd_dt_bias, ssd_d, ssd_norm, win_sink, w_br_diff, w_br_ssd, w_br_win, w_out,
           ffn_w_up, ffn_conv_w, ffn_conv_b, ffn_w_down):
    B, N, D = x.shape
    M = ctx.shape[1]
    L = w_ada.shape[0]
    assert D == 2 * PW and P_WIDTH == COL_GL + 3 * D
    assert N % CONV_ROWS == 0 and M % CONV_ROWS == 0 and (B * N) % M == 0 and B + 1 <= 16
    T = B * N + B * M

    X = jnp.concatenate([x.reshape(B * N, D), ctx.reshape(B * M, D)], axis=0)
    cv = jnp.zeros((16, D), F32).at[:B].set(c).at[B].set(c_ctx)
    mod = _mod_call(cv, w_ada, b_ada)

    tm_in = _pick((N, B * M), 1024, 512, 256)
    tab = _rope_tables(N, tm_in)
    kinds = jnp.asarray(COL_KINDS, jnp.int32)
    edge = _edge_table(B, N, M)
    head_of_lane = jnp.arange(SSD_INNER) // SSD_HEAD_DIM
    spread = jnp.stack([(jnp.arange(LANES)[:, None] == head_of_lane[None, :] + d * SSD_HEADS)
                        for d in range(2)]).astype(BF16)
    pad_lanes = LANES - 2 * SSD_HEADS

    W_in = _layout_w_in(w_in)
    W_bd, W_bs, W_out = w_br_diff.astype(BF16), w_br_ssd.astype(BF16), w_out.astype(BF16)
    W_bw = (w_br_win.reshape(L, WIN_KV_HEADS, WIN_G, WIN_DH, D).transpose(0, 2, 1, 3, 4)
            .reshape(L, WIN_WIDTH, D).astype(BF16))
    W_up, W_down = ffn_w_up.astype(BF16), ffn_w_down.astype(BF16)

    for l in range(L):
        need_ctx = l < L - 1
        lam_init = 0.8 - 0.6 * math.exp(-0.3 * l)
        R = T if need_ctx else B * N
        mod3 = mod[l].reshape(16, 1, 6 * D)
        kw0 = dict(B=B, N=N, M=M)
        kw = dict(layer=l, **kw0)

        P = _inproj_call(X, mod3, norm_g[l, 0:1], W_in, tab, kinds, sc_part=1, sh_part=0, **kw)

        dargs = (P, diff_lambda[l], diff_norm[l].reshape(1, -1))
        YD = _diff_call(*dargs, with_lat=True, lam_init=lam_init, **kw0)
        if need_ctx:
            YD = jnp.concatenate([YD, _diff_call(*dargs, with_lat=False, lam_init=lam_init, **kw0)], axis=0)
        YW = _win_call(P, win_sink[l], need_ctx=need_ctx, **kw0)

        U = _ssdconv_call(P, ssd_conv_w[l], ssd_conv_b[l].reshape(1, -1), **kw0)
        bias = jnp.pad(ssd_dt_bias[l].reshape(1, -1), ((0, 0), (0, pad_lanes)))
        alog = jnp.pad(ssd_a_log[l].reshape(1, -1), ((0, 0), (0, pad_lanes)))
        YF, YB = _ssd_call(U, P, bias, alog, spread, **kw0)

        dsk = jnp.repeat(ssd_d[l], SSD_HEAD_DIM).reshape(1, -1)
        Z = _merge_call(YD, YF, YB, U, P, YW, W_bd, W_bs, W_bw, dsk, ssd_norm[l].reshape(1, -1), layer=l, R=R, D=D)
        X1 = _mnr_call(Z, W_out, X, mod3, norm_g[l, 1:2], R=R, gate_part=2, tk=1024, name="out_proj", **kw)

        A = _ffnup_call(X1, mod3, norm_g[l, 2:3], W_up, ffn_conv_w[l], ffn_conv_b[l].reshape(1, -1), edge,
                        R=R, sc_part=4, sh_part=3, **kw)
        X = _mnr_call(A, W_down, X1, mod3, norm_g[l, 3:4], R=R, gate_part=5,
                      tk=_pick((A.shape[1],), 1408, 512, 128), name="ffn_down", **kw)

    return X[:B * N].reshape(B, N, D)
```

```python
import functools
import math

import jax
import jax.numpy as jnp
from jax import lax
from jax.experimental import pallas as pl
from jax.experimental.pallas import tpu as pltpu

F32 = jnp.float32
BF16 = jnp.bfloat16

GRID_W = 64
ROPE_BASE = 10000.0
NORM_EPS = 1e-6
N_DIFF_HEADS = 8
DIFF_DH = 64
DIFF_WIDTH = N_DIFF_HEADS * 2 * DIFF_DH
SSD_INNER = 1024
SSD_HEAD_DIM = 64
SSD_HEADS = 16
SSD_GROUPS = 4
SSD_STATE = 128
SSD_CONV_W = 5
SSD_CHUNK = 128
SSD_XBC = SSD_INNER + 2 * SSD_GROUPS * SSD_STATE
WIN_HEADS = 16
WIN_KV_HEADS = 4
WIN_G = WIN_HEADS // WIN_KV_HEADS
WIN_DH = 64
WINDOW = 128
WIN_BLOCK = 128
WIN_WIDTH = WIN_HEADS * WIN_DH
WIN_KV_WIDTH = WIN_KV_HEADS * WIN_DH
FFN_CONV_W = 3

LOG2E = 1.4426950408889634
NEG_BIG = -1e30

LANES = 128
HALO = 16
QCOLS = 256
DIFF_LOOKAHEAD = 3
WIN_LOOKAHEAD = 2
CONV_ROWS = 256
VMEM_LIMIT = 56 * 1024 * 1024

PW = 1024
COL_DQ = 0 * PW
COL_DK = 1 * PW
COL_WQ = 2 * PW
COL_WKV = 3 * PW
COL_DT = COL_WKV + 2 * WIN_KV_WIDTH
COL_DV = 4 * PW
COL_SZ = 5 * PW
COL_XBC = 6 * PW
COL_GL = 8 * PW
P_WIDTH = 14 * PW
KIND_NONE, KIND_Q, KIND_K, KIND_K256 = 0, 1, 2, 3
COL_KINDS = (KIND_Q, KIND_K, KIND_Q, KIND_K256) + (KIND_NONE,) * 10


def _cparams(*sem):
    return pltpu.CompilerParams(dimension_semantics=sem, vmem_limit_bytes=VMEM_LIMIT)


def _sigmoid(x):
    return 1.0 / (1.0 + jnp.exp(-x))


def _silu(x):
    return x * _sigmoid(x)


def _pick(n, *cands):
    for c in cands:
        if all(v % c == 0 for v in n):
            return c
    raise ValueError(f"no block size among {cands} divides {n}")


def _mod_kernel(cv_ref, w_ref, b_ref, o_ref):
    s = _silu(cv_ref[...]).astype(BF16)
    o_ref[0] = jnp.dot(s, w_ref[0].astype(BF16), preferred_element_type=F32) + b_ref[0]


def _mod_call(cv, w_ada, b_ada):
    L, D, W6 = w_ada.shape
    tn = 1024
    return pl.pallas_call(
        _mod_kernel,
        grid=(L, W6 // tn),
        in_specs=[
            pl.BlockSpec((16, D), lambda l, j: (0, 0)),
            pl.BlockSpec((1, D, tn), lambda l, j: (l, 0, j)),
            pl.BlockSpec((1, 1, tn), lambda l, j: (l, 0, j)),
        ],
        out_specs=pl.BlockSpec((1, 16, tn), lambda l, j: (l, 0, j)),
        out_shape=jax.ShapeDtypeStruct((L, 16, W6), F32),
        compiler_params=_cparams("parallel", "parallel"),
        name="adaln_mod",
    )(cv, w_ada, b_ada.reshape(L, 1, W6))


def _norm_mod(x, g, sc, sh):
    ms = jnp.mean(x * x, axis=-1, keepdims=True)
    y = x * lax.rsqrt(ms + NORM_EPS) * g
    return y * (1.0 + sc) + sh


def _rope(y, cos, sa, sb):
    w = y.shape[1]
    rep = w // LANES
    if rep > 1:
        cos, sa, sb = (jnp.tile(t, (1, rep)) for t in (cos, sa, sb))
    up = pltpu.roll(y, w - DIFF_DH // 2, axis=1)
    dn = pltpu.roll(y, DIFF_DH // 2, axis=1)
    return y * cos + up * sa + dn * sb


def _inproj_kernel(kind_ref, x_ref, sc_ref, sh_ref, g_ref, w_ref, tab_ref, o_ref, h_scr):
    j = pl.program_id(1)

    @pl.when(j == 0)
    def _():
        h_scr[...] = _norm_mod(x_ref[...], g_ref[...], sc_ref[0], sh_ref[0]).astype(BF16)

    dot = lambda: jnp.dot(h_scr[...], w_ref[...], preferred_element_type=F32)
    kind = kind_ref[j]

    @pl.when(kind == KIND_NONE)
    def _():
        o_ref[...] = dot().astype(BF16)

    @pl.when(jnp.logical_or(kind == KIND_Q, kind == KIND_K))
    def _():
        o_ref[...] = _rope(dot(), tab_ref[0, 0], tab_ref[0, 1], tab_ref[0, 2]).astype(BF16)

    @pl.when(kind == KIND_K256)
    def _():
        kw = WIN_KV_WIDTH
        y = dot()
        o_ref[:, :kw] = _rope(y[:, :kw], tab_ref[0, 0], tab_ref[0, 1], tab_ref[0, 2]).astype(BF16)
        o_ref[:, kw:] = y[:, kw:].astype(BF16)


def _inproj_call(X, mod3, g, W, tab, kinds, *, layer, B, N, M, sc_part, sh_part):
    T, D = X.shape
    tm = _pick((N, B * M), 1024, 512, 256)
    nlat = N // tm
    grp = lambda i: jnp.minimum((i * tm) // N, B)
    rblk = lambda i: jnp.where(i < B * nlat, i % nlat, nlat)
    ksel = lambda k: jnp.where(jnp.logical_or(k == KIND_K, k == KIND_K256), 1, 0)
    grid_spec = pltpu.PrefetchScalarGridSpec(
        num_scalar_prefetch=1,
        grid=(T // tm, P_WIDTH // PW),
        in_specs=[
            pl.BlockSpec((tm, D), lambda i, j, kr: (i, 0)),
            pl.BlockSpec((1, 1, D), lambda i, j, kr: (grp(i), 0, sc_part)),
            pl.BlockSpec((1, 1, D), lambda i, j, kr: (grp(i), 0, sh_part)),
            pl.BlockSpec((1, D), lambda i, j, kr: (0, 0)),
            pl.BlockSpec((None, D, PW), lambda i, j, kr: (layer, 0, j)),
            pl.BlockSpec((1, 3, tm, LANES), lambda i, j, kr: (ksel(kr[j]), 0, rblk(i), 0)),
        ],
        out_specs=pl.BlockSpec((tm, PW), lambda i, j, kr: (i, j)),
        scratch_shapes=[pltpu.VMEM((tm, D), BF16)],
    )
    return pl.pallas_call(
        _inproj_kernel,
        grid_spec=grid_spec,
        out_shape=jax.ShapeDtypeStruct((T, P_WIDTH), BF16),
        compiler_params=_cparams("parallel", "arbitrary"),
        name="in_proj",
    )(kinds, X, mod3, mod3, g, W, tab)


def _ffnup_kernel(x_ref, xp_ref, xn_ref, sc_ref, sh_ref, g_ref, wa_ref, wg_ref, cwa_ref, cwg_ref, ba_ref, bg_ref,
                  edge_ref, o_ref, h_scr, bufa, bufg):
    tm = x_ref.shape[0]

    @pl.when(pl.program_id(1) == 0)
    def _():
        nm = lambda r: _norm_mod(r[...], g_ref[...], sc_ref[0], sh_ref[0]).astype(BF16)
        h_scr[0:HALO] = nm(xp_ref)
        h_scr[HALO:HALO + tm] = nm(x_ref)
        h_scr[HALO + tm:] = nm(xn_ref)

    rep = o_ref.shape[1] // LANES
    prev_ok = jnp.tile(edge_ref[0], (1, rep))
    next_ok = jnp.tile(edge_ref[1], (1, rep))

    def conv(buf, w_ref, cw_ref, b_ref):
        buf[...] = jnp.dot(h_scr[...], w_ref[...], preferred_element_type=F32)
        taps = [buf[pl.ds(HALO - 1 + j, tm), :] for j in range(FFN_CONV_W)]
        out = b_ref[...] + (taps[0] * prev_ok) * cw_ref[0:1, :]
        out = out + taps[1] * cw_ref[1:2, :]
        return out + (taps[2] * next_ok) * cw_ref[2:3, :]

    a = conv(bufa, wa_ref, cwa_ref, ba_ref)
    g = conv(bufg, wg_ref, cwg_ref, bg_ref)
    o_ref[...] = (_silu(a) * g).astype(BF16)


def _ffnup_call(X, mod3, g, W, cw, cb, edge, *, layer, R, B, N, M, sc_part, sh_part):
    D = X.shape[1]
    F = W.shape[2] // 2
    tm = _pick((N, B * M), 1024, 512, 256)
    tn = _pick((F,), 512, 128)
    nc = F // tn
    hb = tm // HALO
    nh = R // HALO
    grp = lambda i: jnp.minimum((i * tm) // N, B)
    col = lambda shape, off: pl.BlockSpec(shape, lambda i, j: (0, j + off))
    return pl.pallas_call(
        _ffnup_kernel,
        grid=(R // tm, nc),
        in_specs=[
            pl.BlockSpec((tm, D), lambda i, j: (i, 0)),
            pl.BlockSpec((HALO, D), lambda i, j: (jnp.maximum(i * hb - 1, 0), 0)),
            pl.BlockSpec((HALO, D), lambda i, j: (jnp.minimum((i + 1) * hb, nh - 1), 0)),
            pl.BlockSpec((1, 1, D), lambda i, j: (grp(i), 0, sc_part)),
            pl.BlockSpec((1, 1, D), lambda i, j: (grp(i), 0, sh_part)),
            pl.BlockSpec((1, D), lambda i, j: (0, 0)),
            pl.BlockSpec((None, D, tn), lambda i, j: (layer, 0, j)),
            pl.BlockSpec((None, D, tn), lambda i, j: (layer, 0, j + nc)),
            col((FFN_CONV_W, tn), 0), col((FFN_CONV_W, tn), nc),
            col((1, tn), 0), col((1, tn), nc),
            pl.BlockSpec((2, tm, LANES), lambda i, j: (0, i, 0)),
        ],
        out_specs=pl.BlockSpec((tm, tn), lambda i, j: (i, j)),
        out_shape=jax.ShapeDtypeStruct((R, F), BF16),
        scratch_shapes=[pltpu.VMEM((tm + 2 * HALO, D), BF16), pltpu.VMEM((tm + 2 * HALO, tn), F32),
                        pltpu.VMEM((tm + 2 * HALO, tn), F32)],
        compiler_params=_cparams("parallel", "arbitrary"),
        name="ffn_up_conv_gate",
    )(X, X, X, mod3, mod3, g, W, W, cw, cw, cb, cb, edge)


def _mnr_kernel(a_ref, w_ref, x_ref, gate_ref, g_ref, o_ref, acc):
    k = pl.program_id(1)

    @pl.when(k == 0)
    def _():
        acc[...] = jnp.zeros_like(acc)

    acc[...] += jnp.dot(a_ref[...], w_ref[...], preferred_element_type=F32)

    @pl.when(k == pl.num_programs(1) - 1)
    def _():
        y = acc[...]
        ms = jnp.mean(y * y, axis=-1, keepdims=True)
        r = y * lax.rsqrt(ms + NORM_EPS) * g_ref[...]
        o_ref[...] = x_ref[...] + gate_ref[0] * r


def _mnr_call(A, W, X, mod3, g, *, layer, R, B, N, M, gate_part, tk, name):
    K, D = W.shape[1:]
    tm = _pick((N, B * M), 512, 256)
    grp = lambda i: jnp.minimum((i * tm) // N, B)
    return pl.pallas_call(
        _mnr_kernel,
        grid=(R // tm, K // tk),
        in_specs=[
            pl.BlockSpec((tm, tk), lambda i, k: (i, k)),
            pl.BlockSpec((None, tk, D), lambda i, k: (layer, k, 0)),
            pl.BlockSpec((tm, D), lambda i, k: (i, 0)),
            pl.BlockSpec((1, 1, D), lambda i, k: (grp(i), 0, gate_part)),
            pl.BlockSpec((1, D), lambda i, k: (0, 0)),
        ],
        out_specs=pl.BlockSpec((tm, D), lambda i, k: (i, 0)),
        out_shape=jax.ShapeDtypeStruct((R, D), F32),
        scratch_shapes=[pltpu.VMEM((tm, D), F32)],
        compiler_params=_cparams("parallel", "arbitrary"),
        name=name,
    )(A, W, X, mod3, g)


def _diff_kernel(lam_ref, nw_ref, q_ref, *refs, tq, tk, with_lat, lam_init):
    if with_lat:
        kl_ref, kc_ref, vl_ref, vc_ref, o_ref, vt_scr, m_scr, acc_scr = refs
        N = kl_ref.shape[0]
    else:
        kc_ref, vc_ref, o_ref, vt_scr, m_scr, acc_scr = refs
        N = 0
    M = kc_ref.shape[0]
    hd = 2 * DIFF_DH
    n_lat_k = N // tk

    @pl.when(pl.program_id(2) == 0)
    def _():
        for c in range(n_lat_k):
            vt_scr[0:hd, c * tk:(c + 1) * tk] = vl_ref[c * tk:(c + 1) * tk, :].astype(F32).T.astype(BF16)
        vt_scr[0:hd, N:N + M] = vc_ref[...].astype(F32).T.astype(BF16)
        vt_scr[hd:, :] = jnp.ones((vt_scr.shape[0] - hd, N + M), BF16)

    q = q_ref[...]
    lane = lax.broadcasted_iota(jnp.int32, q.shape, 1)
    zero = jnp.zeros_like(q)
    qs = jnp.concatenate([jnp.where(lane < DIFF_DH, q, zero), jnp.where(lane >= DIFF_DH, q, zero)], axis=0)

    m_scr[...] = jnp.full_like(m_scr, NEG_BIG)
    acc_scr[...] = jnp.zeros_like(acc_scr)

    qcols = min(QCOLS, 2 * tq)
    chunks = [(c * tk, tk) for c in range(n_lat_k)] + [(N, M)]
    items = [(k0, kn, t * qcols) for (k0, kn) in chunks for t in range(2 * tq // qcols)]

    def keys(k0, kn):
        return kl_ref[k0:k0 + kn, :] if k0 < N else kc_ref[...]

    def scores(k0, kn, c0):
        return lax.dot_general(keys(k0, kn), qs[c0:c0 + qcols], (((1,), (1,)), ((), ())),
                               preferred_element_type=F32)

    def accumulate(s, k0, kn, c0):
        cols = slice(c0, c0 + qcols)
        m_prev = m_scr[:, cols]
        m_new = jnp.maximum(m_prev, jnp.max(s, axis=0, keepdims=True))
        alpha = jnp.exp2(m_prev - m_new)
        p = jnp.exp2(s - m_new).astype(BF16)
        acc_scr[:, cols] = alpha * acc_scr[:, cols] + jnp.dot(vt_scr[:, k0:k0 + kn], p, preferred_element_type=F32)
        m_scr[:, cols] = m_new

    ahead = min(DIFF_LOOKAHEAD, len(items))
    pending = [scores(*it) for it in items[:ahead]]
    for n, item in enumerate(items):
        if n + ahead < len(items):
            pending.append(scores(*items[n + ahead]))
        accumulate(pending.pop(0), *item)

    lp = lam_ref[...]
    lam = (jnp.exp(jnp.sum(lp[0:1] * lp[1:2], axis=1, keepdims=True))
           - jnp.exp(jnp.sum(lp[2:3] * lp[3:4], axis=1, keepdims=True)) + lam_init)
    acc = acc_scr[...]
    o = acc[:hd, :tq] / acc[hd:hd + 1, :tq] - lam * (acc[:hd, tq:] / acc[hd:hd + 1, tq:])
    ms = jnp.mean(o * o, axis=0, keepdims=True)
    r = o * (lax.rsqrt(ms + NORM_EPS) * (1.0 - lam_init))
    o_ref[...] = (r.T * nw_ref[...]).astype(BF16)


def _diff_call(P, lam_p, nw, *, B, N, M, with_lat, lam_init):
    tk = _pick((N,), 512, 256)
    hw = 2 * DIFF_DH
    cq, ck, cv = COL_DQ // hw, COL_DK // hw, COL_DV // hw
    ctx_blk = B * N // M
    small = lambda: pl.BlockSpec((4, DIFF_DH), lambda b, h, qi: (0, 0))
    lat = lambda col: pl.BlockSpec((N, hw), lambda b, h, qi: (b, col + h))
    ctx = lambda col: pl.BlockSpec((M, hw), lambda b, h, qi: (ctx_blk + b, col + h))
    if with_lat:
        tq = _pick((N,), 1024, 512, 256)
        nq, q0, keys = N // tq, 0, N + M
        kv_specs = [lat(ck), ctx(ck), lat(cv), ctx(cv)]
    else:
        tq = M
        nq, q0, keys = 1, B * N // tq, M
        kv_specs = [ctx(ck), ctx(cv)]
    kern = functools.partial(_diff_kernel, tq=tq, tk=tk, with_lat=with_lat, lam_init=lam_init)
    return pl.pallas_call(
        kern,
        grid=(B, N_DIFF_HEADS, nq),
        in_specs=[
            small(),
            pl.BlockSpec((1, hw), lambda b, h, qi: (0, 0)),
            pl.BlockSpec((tq, hw), lambda b, h, qi: (q0 + b * nq + qi, cq + h)),
        ] + kv_specs,
        out_specs=pl.BlockSpec((tq, hw), lambda b, h, qi: (b * nq + qi, h)),
        out_shape=jax.ShapeDtypeStruct((B * nq * tq, DIFF_WIDTH), BF16),
        scratch_shapes=[pltpu.VMEM((hw + HALO, keys), BF16), pltpu.VMEM((1, 2 * tq), F32),
                        pltpu.VMEM((hw + HALO, 2 * tq), F32)],
        compiler_params=_cparams("parallel", "parallel", "arbitrary"),
        name="diff_attn" if with_lat else "diff_attn_ctx",
    )(lam_p, nw, P, *([P] * len(kv_specs)))


def _win_kernel(sink_ref, q_ref, kp_ref, kc_ref, kn_ref, kx_ref, o_ref, *, nb, N, M):
    i = pl.program_id(1)
    wb, kw = WIN_BLOCK, WIN_KV_WIDTH
    q = q_ref[...]
    kcat = jnp.concatenate([kp_ref[:, :kw], kc_ref[:, :kw], kn_ref[:, :kw], kx_ref[:, :kw]], axis=0)
    vcat = jnp.concatenate([kp_ref[:, kw:], kc_ref[:, kw:], kn_ref[:, kw:], kx_ref[:, kw:]], axis=0)
    nk = 3 * wb + M
    rows = WIN_G * wb
    row = lax.broadcasted_iota(jnp.int32, (rows, nk), 0) % wb
    col = lax.broadcasted_iota(jnp.int32, (rows, nk), 1)
    kpos = (i - 1) * wb + col
    n_band = jnp.where(i < nb, N, 0)
    band_ok = (jnp.abs(col - wb - row) <= WINDOW) & (kpos >= 0) & (kpos < n_band)
    valid = band_ok | (col >= 3 * wb)
    lane = lax.broadcasted_iota(jnp.int32, (wb, kw), 1) // WIN_DH
    lane_o = lax.broadcasted_iota(jnp.int32, (rows, kw), 1) // WIN_DH
    grow = lax.broadcasted_iota(jnp.int32, (rows, 1), 0) // wb
    zero = jnp.zeros((wb, kw), BF16)
    acc = jnp.zeros((rows, kw), F32)

    def scores(kv):
        qs = jnp.concatenate([jnp.where(lane == kv, q[:, g * kw:(g + 1) * kw], zero) for g in range(WIN_G)], axis=0)
        return lax.dot_general(qs, kcat, (((1,), (1,)), ((), ())), preferred_element_type=F32)

    pending = [scores(kv) for kv in range(WIN_LOOKAHEAD)]
    for kv in range(WIN_KV_HEADS):
        if kv + WIN_LOOKAHEAD < WIN_KV_HEADS:
            pending.append(scores(kv + WIN_LOOKAHEAD))
        s = jnp.where(valid, pending.pop(0), NEG_BIG)
        sink = jnp.zeros((rows, 1), F32)
        for g in range(WIN_G):
            sink = jnp.where(grow == g, sink_ref[kv * WIN_G + g] * LOG2E, sink)
        m = jnp.maximum(jnp.max(s, axis=1, keepdims=True), sink)
        p = jnp.exp2(s - m)
        l = jnp.sum(p, axis=1, keepdims=True) + jnp.exp2(sink - m)
        o = jnp.dot(p.astype(BF16), vcat, preferred_element_type=F32) / l
        acc = jnp.where(lane_o == kv, o, acc)
    for g in range(WIN_G):
        o_ref[:, g * kw:(g + 1) * kw] = acc[g * wb:(g + 1) * wb].astype(BF16)


def _win_call(P, sink, *, B, N, M, need_ctx):
    wb = WIN_BLOCK
    nb, nbc = N // wb, (M // wb if need_ctx else 0)
    R = B * N + (B * M if need_ctx else 0)
    cq, ckv = COL_WQ // PW, COL_WKV // (2 * WIN_KV_WIDTH)

    def qrow(b, i):
        return jnp.where(i < nb, b * nb + i, B * nb + b * nbc + (i - nb))

    def band(b, i, d):
        return b * nb + jnp.clip(i + d, 0, nb - 1)

    kvw = 2 * WIN_KV_WIDTH
    kern = functools.partial(_win_kernel, nb=nb, N=N, M=M)
    grid_spec = pltpu.PrefetchScalarGridSpec(
        num_scalar_prefetch=1,
        grid=(B, nb + nbc),
        in_specs=[
            pl.BlockSpec((wb, PW), lambda b, i, s: (qrow(b, i), cq)),
            pl.BlockSpec((wb, kvw), lambda b, i, s: (band(b, i, -1), ckv)),
            pl.BlockSpec((wb, kvw), lambda b, i, s: (band(b, i, 0), ckv)),
            pl.BlockSpec((wb, kvw), lambda b, i, s: (band(b, i, 1), ckv)),
            pl.BlockSpec((M, kvw), lambda b, i, s: (B * N // M + b, ckv)),
        ],
        out_specs=pl.BlockSpec((wb, PW), lambda b, i, s: (qrow(b, i), 0)),
    )
    return pl.pallas_call(
        kern,
        grid_spec=grid_spec,
        out_shape=jax.ShapeDtypeStruct((R, WIN_WIDTH), BF16),
        compiler_params=_cparams("parallel", "arbitrary"),
        name="win_attn",
    )(sink, P, P, P, P, P)


def _seq_edges(i, R_lat, N, M):
    r0 = i * CONV_ROWS
    lat = r0 < R_lat
    first = jnp.where(lat, r0 % N == 0, (r0 - R_lat) % M == 0)
    last = jnp.where(lat, (r0 + CONV_ROWS) % N == 0, (r0 + CONV_ROWS - R_lat) % M == 0)
    return first, last


def _conv_taps(buf, main_ref, prev_ref, next_ref, w_ref, b_ref, first, last, k):
    buf[0:HALO] = jnp.where(first, 0.0, prev_ref[...].astype(F32))
    buf[HALO:HALO + CONV_ROWS] = main_ref[...].astype(F32)
    buf[HALO + CONV_ROWS:] = jnp.where(last, 0.0, next_ref[...].astype(F32))
    out = b_ref[...]
    for j in range(k):
        out = out + buf[pl.ds(HALO - k // 2 + j, CONV_ROWS), :] * w_ref[j:j + 1, :]
    return out


def _ssdconv_kernel(m_ref, p_ref, n_ref, w_ref, b_ref, o_ref, buf, *, R_lat, N, M):
    first, last = _seq_edges(pl.program_id(0), R_lat, N, M)
    o_ref[...] = _silu(_conv_taps(buf, m_ref, p_ref, n_ref, w_ref, b_ref, first, last, SSD_CONV_W)).astype(BF16)


def _ssdconv_call(P, w, b, *, B, N, M):
    T = P.shape[0]
    cr, W = CONV_ROWS, SSD_XBC
    hb = cr // HALO
    cb = COL_XBC // W
    nh = T // HALO
    kern = functools.partial(_ssdconv_kernel, R_lat=B * N, N=N, M=M)
    return pl.pallas_call(
        kern,
        grid=(T // cr,),
        in_specs=[
            pl.BlockSpec((cr, W), lambda i: (i, cb)),
            pl.BlockSpec((HALO, W), lambda i: (jnp.maximum(i * hb - 1, 0), cb)),
            pl.BlockSpec((HALO, W), lambda i: (jnp.minimum((i + 1) * hb, nh - 1), cb)),
            pl.BlockSpec((SSD_CONV_W, W), lambda i: (0, 0)),
            pl.BlockSpec((1, W), lambda i: (0, 0)),
        ],
        out_specs=pl.BlockSpec((cr, W), lambda i: (i, 0)),
        out_shape=jax.ShapeDtypeStruct((T, W), BF16),
        scratch_shapes=[pltpu.VMEM((cr + 2 * HALO, W), F32)],
        compiler_params=_cparams("parallel"),
        name="ssd_conv",
    )(P, P, P, w, b)


def _cumsum_rows(a):
    n = a.shape[0]
    tri = (lax.broadcasted_iota(jnp.int32, (n, n), 0) >= lax.broadcasted_iota(jnp.int32, (n, n), 1)).astype(BF16)
    hi = a.astype(BF16)
    r1 = a - hi.astype(F32)
    mid = r1.astype(BF16)
    lo = (r1 - mid.astype(F32)).astype(BF16)
    dot = lambda v: jnp.dot(tri, v, preferred_element_type=F32)
    return dot(hi) + dot(mid) + dot(lo)


def _ssd_prep(dt_ref, bias_ref, alog_ref, e_ref, *, reverse):
    ch = SSD_CHUNK
    x = dt_ref[...].astype(F32) + bias_ref[...]
    dt = jnp.maximum(x, 0.0) + jnp.log(1.0 + jnp.exp(-jnp.abs(x)))
    ad = dt * (-jnp.exp(alog_ref[...]))
    acum = _cumsum_rows(ad)
    total = acum[ch - 1:ch, :]
    if reverse:
        cvec = acum - ad
        e_off = jnp.exp(total - cvec)
        e_ws = jnp.exp(cvec)
    else:
        cvec = acum
        e_off = jnp.exp(acum)
        e_ws = jnp.exp(total - acum)
    tot_rows = jnp.broadcast_to(jnp.exp(total), (HALO, LANES))
    scales = jnp.dot(jnp.concatenate([dt * e_ws, e_off, tot_rows], axis=0).astype(BF16), e_ref[...],
                     preferred_element_type=F32)
    return cvec, cvec.T, dt.T, scales


def _ssd_group(g, prep, u_ref, s_ref, y_ref, *, reverse, lane0):
    cvec, cT, dtT, scales = prep
    ch, hp = SSD_CHUNK, SSD_HEAD_DIM
    hpg = SSD_HEADS // SSD_GROUPS
    gw = hpg * hp
    li = lax.broadcasted_iota(jnp.int32, (ch, ch), 0)
    si = lax.broadcasted_iota(jnp.int32, (ch, ch), 1)
    tri = (si >= li) if reverse else (li >= si)
    hsel = lax.broadcasted_iota(jnp.int32, (ch, gw), 1) // hp
    lanes = slice(g * gw, (g + 1) * gw)
    bg = u_ref[:, SSD_INNER + g * SSD_STATE:SSD_INNER + (g + 1) * SSD_STATE]
    cg = u_ref[:, SSD_INNER + (SSD_GROUPS + g) * SSD_STATE:SSD_INNER + (SSD_GROUPS + g + 1) * SSD_STATE]
    xs = u_ref[:, lanes]
    cb = lax.dot_general(cg, bg, (((1,), (1,)), ((), ())), preferred_element_type=F32)
    sg = s_ref[:, lanes]
    yoff = jnp.dot(cg, sg.astype(BF16), preferred_element_type=F32) * scales[ch:2 * ch, lanes]
    xw = (xs.astype(F32) * scales[0:ch, lanes]).astype(BF16)
    s_ref[:, lanes] = (sg * scales[2 * ch:2 * ch + 1, lanes]
                       + lax.dot_general(bg, xw, (((0,), (0,)), ((), ())), preferred_element_type=F32))
    ydiag = jnp.zeros((ch, gw), F32)
    for r in range(hpg):
        j = lane0 + g * hpg + r
        colv = jnp.broadcast_to(cvec[:, j:j + 1], (ch, ch))
        rowv = jnp.broadcast_to(cT[j:j + 1, :], (ch, ch))
        d = (rowv - colv) if reverse else (colv - rowv)
        lm = jnp.exp(jnp.where(tri, d, NEG_BIG)) * dtT[j:j + 1, :]
        yh = jnp.dot((cb * lm).astype(BF16), xs, preferred_element_type=F32)
        ydiag = jnp.where(hsel == r, yh, ydiag)
    y_ref[:, lanes] = (ydiag + yoff).astype(BF16)


def _ssd_kernel(uf_ref, dtf_ref, ub_ref, dtb_ref, bias_ref, alog_ref, e_ref, yf_ref, yb_ref, sf, sb):
    @pl.when(pl.program_id(1) == 0)
    def _():
        sf[...] = jnp.zeros_like(sf)
        sb[...] = jnp.zeros_like(sb)

    pf = _ssd_prep(dtf_ref, bias_ref, alog_ref, e_ref.at[0], reverse=False)
    pb = _ssd_prep(dtb_ref, bias_ref, alog_ref, e_ref.at[1], reverse=True)
    for g in range(SSD_GROUPS):
        _ssd_group(g, pf, uf_ref, sf, yf_ref, reverse=False, lane0=0)
        _ssd_group(g, pb, ub_ref, sb, yb_ref, reverse=True, lane0=SSD_HEADS)


def _ssd_call(U, P, bias, alog, spread, *, B, N, M):
    T = U.shape[0]
    ch = SSD_CHUNK
    ncl, ncc = N // ch, M // ch
    cdt = COL_DT // LANES

    def fwd(b, t):
        return jnp.where(t < ncc, B * ncl + b * ncc + t, b * ncl + (t - ncc))

    def bwd(b, t):
        return jnp.where(t < ncc, B * ncl + b * ncc + (ncc - 1 - t), b * ncl + (ncl - 1 - (t - ncc)))

    return pl.pallas_call(
        _ssd_kernel,
        grid=(B, ncl + ncc),
        in_specs=[
            pl.BlockSpec((ch, SSD_XBC), lambda b, t: (fwd(b, t), 0)),
            pl.BlockSpec((ch, LANES), lambda b, t: (fwd(b, t), cdt)),
            pl.BlockSpec((ch, SSD_XBC), lambda b, t: (bwd(b, t), 0)),
            pl.BlockSpec((ch, LANES), lambda b, t: (bwd(b, t), cdt)),
            pl.BlockSpec((1, LANES), lambda b, t: (0, 0)),
            pl.BlockSpec((1, LANES), lambda b, t: (0, 0)),
            pl.BlockSpec((2, LANES, SSD_INNER), lambda b, t: (0, 0, 0)),
        ],
        out_specs=[
            pl.BlockSpec((ch, SSD_INNER), lambda b, t: (fwd(b, t), 0)),
            pl.BlockSpec((ch, SSD_INNER), lambda b, t: (bwd(b, t), 0)),
        ],
        out_shape=[jax.ShapeDtypeStruct((T, SSD_INNER), BF16)] * 2,
        scratch_shapes=[pltpu.VMEM((SSD_STATE, SSD_INNER), F32)] * 2,
        compiler_params=_cparams("parallel", "arbitrary"),
        name="ssd_scan",
    )(U, P, U, P, bias, alog, spread)


def _merge_kernel(yd_ref, yf_ref, yb_ref, xs_ref, z_ref, yw_ref, gd_ref, gs_ref, gw_ref,
                  wd_ref, ws_ref, ww_ref, dsk_ref, sn_ref, o_ref):
    y = yf_ref[...].astype(F32) + yb_ref[...].astype(F32) + dsk_ref[...] * xs_ref[...].astype(F32)
    y = y * _silu(z_ref[...].astype(F32))
    ms = jnp.mean(y * y, axis=-1, keepdims=True)
    s = (y * lax.rsqrt(ms + NORM_EPS) * sn_ref[...]).astype(BF16)
    dot = lambda a, w: jnp.dot(a, w[...], preferred_element_type=F32)
    sig = lambda r: _sigmoid(r[...].astype(F32))
    acc = sig(gd_ref) * dot(yd_ref[...], wd_ref)
    acc = acc + sig(gs_ref) * dot(s, ws_ref)
    acc = acc + sig(gw_ref) * dot(yw_ref[...], ww_ref)
    o_ref[...] = acc.astype(BF16)


def _merge_call(YD, YF, YB, U, P, YW, wd, ws, ww, dsk, sn, *, layer, R, D):
    tm = 256
    W = SSD_INNER
    cg = COL_GL // D
    row = lambda c: pl.BlockSpec((tm, W), lambda i: (i, c))
    gate = lambda c: pl.BlockSpec((tm, D), lambda i: (i, cg + c))
    wspec = pl.BlockSpec((None, W, D), lambda i: (layer, 0, 0))
    vec = pl.BlockSpec((1, W), lambda i: (0, 0))
    return pl.pallas_call(
        _merge_kernel,
        grid=(R // tm,),
        in_specs=[row(0), row(0), row(0), row(0), row(COL_SZ // W), row(0),
                  gate(0), gate(1), gate(2), wspec, wspec, wspec, vec, vec],
        out_specs=pl.BlockSpec((tm, D), lambda i: (i, 0)),
        out_shape=jax.ShapeDtypeStruct((R, D), BF16),
        compiler_params=_cparams("parallel"),
        name="branch_merge",
    )(YD, YF, YB, U, P, YW, P, P, P, wd, ws, ww, dsk, sn)


def _rope_tables(N, rows_ctx):
    hd = DIFF_DH
    pos = jnp.arange(N)
    row = (pos // GRID_W).astype(F32)
    colp = (pos % GRID_W).astype(F32)
    axis_dim = hd // 2
    inv = ROPE_BASE ** (-jnp.arange(0, axis_dim, 2, dtype=F32) / axis_dim)
    ang = jnp.concatenate([row[:, None] * inv, colp[:, None] * inv], axis=-1)
    lane = jnp.arange(LANES)
    cos = jnp.cos(ang)[:, lane % (hd // 2)]
    sin = jnp.sin(ang)[:, lane % (hd // 2)]
    first = (lane % hd) < hd // 2
    lat = jnp.stack([cos, jnp.where(first, -sin, 0.0), jnp.where(first, 0.0, sin)])
    ctx = jnp.stack([jnp.ones((rows_ctx, LANES), F32), jnp.zeros((rows_ctx, LANES), F32),
                     jnp.zeros((rows_ctx, LANES), F32)])
    k_tab = jnp.concatenate([lat, ctx], axis=1)
    q_tab = k_tab * (DIFF_DH ** -0.5 * LOG2E)
    return jnp.stack([q_tab, k_tab])


def _edge_table(B, N, M):
    pos = jnp.concatenate([jnp.tile(jnp.arange(N), B), jnp.tile(jnp.arange(M), B)])
    last = jnp.concatenate([jnp.full((B * N,), N - 1), jnp.full((B * M,), M - 1)])
    t = jnp.stack([pos != 0, pos != last]).astype(F32)
    return jnp.broadcast_to(t[:, :, None], t.shape + (LANES,))


def _layout_w_in(w):
    L, D = w.shape[:2]
    sizes = (DIFF_WIDTH, DIFF_WIDTH, DIFF_WIDTH, SSD_INNER, SSD_XBC, 2 * SSD_HEADS,
             WIN_WIDTH, WIN_KV_WIDTH, WIN_KV_WIDTH, 3 * D)
    parts, s = [], 0
    for z in sizes:
        parts.append(w[..., s:s + z])
        s += z
    dq, dk, dv, sz, sxbc, sdt, wq, wk, wv, gl = parts
    wq = wq.reshape(L, D, WIN_KV_HEADS, WIN_G, WIN_DH).transpose(0, 1, 3, 2, 4).reshape(L, D, WIN_WIDTH)
    pad = jnp.zeros((L, D, PW - 2 * WIN_KV_WIDTH - 2 * SSD_HEADS), w.dtype)
    return jnp.concatenate([dq, dk, wq, wk, wv, sdt, pad, dv, sz, sxbc, gl], axis=-1).astype(BF16)


def kernel(x, c, ctx, c_ctx, w_ada, b_ada, norm_g, w_in, diff_lambda, diff_norm, ssd_conv_w, ssd_conv_b,
           ssd_a_log, ssd_dt_bias, ssd_d, ssd_norm, win_sink, w_br_diff, w_br_ssd, w_br_win, w_out,
           ffn_w_up, ffn_conv_w, ffn_conv_b, ffn_w_down):
    B, N, D = x.shape
    M = ctx.shape[1]
    L = w_ada.shape[0]
    assert D == 2 * PW and P_WIDTH == COL_GL + 3 * D
    assert N % CONV_ROWS == 0 and M % CONV_ROWS == 0 and (B * N) % M == 0 and B + 1 <= 16
    T = B * N + B * M

    X = jnp.concatenate([x.reshape(B * N, D), ctx.reshape(B * M, D)], axis=0)
    cv = jnp.zeros((16, D), F32).at[:B].set(c).at[B].set(c_ctx)
    mod = _mod_call(cv, w_ada, b_ada)

    tm_in = _pick((N, B * M), 1024, 512, 256)
    tab = _rope_tables(N, tm_in)
    kinds = jnp.asarray(COL_KINDS, jnp.int32)
    edge = _edge_table(B, N, M)
    head_of_lane = jnp.arange(SSD_INNER) // SSD_HEAD_DIM
    spread = jnp.stack([(jnp.arange(LANES)[:, None] == head_of_lane[None, :] + d * SSD_HEADS)
                        for d in range(2)]).astype(BF16)
    pad_lanes = LANES - 2 * SSD_HEADS

    W_in = _layout_w_in(w_in)
    W_bd, W_bs, W_out = w_br_diff.astype(BF16), w_br_ssd.astype(BF16), w_out.astype(BF16)
    W_bw = (w_br_win.reshape(L, WIN_KV_HEADS, WIN_G, WIN_DH, D).transpose(0, 2, 1, 3, 4)
            .reshape(L, WIN_WIDTH, D).astype(BF16))
    W_up, W_down = ffn_w_up.astype(BF16), ffn_w_down.astype(BF16)

    for l in range(L):
        need_ctx = l < L - 1
        lam_init = 0.8 - 0.6 * math.exp(-0.3 * l)
        R = T if need_ctx else B * N
        mod3 = mod[l].reshape(16, 1, 6 * D)
        kw0 = dict(B=B, N=N, M=M)
        kw = dict(layer=l, **kw0)

        P = _inproj_call(X, mod3, norm_g[l, 0:1], W_in, tab, kinds, sc_part=1, sh_part=0, **kw)

        dargs = (P, diff_lambda[l], diff_norm[l].reshape(1, -1))
        YD = _diff_call(*dargs, with_lat=True, lam_init=lam_init, **kw0)
        if need_ctx:
            YD = jnp.concatenate([YD, _diff_call(*dargs, with_lat=False, lam_init=lam_init, **kw0)], axis=0)
        YW = _win_call(P, win_sink[l], need_ctx=need_ctx, **kw0)

        U = _ssdconv_call(P, ssd_conv_w[l], ssd_conv_b[l].reshape(1, -1), **kw0)
        bias = jnp.pad(ssd_dt_bias[l].reshape(1, -1), ((0, 0), (0, pad_lanes)))
        alog = jnp.pad(ssd_a_log[l].reshape(1, -1), ((0, 0), (0, pad_lanes)))
        YF, YB = _ssd_call(U, P, bias, alog, spread, **kw0)

        dsk = jnp.repeat(ssd_d[l], SSD_HEAD_DIM).reshape(1, -1)
        Z = _merge_call(YD, YF, YB, U, P, YW, W_bd, W_bs, W_bw, dsk, ssd_norm[l].reshape(1, -1), layer=l, R=R, D=D)
        X1 = _mnr_call(Z, W_out, X, mod3, norm_g[l, 1:2], R=R, gate_part=2, tk=W_out.shape[1], name="out_proj", **kw)

        A = _ffnup_call(X1, mod3, norm_g[l, 2:3], W_up, ffn_conv_w[l], ffn_conv_b[l].reshape(1, -1), edge,
                        R=R, sc_part=4, sh_part=3, **kw)
        X = _mnr_call(A, W_down, X1, mod3, norm_g[l, 3:4], R=R, gate_part=5,
                      tk=_pick((A.shape[1],), 2816, 512, 128), name="ffn_down", **kw)

    return X[:B * N].reshape(B, N, D)
```

```python
import functools
import math

import jax
import jax.numpy as jnp
from jax import lax
from jax.experimental import pallas as pl
from jax.experimental.pallas import tpu as pltpu

F32 = jnp.float32
BF16 = jnp.bfloat16

GRID_W = 64
ROPE_BASE = 10000.0
NORM_EPS = 1e-6
N_DIFF_HEADS = 8
DIFF_DH = 64
DIFF_WIDTH = N_DIFF_HEADS * 2 * DIFF_DH
SSD_INNER = 1024
SSD_HEAD_DIM = 64
SSD_HEADS = 16
SSD_GROUPS = 4
SSD_STATE = 128
SSD_CONV_W = 5
SSD_CHUNK = 128
SSD_XBC = SSD_INNER + 2 * SSD_GROUPS * SSD_STATE
WIN_HEADS = 16
WIN_KV_HEADS = 4
WIN_G = WIN_HEADS // WIN_KV_HEADS
WIN_DH = 64
WINDOW = 128
WIN_BLOCK = 128
WIN_WIDTH = WIN_HEADS * WIN_DH
WIN_KV_WIDTH = WIN_KV_HEADS * WIN_DH
FFN_CONV_W = 3

LOG2E = 1.4426950408889634
NEG_BIG = -1e30

LANES = 128
HALO = 16
QCOLS = 256
DIFF_LOOKAHEAD = 3
WIN_LOOKAHEAD = 2
SSD_STEP_CHUNKS = 2
CONV_ROWS = 256
VMEM_LIMIT = 56 * 1024 * 1024

PW = 1024
COL_DQ = 0 * PW
COL_DK = 1 * PW
COL_WQ = 2 * PW
COL_WKV = 3 * PW
COL_DT = COL_WKV + 2 * WIN_KV_WIDTH
COL_DV = 4 * PW
COL_SZ = 5 * PW
COL_XBC = 6 * PW
COL_GL = 8 * PW
P_WIDTH = 14 * PW
KIND_NONE, KIND_Q, KIND_K, KIND_K256 = 0, 1, 2, 3
COL_KINDS = (KIND_Q, KIND_K, KIND_Q, KIND_K256) + (KIND_NONE,) * 10


def _cparams(*sem):
    return pltpu.CompilerParams(dimension_semantics=sem, vmem_limit_bytes=VMEM_LIMIT)


def _sigmoid(x):
    return 1.0 / (1.0 + jnp.exp(-x))


def _silu(x):
    return x * _sigmoid(x)


def _pick(n, *cands):
    for c in cands:
        if all(v % c == 0 for v in n):
            return c
    raise ValueError(f"no block size among {cands} divides {n}")


def _mod_kernel(cv_ref, w_ref, b_ref, o_ref):
    s = _silu(cv_ref[...]).astype(BF16)
    o_ref[0] = jnp.dot(s, w_ref[0].astype(BF16), preferred_element_type=F32) + b_ref[0]


def _mod_call(cv, w_ada, b_ada):
    L, D, W6 = w_ada.shape
    tn = 1024
    return pl.pallas_call(
        _mod_kernel,
        grid=(L, W6 // tn),
        in_specs=[
            pl.BlockSpec((16, D), lambda l, j: (0, 0)),
            pl.BlockSpec((1, D, tn), lambda l, j: (l, 0, j)),
            pl.BlockSpec((1, 1, tn), lambda l, j: (l, 0, j)),
        ],
        out_specs=pl.BlockSpec((1, 16, tn), lambda l, j: (l, 0, j)),
        out_shape=jax.ShapeDtypeStruct((L, 16, W6), F32),
        compiler_params=_cparams("parallel", "parallel"),
        name="adaln_mod",
    )(cv, w_ada, b_ada.reshape(L, 1, W6))


def _norm_mod(x, g, sc, sh):
    ms = jnp.mean(x * x, axis=-1, keepdims=True)
    y = x * lax.rsqrt(ms + NORM_EPS) * g
    return y * (1.0 + sc) + sh


def _rope(y, cos, sa, sb):
    w = y.shape[1]
    rep = w // LANES
    if rep > 1:
        cos, sa, sb = (jnp.tile(t, (1, rep)) for t in (cos, sa, sb))
    up = pltpu.roll(y, w - DIFF_DH // 2, axis=1)
    dn = pltpu.roll(y, DIFF_DH // 2, axis=1)
    return y * cos + up * sa + dn * sb


def _inproj_kernel(kind_ref, xa_ref, xb_ref, sc_ref, sh_ref, g_ref, w_ref, tab_ref, o_ref, h_scr, *, n_first):
    j = pl.program_id(1)

    @pl.when(j == 0)
    def _():
        x = jnp.where(pl.program_id(0) < n_first, xa_ref[...], xb_ref[...])
        h_scr[...] = _norm_mod(x, g_ref[...], sc_ref[0], sh_ref[0]).astype(BF16)

    dot = lambda: jnp.dot(h_scr[...], w_ref[...], preferred_element_type=F32)
    kind = kind_ref[j]

    @pl.when(kind == KIND_NONE)
    def _():
        o_ref[...] = dot().astype(BF16)

    @pl.when(jnp.logical_or(kind == KIND_Q, kind == KIND_K))
    def _():
        o_ref[...] = _rope(dot(), tab_ref[0, 0], tab_ref[0, 1], tab_ref[0, 2]).astype(BF16)

    @pl.when(kind == KIND_K256)
    def _():
        kw = WIN_KV_WIDTH
        y = dot()
        o_ref[:, :kw] = _rope(y[:, :kw], tab_ref[0, 0], tab_ref[0, 1], tab_ref[0, 2]).astype(BF16)
        o_ref[:, kw:] = y[:, kw:].astype(BF16)


def _inproj_call(Xa, Xb, mod3, g, W, tab, kinds, *, layer, B, N, M, sc_part, sh_part):
    D = Xa.shape[1]
    T = B * N + B * M
    tm = _pick((N, B * M), 1024, 512, 256)
    n_first = min(Xa.shape[0], T) // tm
    nlat = N // tm
    grp = lambda i: jnp.minimum((i * tm) // N, B)
    rblk = lambda i: jnp.where(i < B * nlat, i % nlat, nlat)
    ksel = lambda k: jnp.where(jnp.logical_or(k == KIND_K, k == KIND_K256), 1, 0)
    grid_spec = pltpu.PrefetchScalarGridSpec(
        num_scalar_prefetch=1,
        grid=(T // tm, P_WIDTH // PW),
        in_specs=[
            pl.BlockSpec((tm, D), lambda i, j, kr: (jnp.minimum(i, n_first - 1), 0)),
            pl.BlockSpec((tm, D), lambda i, j, kr: (jnp.maximum(i - n_first, 0), 0), pipeline_mode=pl.Buffered(1)),
            pl.BlockSpec((1, 1, D), lambda i, j, kr: (grp(i), 0, sc_part)),
            pl.BlockSpec((1, 1, D), lambda i, j, kr: (grp(i), 0, sh_part)),
            pl.BlockSpec((1, D), lambda i, j, kr: (0, 0)),
            pl.BlockSpec((None, D, PW), lambda i, j, kr: (layer, 0, j)),
            pl.BlockSpec((1, 3, tm, LANES), lambda i, j, kr: (ksel(kr[j]), 0, rblk(i), 0)),
        ],
        out_specs=pl.BlockSpec((tm, PW), lambda i, j, kr: (i, j)),
        scratch_shapes=[pltpu.VMEM((tm, D), BF16)],
    )
    return pl.pallas_call(
        functools.partial(_inproj_kernel, n_first=n_first),
        grid_spec=grid_spec,
        out_shape=jax.ShapeDtypeStruct((T, P_WIDTH), BF16),
        compiler_params=_cparams("parallel", "arbitrary"),
        name="in_proj",
    )(kinds, Xa, Xb, mod3, mod3, g, W, tab)


def _ffnup_kernel(x_ref, xp_ref, xn_ref, sc_ref, sh_ref, g_ref, wa_ref, wg_ref, cwa_ref, cwg_ref, ba_ref, bg_ref,
                  edge_ref, o_ref, h_scr, bufa, bufg):
    tm = x_ref.shape[0]

    @pl.when(pl.program_id(1) == 0)
    def _():
        nm = lambda r: _norm_mod(r[...], g_ref[...], sc_ref[0], sh_ref[0]).astype(BF16)
        h_scr[0:HALO] = nm(xp_ref)
        h_scr[HALO:HALO + tm] = nm(x_ref)
        h_scr[HALO + tm:] = nm(xn_ref)

    rep = o_ref.shape[1] // LANES
    prev_ok = jnp.tile(edge_ref[0], (1, rep))
    next_ok = jnp.tile(edge_ref[1], (1, rep))

    def conv(buf, w_ref, cw_ref, b_ref):
        buf[...] = jnp.dot(h_scr[...], w_ref[...], preferred_element_type=F32)
        taps = [buf[pl.ds(HALO - 1 + j, tm), :] for j in range(FFN_CONV_W)]
        out = b_ref[...] + (taps[0] * prev_ok) * cw_ref[0:1, :]
        out = out + taps[1] * cw_ref[1:2, :]
        return out + (taps[2] * next_ok) * cw_ref[2:3, :]

    a = conv(bufa, wa_ref, cwa_ref, ba_ref)
    g = conv(bufg, wg_ref, cwg_ref, bg_ref)
    o_ref[...] = (_silu(a) * g).astype(BF16)


def _ffnup_call(X, mod3, g, W, cw, cb, edge, *, layer, R, B, N, M, sc_part, sh_part):
    D = X.shape[1]
    F = W.shape[2] // 2
    tm = _pick((N, B * M), 1024, 512, 256)
    tn = _pick((F,), 512, 128)
    nc = F // tn
    hb = tm // HALO
    nh = R // HALO
    grp = lambda i: jnp.minimum((i * tm) // N, B)
    col = lambda shape, off: pl.BlockSpec(shape, lambda i, j: (0, j + off))
    return pl.pallas_call(
        _ffnup_kernel,
        grid=(R // tm, nc),
        in_specs=[
            pl.BlockSpec((tm, D), lambda i, j: (i, 0)),
            pl.BlockSpec((HALO, D), lambda i, j: (jnp.maximum(i * hb - 1, 0), 0)),
            pl.BlockSpec((HALO, D), lambda i, j: (jnp.minimum((i + 1) * hb, nh - 1), 0)),
            pl.BlockSpec((1, 1, D), lambda i, j: (grp(i), 0, sc_part)),
            pl.BlockSpec((1, 1, D), lambda i, j: (grp(i), 0, sh_part)),
            pl.BlockSpec((1, D), lambda i, j: (0, 0)),
            pl.BlockSpec((None, D, tn), lambda i, j: (layer, 0, j)),
            pl.BlockSpec((None, D, tn), lambda i, j: (layer, 0, j + nc)),
            col((FFN_CONV_W, tn), 0), col((FFN_CONV_W, tn), nc),
            col((1, tn), 0), col((1, tn), nc),
            pl.BlockSpec((2, tm, LANES), lambda i, j: (0, i, 0)),
        ],
        out_specs=pl.BlockSpec((tm, tn), lambda i, j: (i, j)),
        out_shape=jax.ShapeDtypeStruct((R, F), BF16),
        scratch_shapes=[pltpu.VMEM((tm + 2 * HALO, D), BF16), pltpu.VMEM((tm + 2 * HALO, tn), F32),
                        pltpu.VMEM((tm + 2 * HALO, tn), F32)],
        compiler_params=_cparams("parallel", "arbitrary"),
        name="ffn_up_conv_gate",
    )(X, X, X, mod3, mod3, g, W, W, cw, cw, cb, cb, edge)


def _mnr_kernel(a_ref, w_ref, x_ref, gate_ref, g_ref, o_ref, acc):
    k = pl.program_id(1)

    @pl.when(k == 0)
    def _():
        acc[...] = jnp.zeros_like(acc)

    acc[...] += jnp.dot(a_ref[...], w_ref[...], preferred_element_type=F32)

    @pl.when(k == pl.num_programs(1) - 1)
    def _():
        y = acc[...]
        ms = jnp.mean(y * y, axis=-1, keepdims=True)
        r = y * lax.rsqrt(ms + NORM_EPS) * g_ref[...]
        o_ref[...] = x_ref[...] + gate_ref[0] * r


def _mnr_call(A, W, X, mod3, g, *, layer, R, B, N, M, gate_part, tk, name):
    K, D = W.shape[1:]
    tm = _pick((N, B * M), 512, 256)
    grp = lambda i: jnp.minimum((i * tm) // N, B)
    return pl.pallas_call(
        _mnr_kernel,
        grid=(R // tm, K // tk),
        in_specs=[
            pl.BlockSpec((tm, tk), lambda i, k: (i, k)),
            pl.BlockSpec((None, tk, D), lambda i, k: (layer, k, 0)),
            pl.BlockSpec((tm, D), lambda i, k: (i, 0)),
            pl.BlockSpec((1, 1, D), lambda i, k: (grp(i), 0, gate_part)),
            pl.BlockSpec((1, D), lambda i, k: (0, 0)),
        ],
        out_specs=pl.BlockSpec((tm, D), lambda i, k: (i, 0)),
        out_shape=jax.ShapeDtypeStruct((R, D), F32),
        scratch_shapes=[pltpu.VMEM((tm, D), F32)],
        compiler_params=_cparams("parallel", "arbitrary"),
        name=name,
    )(A, W, X, mod3, g)


def _diff_kernel(lam_ref, nw_ref, q_ref, *refs, tq, tk, with_lat, lam_init):
    if with_lat:
        kl_ref, kc_ref, vl_ref, vc_ref, o_ref, vt_scr, m_scr, acc_scr = refs
        N = kl_ref.shape[0]
    else:
        kc_ref, vc_ref, o_ref, vt_scr, m_scr, acc_scr = refs
        N = 0
    M = kc_ref.shape[0]
    hd = 2 * DIFF_DH
    n_lat_k = N // tk

    @pl.when(pl.program_id(2) == 0)
    def _():
        for c in range(n_lat_k):
            vt_scr[0:hd, c * tk:(c + 1) * tk] = vl_ref[c * tk:(c + 1) * tk, :].astype(F32).T.astype(BF16)
        vt_scr[0:hd, N:N + M] = vc_ref[...].astype(F32).T.astype(BF16)
        vt_scr[hd:, :] = jnp.ones((vt_scr.shape[0] - hd, N + M), BF16)

    q = q_ref[...]
    lane = lax.broadcasted_iota(jnp.int32, q.shape, 1)
    zero = jnp.zeros_like(q)
    qs = jnp.concatenate([jnp.where(lane < DIFF_DH, q, zero), jnp.where(lane >= DIFF_DH, q, zero)], axis=0)

    m_scr[...] = jnp.full_like(m_scr, NEG_BIG)
    acc_scr[...] = jnp.zeros_like(acc_scr)

    qcols = min(QCOLS, 2 * tq)
    chunks = [(c * tk, tk) for c in range(n_lat_k)] + [(N, M)]
    items = [(k0, kn, t * qcols) for (k0, kn) in chunks for t in range(2 * tq // qcols)]

    def keys(k0, kn):
        return kl_ref[k0:k0 + kn, :] if k0 < N else kc_ref[...]

    def scores(k0, kn, c0):
        return lax.dot_general(keys(k0, kn), qs[c0:c0 + qcols], (((1,), (1,)), ((), ())),
                               preferred_element_type=F32)

    def accumulate(s, k0, kn, c0):
        cols = slice(c0, c0 + qcols)
        m_prev = m_scr[:, cols]
        m_new = jnp.maximum(m_prev, jnp.max(s, axis=0, keepdims=True))
        alpha = jnp.exp2(m_prev - m_new)
        p = jnp.exp2(s - m_new).astype(BF16)
        acc_scr[:, cols] = alpha * acc_scr[:, cols] + jnp.dot(vt_scr[:, k0:k0 + kn], p, preferred_element_type=F32)
        m_scr[:, cols] = m_new

    ahead = min(DIFF_LOOKAHEAD, len(items))
    pending = [scores(*it) for it in items[:ahead]]
    for n, item in enumerate(items):
        if n + ahead < len(items):
            pending.append(scores(*items[n + ahead]))
        accumulate(pending.pop(0), *item)

    lp = lam_ref[...]
    lam = (jnp.exp(jnp.sum(lp[0:1] * lp[1:2], axis=1, keepdims=True))
           - jnp.exp(jnp.sum(lp[2:3] * lp[3:4], axis=1, keepdims=True)) + lam_init)
    acc = acc_scr[...]
    o = acc[:hd, :tq] / acc[hd:hd + 1, :tq] - lam * (acc[:hd, tq:] / acc[hd:hd + 1, tq:])
    ms = jnp.mean(o * o, axis=0, keepdims=True)
    r = o * (lax.rsqrt(ms + NORM_EPS) * (1.0 - lam_init))
    o_ref[...] = (r.T * nw_ref[...]).astype(BF16)


def _diff_call(P, lam_p, nw, *, B, N, M, with_lat, lam_init):
    tk = _pick((N,), 512, 256)
    hw = 2 * DIFF_DH
    cq, ck, cv = COL_DQ // hw, COL_DK // hw, COL_DV // hw
    ctx_blk = B * N // M
    small = lambda: pl.BlockSpec((4, DIFF_DH), lambda b, h, qi: (0, 0))
    lat = lambda col: pl.BlockSpec((N, hw), lambda b, h, qi: (b, col + h))
    ctx = lambda col: pl.BlockSpec((M, hw), lambda b, h, qi: (ctx_blk + b, col + h))
    if with_lat:
        tq = _pick((N,), 1024, 512, 256)
        nq, q0, keys = N // tq, 0, N + M
        kv_specs = [lat(ck), ctx(ck), lat(cv), ctx(cv)]
    else:
        tq = M
        nq, q0, keys = 1, B * N // tq, M
        kv_specs = [ctx(ck), ctx(cv)]
    kern = functools.partial(_diff_kernel, tq=tq, tk=tk, with_lat=with_lat, lam_init=lam_init)
    return pl.pallas_call(
        kern,
        grid=(B, N_DIFF_HEADS, nq),
        in_specs=[
            small(),
            pl.BlockSpec((1, hw), lambda b, h, qi: (0, 0)),
            pl.BlockSpec((tq, hw), lambda b, h, qi: (q0 + b * nq + qi, cq + h)),
        ] + kv_specs,
        out_specs=pl.BlockSpec((tq, hw), lambda b, h, qi: (b * nq + qi, h)),
        out_shape=jax.ShapeDtypeStruct((B * nq * tq, DIFF_WIDTH), BF16),
        scratch_shapes=[pltpu.VMEM((hw + HALO, keys), BF16), pltpu.VMEM((1, 2 * tq), F32),
                        pltpu.VMEM((hw + HALO, 2 * tq), F32)],
        compiler_params=_cparams("parallel", "parallel", "arbitrary"),
        name="diff_attn" if with_lat else "diff_attn_ctx",
    )(lam_p, nw, P, *([P] * len(kv_specs)))


def _win_kernel(sink_ref, q_ref, kp_ref, kc_ref, kn_ref, kx_ref, o_ref, *, nb, N, M):
    i = pl.program_id(1)
    wb, kw = WIN_BLOCK, WIN_KV_WIDTH
    q = q_ref[...]
    kcat = jnp.concatenate([kp_ref[:, :kw], kc_ref[:, :kw], kn_ref[:, :kw], kx_ref[:, :kw]], axis=0)
    vcat = jnp.concatenate([kp_ref[:, kw:], kc_ref[:, kw:], kn_ref[:, kw:], kx_ref[:, kw:]], axis=0)
    nk = 3 * wb + M
    rows = WIN_G * wb
    row = lax.broadcasted_iota(jnp.int32, (rows, nk), 0) % wb
    col = lax.broadcasted_iota(jnp.int32, (rows, nk), 1)
    kpos = (i - 1) * wb + col
    n_band = jnp.where(i < nb, N, 0)
    band_ok = (jnp.abs(col - wb - row) <= WINDOW) & (kpos >= 0) & (kpos < n_band)
    valid = band_ok | (col >= 3 * wb)
    lane = lax.broadcasted_iota(jnp.int32, (wb, kw), 1) // WIN_DH
    lane_o = lax.broadcasted_iota(jnp.int32, (rows, kw), 1) // WIN_DH
    grow = lax.broadcasted_iota(jnp.int32, (rows, 1), 0) // wb
    zero = jnp.zeros((wb, kw), BF16)
    acc = jnp.zeros((rows, kw), F32)

    def scores(kv):
        qs = jnp.concatenate([jnp.where(lane == kv, q[:, g * kw:(g + 1) * kw], zero) for g in range(WIN_G)], axis=0)
        return lax.dot_general(qs, kcat, (((1,), (1,)), ((), ())), preferred_element_type=F32)

    pending = [scores(kv) for kv in range(WIN_LOOKAHEAD)]
    for kv in range(WIN_KV_HEADS):
        if kv + WIN_LOOKAHEAD < WIN_KV_HEADS:
            pending.append(scores(kv + WIN_LOOKAHEAD))
        s = jnp.where(valid, pending.pop(0), NEG_BIG)
        sink = jnp.zeros((rows, 1), F32)
        for g in range(WIN_G):
            sink = jnp.where(grow == g, sink_ref[kv * WIN_G + g] * LOG2E, sink)
        m = jnp.maximum(jnp.max(s, axis=1, keepdims=True), sink)
        p = jnp.exp2(s - m)
        l = jnp.sum(p, axis=1, keepdims=True) + jnp.exp2(sink - m)
        o = jnp.dot(p.astype(BF16), vcat, preferred_element_type=F32) / l
        acc = jnp.where(lane_o == kv, o, acc)
    for g in range(WIN_G):
        o_ref[:, g * kw:(g + 1) * kw] = acc[g * wb:(g + 1) * wb].astype(BF16)


def _win_call(P, sink, *, B, N, M, need_ctx):
    wb = WIN_BLOCK
    nb, nbc = N // wb, (M // wb if need_ctx else 0)
    R = B * N + (B * M if need_ctx else 0)
    cq, ckv = COL_WQ // PW, COL_WKV // (2 * WIN_KV_WIDTH)

    def qrow(b, i):
        return jnp.where(i < nb, b * nb + i, B * nb + b * nbc + (i - nb))

    def band(b, i, d):
        return b * nb + jnp.clip(i + d, 0, nb - 1)

    kvw = 2 * WIN_KV_WIDTH
    kern = functools.partial(_win_kernel, nb=nb, N=N, M=M)
    grid_spec = pltpu.PrefetchScalarGridSpec(
        num_scalar_prefetch=1,
        grid=(B, nb + nbc),
        in_specs=[
            pl.BlockSpec((wb, PW), lambda b, i, s: (qrow(b, i), cq)),
            pl.BlockSpec((wb, kvw), lambda b, i, s: (band(b, i, -1), ckv)),
            pl.BlockSpec((wb, kvw), lambda b, i, s: (band(b, i, 0), ckv)),
            pl.BlockSpec((wb, kvw), lambda b, i, s: (band(b, i, 1), ckv)),
            pl.BlockSpec((M, kvw), lambda b, i, s: (B * N // M + b, ckv)),
        ],
        out_specs=pl.BlockSpec((wb, PW), lambda b, i, s: (qrow(b, i), 0)),
    )
    return pl.pallas_call(
        kern,
        grid_spec=grid_spec,
        out_shape=jax.ShapeDtypeStruct((R, WIN_WIDTH), BF16),
        compiler_params=_cparams("parallel", "arbitrary"),
        name="win_attn",
    )(sink, P, P, P, P, P)


def _seq_edges(i, R_lat, N, M):
    r0 = i * CONV_ROWS
    lat = r0 < R_lat
    first = jnp.where(lat, r0 % N == 0, (r0 - R_lat) % M == 0)
    last = jnp.where(lat, (r0 + CONV_ROWS) % N == 0, (r0 + CONV_ROWS - R_lat) % M == 0)
    return first, last


def _conv_taps(buf, main_ref, prev_ref, next_ref, w_ref, b_ref, first, last, k):
    buf[0:HALO] = jnp.where(first, 0.0, prev_ref[...].astype(F32))
    buf[HALO:HALO + CONV_ROWS] = main_ref[...].astype(F32)
    buf[HALO + CONV_ROWS:] = jnp.where(last, 0.0, next_ref[...].astype(F32))
    out = b_ref[...]
    for j in range(k):
        out = out + buf[pl.ds(HALO - k // 2 + j, CONV_ROWS), :] * w_ref[j:j + 1, :]
    return out


def _ssdconv_kernel(m_ref, p_ref, n_ref, w_ref, b_ref, o_ref, buf, *, R_lat, N, M):
    first, last = _seq_edges(pl.program_id(0), R_lat, N, M)
    o_ref[...] = _silu(_conv_taps(buf, m_ref, p_ref, n_ref, w_ref, b_ref, first, last, SSD_CONV_W)).astype(BF16)


def _ssdconv_call(P, w, b, *, B, N, M):
    T = P.shape[0]
    cr, W = CONV_ROWS, SSD_XBC
    hb = cr // HALO
    cb = COL_XBC // W
    nh = T // HALO
    kern = functools.partial(_ssdconv_kernel, R_lat=B * N, N=N, M=M)
    return pl.pallas_call(
        kern,
        grid=(T // cr,),
        in_specs=[
            pl.BlockSpec((cr, W), lambda i: (i, cb)),
            pl.BlockSpec((HALO, W), lambda i: (jnp.maximum(i * hb - 1, 0), cb)),
            pl.BlockSpec((HALO, W), lambda i: (jnp.minimum((i + 1) * hb, nh - 1), cb)),
            pl.BlockSpec((SSD_CONV_W, W), lambda i: (0, 0)),
            pl.BlockSpec((1, W), lambda i: (0, 0)),
        ],
        out_specs=pl.BlockSpec((cr, W), lambda i: (i, 0)),
        out_shape=jax.ShapeDtypeStruct((T, W), BF16),
        scratch_shapes=[pltpu.VMEM((cr + 2 * HALO, W), F32)],
        compiler_params=_cparams("parallel"),
        name="ssd_conv",
    )(P, P, P, w, b)


def _cumsum_rows(a):
    n = a.shape[0]
    tri = (lax.broadcasted_iota(jnp.int32, (n, n), 0) >= lax.broadcasted_iota(jnp.int32, (n, n), 1)).astype(BF16)
    hi = a.astype(BF16)
    r1 = a - hi.astype(F32)
    mid = r1.astype(BF16)
    lo = (r1 - mid.astype(F32)).astype(BF16)
    dot = lambda v: jnp.dot(tri, v, preferred_element_type=F32)
    return dot(hi) + dot(mid) + dot(lo)


def _ssd_prep(dt_ref, bias_ref, alog_ref, e_ref, rows, *, reverse):
    ch = SSD_CHUNK
    x = dt_ref[rows, :].astype(F32) + bias_ref[...]
    dt = jnp.maximum(x, 0.0) + jnp.log(1.0 + jnp.exp(-jnp.abs(x)))
    ad = dt * (-jnp.exp(alog_ref[...]))
    acum = _cumsum_rows(ad)
    total = acum[ch - 1:ch, :]
    if reverse:
        cvec = acum - ad
        e_off = jnp.exp(total - cvec)
        e_ws = jnp.exp(cvec)
    else:
        cvec = acum
        e_off = jnp.exp(acum)
        e_ws = jnp.exp(total - acum)
    tot_rows = jnp.broadcast_to(jnp.exp(total), (HALO, LANES))
    scales = jnp.dot(jnp.concatenate([dt * e_ws, e_off, tot_rows], axis=0).astype(BF16), e_ref[...],
                     preferred_element_type=F32)
    return cvec, cvec.T, dt.T, scales


def _ssd_group(g, prep, u_ref, s_ref, y_ref, rows, *, reverse, lane0):
    cvec, cT, dtT, scales = prep
    ch, hp = SSD_CHUNK, SSD_HEAD_DIM
    hpg = SSD_HEADS // SSD_GROUPS
    gw = hpg * hp
    li = lax.broadcasted_iota(jnp.int32, (ch, ch), 0)
    si = lax.broadcasted_iota(jnp.int32, (ch, ch), 1)
    tri = (si >= li) if reverse else (li >= si)
    hsel = lax.broadcasted_iota(jnp.int32, (ch, gw), 1) // hp
    lanes = slice(g * gw, (g + 1) * gw)
    bg = u_ref[rows, SSD_INNER + g * SSD_STATE:SSD_INNER + (g + 1) * SSD_STATE]
    cg = u_ref[rows, SSD_INNER + (SSD_GROUPS + g) * SSD_STATE:SSD_INNER + (SSD_GROUPS + g + 1) * SSD_STATE]
    xs = u_ref[rows, lanes]
    cb = lax.dot_general(cg, bg, (((1,), (1,)), ((), ())), preferred_element_type=F32)
    sg = s_ref[:, lanes]
    yoff = jnp.dot(cg, sg.astype(BF16), preferred_element_type=F32) * scales[ch:2 * ch, lanes]
    xw = (xs.astype(F32) * scales[0:ch, lanes]).astype(BF16)
    s_ref[:, lanes] = (sg * scales[2 * ch:2 * ch + 1, lanes]
                       + lax.dot_general(bg, xw, (((0,), (0,)), ((), ())), preferred_element_type=F32))
    ydiag = jnp.zeros((ch, gw), F32)
    for r in range(hpg):
        j = lane0 + g * hpg + r
        colv = jnp.broadcast_to(cvec[:, j:j + 1], (ch, ch))
        rowv = jnp.broadcast_to(cT[j:j + 1, :], (ch, ch))
        d = (rowv - colv) if reverse else (colv - rowv)
        lm = jnp.exp(jnp.where(tri, d, NEG_BIG)) * dtT[j:j + 1, :]
        yh = jnp.dot((cb * lm).astype(BF16), xs, preferred_element_type=F32)
        ydiag = jnp.where(hsel == r, yh, ydiag)
    y_ref[rows, lanes] = (ydiag + yoff).astype(BF16)


def _ssd_kernel(uf_ref, dtf_ref, ub_ref, dtb_ref, bias_ref, alog_ref, e_ref, yf_ref, yb_ref, sf, sb):
    @pl.when(pl.program_id(1) == 0)
    def _():
        sf[...] = jnp.zeros_like(sf)
        sb[...] = jnp.zeros_like(sb)

    ch, n = SSD_CHUNK, SSD_STEP_CHUNKS
    f_rows = [slice(k * ch, (k + 1) * ch) for k in range(n)]
    b_rows = f_rows[::-1]
    pf = [_ssd_prep(dtf_ref, bias_ref, alog_ref, e_ref.at[0], r, reverse=False) for r in f_rows]
    pb = [_ssd_prep(dtb_ref, bias_ref, alog_ref, e_ref.at[1], r, reverse=True) for r in b_rows]
    for k in range(n):
        for g in range(SSD_GROUPS):
            _ssd_group(g, pf[k], uf_ref, sf, yf_ref, f_rows[k], reverse=False, lane0=0)
            _ssd_group(g, pb[k], ub_ref, sb, yb_ref, b_rows[k], reverse=True, lane0=SSD_HEADS)


def _ssd_call(U, P, bias, alog, spread, *, B, N, M):
    T = U.shape[0]
    ch = SSD_CHUNK * SSD_STEP_CHUNKS
    ncl, ncc = N // ch, M // ch
    cdt = COL_DT // LANES

    def fwd(b, t):
        return jnp.where(t < ncc, B * ncl + b * ncc + t, b * ncl + (t - ncc))

    def bwd(b, t):
        return jnp.where(t < ncc, B * ncl + b * ncc + (ncc - 1 - t), b * ncl + (ncl - 1 - (t - ncc)))

    return pl.pallas_call(
        _ssd_kernel,
        grid=(B, ncl + ncc),
        in_specs=[
            pl.BlockSpec((ch, SSD_XBC), lambda b, t: (fwd(b, t), 0)),
            pl.BlockSpec((ch, LANES), lambda b, t: (fwd(b, t), cdt)),
            pl.BlockSpec((ch, SSD_XBC), lambda b, t: (bwd(b, t), 0)),
            pl.BlockSpec((ch, LANES), lambda b, t: (bwd(b, t), cdt)),
            pl.BlockSpec((1, LANES), lambda b, t: (0, 0)),
            pl.BlockSpec((1, LANES), lambda b, t: (0, 0)),
            pl.BlockSpec((2, LANES, SSD_INNER), lambda b, t: (0, 0, 0)),
        ],
        out_specs=[
            pl.BlockSpec((ch, SSD_INNER), lambda b, t: (fwd(b, t), 0)),
            pl.BlockSpec((ch, SSD_INNER), lambda b, t: (bwd(b, t), 0)),
        ],
        out_shape=[jax.ShapeDtypeStruct((T, SSD_INNER), BF16)] * 2,
        scratch_shapes=[pltpu.VMEM((SSD_STATE, SSD_INNER), F32)] * 2,
        compiler_params=_cparams("parallel", "arbitrary"),
        name="ssd_scan",
    )(U, P, U, P, bias, alog, spread)


def _merge_kernel(yd_ref, yf_ref, yb_ref, xs_ref, z_ref, yw_ref, gd_ref, gs_ref, gw_ref,
                  wd_ref, ws_ref, ww_ref, wo_ref, dsk_ref, sn_ref, xa_ref, xb_ref, gate_ref, g_ref, o_ref,
                  *, n_first):
    y = yf_ref[...].astype(F32) + yb_ref[...].astype(F32) + dsk_ref[...] * xs_ref[...].astype(F32)
    y = y * _silu(z_ref[...].astype(F32))
    ms = jnp.mean(y * y, axis=-1, keepdims=True)
    s = (y * lax.rsqrt(ms + NORM_EPS) * sn_ref[...]).astype(BF16)
    dot = lambda a, w: jnp.dot(a, w[...], preferred_element_type=F32)
    sig = lambda r: _sigmoid(r[...].astype(F32))
    acc = sig(gd_ref) * dot(yd_ref[...], wd_ref)
    acc = acc + sig(gs_ref) * dot(s, ws_ref)
    acc = acc + sig(gw_ref) * dot(yw_ref[...], ww_ref)
    t = dot(acc.astype(BF16), wo_ref)
    ms = jnp.mean(t * t, axis=-1, keepdims=True)
    x = jnp.where(pl.program_id(0) < n_first, xa_ref[...], xb_ref[...])
    o_ref[...] = x + gate_ref[0] * (t * lax.rsqrt(ms + NORM_EPS) * g_ref[...])


def _merge_call(YD, YF, YB, U, P, YW, wd, ws, ww, wo, dsk, sn, Xa, Xb, mod3, g, *, layer, R, B, N, M, gate_part):
    D = Xa.shape[1]
    tm = 256
    n_first = min(Xa.shape[0], R) // tm
    W = SSD_INNER
    cg = COL_GL // D
    grp = lambda i: jnp.minimum((i * tm) // N, B)
    row = lambda c: pl.BlockSpec((tm, W), lambda i: (i, c))
    gate = lambda c: pl.BlockSpec((tm, D), lambda i: (i, cg + c))
    once = pl.Buffered(1)
    wspec = pl.BlockSpec((None, W, D), lambda i: (layer, 0, 0), pipeline_mode=once)
    vec = pl.BlockSpec((1, W), lambda i: (0, 0))
    return pl.pallas_call(
        functools.partial(_merge_kernel, n_first=n_first),
        grid=(R // tm,),
        in_specs=[row(0), row(0), row(0), row(0), row(COL_SZ // W), row(0),
                  gate(0), gate(1), gate(2), wspec, wspec, wspec,
                  pl.BlockSpec((None, D, D), lambda i: (layer, 0, 0), pipeline_mode=once), vec, vec,
                  pl.BlockSpec((tm, D), lambda i: (jnp.minimum(i, n_first - 1), 0)),
                  pl.BlockSpec((tm, D), lambda i: (jnp.maximum(i - n_first, 0), 0)),
                  pl.BlockSpec((1, 1, D), lambda i: (grp(i), 0, gate_part)),
                  pl.BlockSpec((1, D), lambda i: (0, 0))],
        out_specs=pl.BlockSpec((tm, D), lambda i: (i, 0)),
        out_shape=jax.ShapeDtypeStruct((R, D), F32),
        compiler_params=_cparams("parallel"),
        name="merge_out_proj",
    )(YD, YF, YB, U, P, YW, P, P, P, wd, ws, ww, wo, dsk, sn, Xa, Xb, mod3, g)


def _rope_tables(N, rows_ctx):
    hd = DIFF_DH
    pos = jnp.arange(N)
    row = (pos // GRID_W).astype(F32)
    colp = (pos % GRID_W).astype(F32)
    axis_dim = hd // 2
    inv = ROPE_BASE ** (-jnp.arange(0, axis_dim, 2, dtype=F32) / axis_dim)
    ang = jnp.concatenate([row[:, None] * inv, colp[:, None] * inv], axis=-1)
    lane = jnp.arange(LANES)
    cos = jnp.cos(ang)[:, lane % (hd // 2)]
    sin = jnp.sin(ang)[:, lane % (hd // 2)]
    first = (lane % hd) < hd // 2
    lat = jnp.stack([cos, jnp.where(first, -sin, 0.0), jnp.where(first, 0.0, sin)])
    ctx = jnp.stack([jnp.ones((rows_ctx, LANES), F32), jnp.zeros((rows_ctx, LANES), F32),
                     jnp.zeros((rows_ctx, LANES), F32)])
    k_tab = jnp.concatenate([lat, ctx], axis=1)
    q_tab = k_tab * (DIFF_DH ** -0.5 * LOG2E)
    return jnp.stack([q_tab, k_tab])


def _edge_table(B, N, M):
    pos = jnp.concatenate([jnp.tile(jnp.arange(N), B), jnp.tile(jnp.arange(M), B)])
    last = jnp.concatenate([jnp.full((B * N,), N - 1), jnp.full((B * M,), M - 1)])
    t = jnp.stack([pos != 0, pos != last]).astype(F32)
    return jnp.broadcast_to(t[:, :, None], t.shape + (LANES,))


def _layout_w_in(w):
    L, D = w.shape[:2]
    sizes = (DIFF_WIDTH, DIFF_WIDTH, DIFF_WIDTH, SSD_INNER, SSD_XBC, 2 * SSD_HEADS,
             WIN_WIDTH, WIN_KV_WIDTH, WIN_KV_WIDTH, 3 * D)
    parts, s = [], 0
    for z in sizes:
        parts.append(w[..., s:s + z])
        s += z
    dq, dk, dv, sz, sxbc, sdt, wq, wk, wv, gl = parts
    wq = wq.reshape(L, D, WIN_KV_HEADS, WIN_G, WIN_DH).transpose(0, 1, 3, 2, 4).reshape(L, D, WIN_WIDTH)
    out = jnp.zeros((L, D, P_WIDTH), BF16)
    for off, piece in ((COL_DQ, dq), (COL_DK, dk), (COL_WQ, wq), (COL_WKV, wk), (COL_WKV + WIN_KV_WIDTH, wv),
                       (COL_DT, sdt), (COL_DV, dv), (COL_SZ, sz), (COL_XBC, sxbc), (COL_GL, gl)):
        out = lax.dynamic_update_slice(out, piece.astype(BF16), (0, 0, off))
    return out


def kernel(x, c, ctx, c_ctx, w_ada, b_ada, norm_g, w_in, diff_lambda, diff_norm, ssd_conv_w, ssd_conv_b,
           ssd_a_log, ssd_dt_bias, ssd_d, ssd_norm, win_sink, w_br_diff, w_br_ssd, w_br_win, w_out,
           ffn_w_up, ffn_conv_w, ffn_conv_b, ffn_w_down):
    B, N, D = x.shape
    M = ctx.shape[1]
    L = w_ada.shape[0]
    assert D == 2 * PW and P_WIDTH == COL_GL + 3 * D
    assert N % CONV_ROWS == 0 and M % CONV_ROWS == 0 and (B * N) % M == 0 and B + 1 <= 16
    assert CONV_ROWS % (SSD_CHUNK * SSD_STEP_CHUNKS) == 0
    T = B * N + B * M

    Xa, Xb = x.reshape(B * N, D), ctx.reshape(B * M, D)
    cv = jnp.zeros((16, D), F32).at[:B].set(c).at[B].set(c_ctx)
    mod = _mod_call(cv, w_ada, b_ada)

    tm_in = _pick((N, B * M), 1024, 512, 256)
    tab = _rope_tables(N, tm_in)
    kinds = jnp.asarray(COL_KINDS, jnp.int32)
    edge = _edge_table(B, N, M)
    head_of_lane = jnp.arange(SSD_INNER) // SSD_HEAD_DIM
    spread = jnp.stack([(jnp.arange(LANES)[:, None] == head_of_lane[None, :] + d * SSD_HEADS)
                        for d in range(2)]).astype(BF16)
    pad_lanes = LANES - 2 * SSD_HEADS

    W_in = _layout_w_in(w_in)
    W_bd, W_bs, W_out = w_br_diff.astype(BF16), w_br_ssd.astype(BF16), w_out.astype(BF16)
    W_bw = (w_br_win.reshape(L, WIN_KV_HEADS, WIN_G, WIN_DH, D).transpose(0, 2, 1, 3, 4)
            .reshape(L, WIN_WIDTH, D).astype(BF16))
    W_up, W_down = ffn_w_up.astype(BF16), ffn_w_down.astype(BF16)

    for l in range(L):
        need_ctx = l < L - 1
        lam_init = 0.8 - 0.6 * math.exp(-0.3 * l)
        R = T if need_ctx else B * N
        mod3 = mod[l].reshape(16, 1, 6 * D)
        kw0 = dict(B=B, N=N, M=M)
        kw = dict(layer=l, **kw0)

        P = _inproj_call(Xa, Xb, mod3, norm_g[l, 0:1], W_in, tab, kinds, sc_part=1, sh_part=0, **kw)

        dargs = (P, diff_lambda[l], diff_norm[l].reshape(1, -1))
        YD = _diff_call(*dargs, with_lat=True, lam_init=lam_init, **kw0)
        if need_ctx:
            YD = jnp.concatenate([YD, _diff_call(*dargs, with_lat=False, lam_init=lam_init, **kw0)], axis=0)
        YW = _win_call(P, win_sink[l], need_ctx=need_ctx, **kw0)

        U = _ssdconv_call(P, ssd_conv_w[l], ssd_conv_b[l].reshape(1, -1), **kw0)
        bias = jnp.pad(ssd_dt_bias[l].reshape(1, -1), ((0, 0), (0, pad_lanes)))
        alog = jnp.pad(ssd_a_log[l].reshape(1, -1), ((0, 0), (0, pad_lanes)))
        YF, YB = _ssd_call(U, P, bias, alog, spread, **kw0)

        dsk = jnp.repeat(ssd_d[l], SSD_HEAD_DIM).reshape(1, -1)
        X1 = _merge_call(YD, YF, YB, U, P, YW, W_bd, W_bs, W_bw, W_out, dsk, ssd_norm[l].reshape(1, -1),
                         Xa, Xb, mod3, norm_g[l, 1:2], R=R, gate_part=2, **kw)

        A = _ffnup_call(X1, mod3, norm_g[l, 2:3], W_up, ffn_conv_w[l], ffn_conv_b[l].reshape(1, -1), edge,
                        R=R, sc_part=4, sh_part=3, **kw)
        Xa = Xb = _mnr_call(A, W_down, X1, mod3, norm_g[l, 3:4], R=R, gate_part=5,
                            tk=_pick((A.shape[1],), 2816, 512, 128), name="ffn_down", **kw)

    return Xa[:B * N].reshape(B, N, D)
```

```python
import functools
import math

import jax
import jax.numpy as jnp
from jax import lax
from jax.experimental import pallas as pl
from jax.experimental.pallas import tpu as pltpu

F32 = jnp.float32
BF16 = jnp.bfloat16

GRID_W = 64
ROPE_BASE = 10000.0
NORM_EPS = 1e-6
N_DIFF_HEADS = 8
DIFF_DH = 64
DIFF_WIDTH = N_DIFF_HEADS * 2 * DIFF_DH
SSD_INNER = 1024
SSD_HEAD_DIM = 64
SSD_HEADS = 16
SSD_GROUPS = 4
SSD_STATE = 128
SSD_CONV_W = 5
SSD_CHUNK = 128
SSD_XBC = SSD_INNER + 2 * SSD_GROUPS * SSD_STATE
WIN_HEADS = 16
WIN_KV_HEADS = 4
WIN_G = WIN_HEADS // WIN_KV_HEADS
WIN_DH = 64
WINDOW = 128
WIN_BLOCK = 128
WIN_WIDTH = WIN_HEADS * WIN_DH
WIN_KV_WIDTH = WIN_KV_HEADS * WIN_DH
FFN_CONV_W = 3

LOG2E = 1.4426950408889634
NEG_BIG = -1e30

LANES = 128
HALO = 16
QCOLS = 256
DIFF_LOOKAHEAD = 3
WIN_LOOKAHEAD = 2
SSD_STEP_CHUNKS = 2
CONV_ROWS = 256
VMEM_LIMIT = 56 * 1024 * 1024

PW = 1024
COL_DQ = 0 * PW
COL_DK = 1 * PW
COL_WQ = 2 * PW
COL_WKV = 3 * PW
COL_DT = COL_WKV + 2 * WIN_KV_WIDTH
COL_DV = 4 * PW
COL_SZ = 5 * PW
COL_XBC = 6 * PW
COL_GL = 8 * PW
P_WIDTH = 14 * PW
KIND_NONE, KIND_Q, KIND_K, KIND_K256 = 0, 1, 2, 3
COL_KINDS = (KIND_Q, KIND_K, KIND_Q, KIND_K256) + (KIND_NONE,) * 10


def _cparams(*sem):
    return pltpu.CompilerParams(dimension_semantics=sem, vmem_limit_bytes=VMEM_LIMIT)


def _sigmoid(x):
    return 1.0 / (1.0 + jnp.exp(-x))


def _silu(x):
    return x * _sigmoid(x)


def _pick(n, *cands):
    for c in cands:
        if all(v % c == 0 for v in n):
            return c
    raise ValueError(f"no block size among {cands} divides {n}")


def _mod_kernel(cv_ref, w_ref, b_ref, o_ref):
    s = _silu(cv_ref[...]).astype(BF16)
    o_ref[0] = jnp.dot(s, w_ref[0].astype(BF16), preferred_element_type=F32) + b_ref[0]


def _mod_call(cv, w_ada, b_ada):
    L, D, W6 = w_ada.shape
    tn = 1024
    return pl.pallas_call(
        _mod_kernel,
        grid=(L, W6 // tn),
        in_specs=[
            pl.BlockSpec((16, D), lambda l, j: (0, 0)),
            pl.BlockSpec((1, D, tn), lambda l, j: (l, 0, j)),
            pl.BlockSpec((1, 1, tn), lambda l, j: (l, 0, j)),
        ],
        out_specs=pl.BlockSpec((1, 16, tn), lambda l, j: (l, 0, j)),
        out_shape=jax.ShapeDtypeStruct((L, 16, W6), F32),
        compiler_params=_cparams("parallel", "parallel"),
        name="adaln_mod",
    )(cv, w_ada, b_ada.reshape(L, 1, W6))


def _norm_mod(x, g, sc, sh):
    ms = jnp.mean(x * x, axis=-1, keepdims=True)
    y = x * lax.rsqrt(ms + NORM_EPS) * g
    return y * (1.0 + sc) + sh


def _rope(y, cos, sa, sb):
    w = y.shape[1]
    rep = w // LANES
    if rep > 1:
        cos, sa, sb = (jnp.tile(t, (1, rep)) for t in (cos, sa, sb))
    up = pltpu.roll(y, w - DIFF_DH // 2, axis=1)
    dn = pltpu.roll(y, DIFF_DH // 2, axis=1)
    return y * cos + up * sa + dn * sb


def _inproj_kernel(kind_ref, xa_ref, xb_ref, sc_ref, sh_ref, g_ref, w_ref, tab_ref, o_ref, h_scr, *, n_first):
    j = pl.program_id(1)

    @pl.when(j == 0)
    def _():
        x = jnp.where(pl.program_id(0) < n_first, xa_ref[...], xb_ref[...])
        h_scr[...] = _norm_mod(x, g_ref[...], sc_ref[0], sh_ref[0]).astype(BF16)

    dot = lambda: jnp.dot(h_scr[...], w_ref[...], preferred_element_type=F32)
    kind = kind_ref[j]

    @pl.when(kind == KIND_NONE)
    def _():
        o_ref[...] = dot().astype(BF16)

    @pl.when(jnp.logical_or(kind == KIND_Q, kind == KIND_K))
    def _():
        o_ref[...] = _rope(dot(), tab_ref[0, 0], tab_ref[0, 1], tab_ref[0, 2]).astype(BF16)

    @pl.when(kind == KIND_K256)
    def _():
        kw = WIN_KV_WIDTH
        y = dot()
        o_ref[:, :kw] = _rope(y[:, :kw], tab_ref[0, 0], tab_ref[0, 1], tab_ref[0, 2]).astype(BF16)
        o_ref[:, kw:] = y[:, kw:].astype(BF16)


def _inproj_call(Xa, Xb, mod3, g, W, tab, kinds, *, layer, B, N, M, sc_part, sh_part):
    D = Xa.shape[1]
    T = B * N + B * M
    tm = _pick((N, B * M), 1024, 512, 256)
    n_first = min(Xa.shape[0], T) // tm
    nlat = N // tm
    grp = lambda i: jnp.minimum((i * tm) // N, B)
    rblk = lambda i: jnp.where(i < B * nlat, i % nlat, nlat)
    ksel = lambda k: jnp.where(jnp.logical_or(k == KIND_K, k == KIND_K256), 1, 0)
    grid_spec = pltpu.PrefetchScalarGridSpec(
        num_scalar_prefetch=1,
        grid=(T // tm, P_WIDTH // PW),
        in_specs=[
            pl.BlockSpec((tm, D), lambda i, j, kr: (jnp.minimum(i, n_first - 1), 0)),
            pl.BlockSpec((tm, D), lambda i, j, kr: (jnp.maximum(i - n_first, 0), 0), pipeline_mode=pl.Buffered(1)),
            pl.BlockSpec((1, 1, D), lambda i, j, kr: (grp(i), 0, sc_part)),
            pl.BlockSpec((1, 1, D), lambda i, j, kr: (grp(i), 0, sh_part)),
            pl.BlockSpec((1, D), lambda i, j, kr: (0, 0)),
            pl.BlockSpec((None, D, PW), lambda i, j, kr: (layer, 0, j)),
            pl.BlockSpec((1, 3, tm, LANES), lambda i, j, kr: (ksel(kr[j]), 0, rblk(i), 0)),
        ],
        out_specs=pl.BlockSpec((tm, PW), lambda i, j, kr: (i, j)),
        scratch_shapes=[pltpu.VMEM((tm, D), BF16)],
    )
    return pl.pallas_call(
        functools.partial(_inproj_kernel, n_first=n_first),
        grid_spec=grid_spec,
        out_shape=jax.ShapeDtypeStruct((T, P_WIDTH), BF16),
        compiler_params=_cparams("parallel", "arbitrary"),
        name="in_proj",
    )(kinds, Xa, Xb, mod3, mod3, g, W, tab)


def _ffnup_kernel(x_ref, xp_ref, xn_ref, sc_ref, sh_ref, g_ref, wa_ref, wg_ref, cwa_ref, cwg_ref, ba_ref, bg_ref,
                  edge_ref, o_ref, h_scr, bufa, bufg):
    tm = x_ref.shape[0]

    @pl.when(pl.program_id(1) == 0)
    def _():
        nm = lambda r: _norm_mod(r[...], g_ref[...], sc_ref[0], sh_ref[0]).astype(BF16)
        h_scr[0:HALO] = nm(xp_ref)
        h_scr[HALO:HALO + tm] = nm(x_ref)
        h_scr[HALO + tm:] = nm(xn_ref)

    rep = o_ref.shape[1] // LANES
    prev_ok = jnp.tile(edge_ref[0], (1, rep))
    next_ok = jnp.tile(edge_ref[1], (1, rep))

    def conv(buf, w_ref, cw_ref, b_ref):
        buf[...] = jnp.dot(h_scr[...], w_ref[...], preferred_element_type=F32)
        taps = [buf[pl.ds(HALO - 1 + j, tm), :] for j in range(FFN_CONV_W)]
        out = b_ref[...] + (taps[0] * prev_ok) * cw_ref[0:1, :]
        out = out + taps[1] * cw_ref[1:2, :]
        return out + (taps[2] * next_ok) * cw_ref[2:3, :]

    a = conv(bufa, wa_ref, cwa_ref, ba_ref)
    g = conv(bufg, wg_ref, cwg_ref, bg_ref)
    o_ref[...] = (_silu(a) * g).astype(BF16)


def _ffnup_call(X, mod3, g, W, cw, cb, edge, *, layer, R, B, N, M, sc_part, sh_part):
    D = X.shape[1]
    F = W.shape[2] // 2
    tm = _pick((N, B * M), 1024, 512, 256)
    tn = _pick((F,), 512, 128)
    nc = F // tn
    hb = tm // HALO
    nh = R // HALO
    grp = lambda i: jnp.minimum((i * tm) // N, B)
    col = lambda shape, off: pl.BlockSpec(shape, lambda i, j: (0, j + off))
    return pl.pallas_call(
        _ffnup_kernel,
        grid=(R // tm, nc),
        in_specs=[
            pl.BlockSpec((tm, D), lambda i, j: (i, 0)),
            pl.BlockSpec((HALO, D), lambda i, j: (jnp.maximum(i * hb - 1, 0), 0)),
            pl.BlockSpec((HALO, D), lambda i, j: (jnp.minimum((i + 1) * hb, nh - 1), 0)),
            pl.BlockSpec((1, 1, D), lambda i, j: (grp(i), 0, sc_part)),
            pl.BlockSpec((1, 1, D), lambda i, j: (grp(i), 0, sh_part)),
            pl.BlockSpec((1, D), lambda i, j: (0, 0)),
            pl.BlockSpec((None, D, tn), lambda i, j: (layer, 0, j)),
            pl.BlockSpec((None, D, tn), lambda i, j: (layer, 0, j + nc)),
            col((FFN_CONV_W, tn), 0), col((FFN_CONV_W, tn), nc),
            col((1, tn), 0), col((1, tn), nc),
            pl.BlockSpec((2, tm, LANES), lambda i, j: (0, i, 0)),
        ],
        out_specs=pl.BlockSpec((tm, tn), lambda i, j: (i, j)),
        out_shape=jax.ShapeDtypeStruct((R, F), BF16),
        scratch_shapes=[pltpu.VMEM((tm + 2 * HALO, D), BF16), pltpu.VMEM((tm + 2 * HALO, tn), F32),
                        pltpu.VMEM((tm + 2 * HALO, tn), F32)],
        compiler_params=_cparams("parallel", "arbitrary"),
        name="ffn_up_conv_gate",
    )(X, X, X, mod3, mod3, g, W, W, cw, cw, cb, cb, edge)


def _mnr_kernel(a_ref, w_ref, xa_ref, *refs, n_first):
    xb_ref, (gate_ref, g_ref, o_ref, acc) = (refs[0], refs[1:]) if len(refs) == 5 else (None, refs)
    k = pl.program_id(1)

    @pl.when(k == 0)
    def _():
        acc[...] = jnp.zeros_like(acc)

    acc[...] += jnp.dot(a_ref[...], w_ref[...], preferred_element_type=F32)

    @pl.when(k == pl.num_programs(1) - 1)
    def _():
        y = acc[...]
        ms = jnp.mean(y * y, axis=-1, keepdims=True)
        r = y * lax.rsqrt(ms + NORM_EPS) * g_ref[...]
        x = xa_ref[...]
        if xb_ref is not None:
            x = jnp.where(pl.program_id(0) < n_first, x, xb_ref[...])
        o_ref[...] = x + gate_ref[0] * r


def _mnr_call(A, W, Xa, Xb, mod3, g, *, layer, R, B, N, M, gate_part, tk, name):
    K, D = W.shape[1:]
    tm = _pick((N, B * M), 512, 256)
    n_first = min(Xa.shape[0], R) // tm
    xb_specs = [] if Xb is None else [pl.BlockSpec((tm, D), lambda i, k: (jnp.maximum(i - n_first, 0), 0))]
    grp = lambda i: jnp.minimum((i * tm) // N, B)
    return pl.pallas_call(
        functools.partial(_mnr_kernel, n_first=n_first),
        grid=(R // tm, K // tk),
        in_specs=[
            pl.BlockSpec((tm, tk), lambda i, k: (i, k)),
            pl.BlockSpec((None, tk, D), lambda i, k: (layer, k, 0)),
            pl.BlockSpec((tm, D), lambda i, k: (jnp.minimum(i, n_first - 1), 0)),
        ] + xb_specs + [
            pl.BlockSpec((1, 1, D), lambda i, k: (grp(i), 0, gate_part)),
            pl.BlockSpec((1, D), lambda i, k: (0, 0)),
        ],
        out_specs=pl.BlockSpec((tm, D), lambda i, k: (i, 0)),
        out_shape=jax.ShapeDtypeStruct((R, D), F32),
        scratch_shapes=[pltpu.VMEM((tm, D), F32)],
        compiler_params=_cparams("parallel", "arbitrary"),
        name=name,
    )(A, W, Xa, *([] if Xb is None else [Xb]), mod3, g)


def _diff_kernel(lam_ref, nw_ref, q_ref, *refs, tq, tk, with_lat, lam_init):
    if with_lat:
        kl_ref, kc_ref, vl_ref, vc_ref, o_ref, vt_scr, m_scr, acc_scr = refs
        N = kl_ref.shape[0]
    else:
        kc_ref, vc_ref, o_ref, vt_scr, m_scr, acc_scr = refs
        N = 0
    M = kc_ref.shape[0]
    hd = 2 * DIFF_DH
    n_lat_k = N // tk

    @pl.when(pl.program_id(2) == 0)
    def _():
        for c in range(n_lat_k):
            vt_scr[0:hd, c * tk:(c + 1) * tk] = vl_ref[c * tk:(c + 1) * tk, :].astype(F32).T.astype(BF16)
        vt_scr[0:hd, N:N + M] = vc_ref[...].astype(F32).T.astype(BF16)
        vt_scr[hd:, :] = jnp.ones((vt_scr.shape[0] - hd, N + M), BF16)

    q = q_ref[...]
    lane = lax.broadcasted_iota(jnp.int32, q.shape, 1)
    zero = jnp.zeros_like(q)
    qs = jnp.concatenate([jnp.where(lane < DIFF_DH, q, zero), jnp.where(lane >= DIFF_DH, q, zero)], axis=0)

    m_scr[...] = jnp.full_like(m_scr, NEG_BIG)
    acc_scr[...] = jnp.zeros_like(acc_scr)

    qcols = min(QCOLS, 2 * tq)
    chunks = [(c * tk, tk) for c in range(n_lat_k)] + [(N, M)]
    items = [(k0, kn, t * qcols) for (k0, kn) in chunks for t in range(2 * tq // qcols)]

    def keys(k0, kn):
        return kl_ref[k0:k0 + kn, :] if k0 < N else kc_ref[...]

    def scores(k0, kn, c0):
        return lax.dot_general(keys(k0, kn), qs[c0:c0 + qcols], (((1,), (1,)), ((), ())),
                               preferred_element_type=F32)

    def accumulate(s, k0, kn, c0):
        cols = slice(c0, c0 + qcols)
        m_prev = m_scr[:, cols]
        m_new = jnp.maximum(m_prev, jnp.max(s, axis=0, keepdims=True))
        alpha = jnp.exp2(m_prev - m_new)
        p = jnp.exp2(s - m_new).astype(BF16)
        acc_scr[:, cols] = alpha * acc_scr[:, cols] + jnp.dot(vt_scr[:, k0:k0 + kn], p, preferred_element_type=F32)
        m_scr[:, cols] = m_new

    ahead = min(DIFF_LOOKAHEAD, len(items))
    pending = [scores(*it) for it in items[:ahead]]
    for n, item in enumerate(items):
        if n + ahead < len(items):
            pending.append(scores(*items[n + ahead]))
        accumulate(pending.pop(0), *item)

    lp = lam_ref[...]
    lam = (jnp.exp(jnp.sum(lp[0:1] * lp[1:2], axis=1, keepdims=True))
           - jnp.exp(jnp.sum(lp[2:3] * lp[3:4], axis=1, keepdims=True)) + lam_init)
    acc = acc_scr[...]
    o = acc[:hd, :tq] / acc[hd:hd + 1, :tq] - lam * (acc[:hd, tq:] / acc[hd:hd + 1, tq:])
    ms = jnp.mean(o * o, axis=0, keepdims=True)
    r = o * (lax.rsqrt(ms + NORM_EPS) * (1.0 - lam_init))
    o_ref[...] = (r.T * nw_ref[...]).astype(BF16)


def _diff_call(P, lam_p, nw, *, B, N, M, with_lat, lam_init):
    tk = _pick((N,), 512, 256)
    hw = 2 * DIFF_DH
    cq, ck, cv = COL_DQ // hw, COL_DK // hw, COL_DV // hw
    ctx_blk = B * N // M
    small = lambda: pl.BlockSpec((4, DIFF_DH), lambda b, h, qi: (0, 0))
    lat = lambda col: pl.BlockSpec((N, hw), lambda b, h, qi: (b, col + h))
    ctx = lambda col: pl.BlockSpec((M, hw), lambda b, h, qi: (ctx_blk + b, col + h))
    if with_lat:
        tq = _pick((N,), 2048, 1024, 512, 256)
        nq, q0, keys = N // tq, 0, N + M
        kv_specs = [lat(ck), ctx(ck), lat(cv), ctx(cv)]
    else:
        tq = M
        nq, q0, keys = 1, B * N // tq, M
        kv_specs = [ctx(ck), ctx(cv)]
    kern = functools.partial(_diff_kernel, tq=tq, tk=tk, with_lat=with_lat, lam_init=lam_init)
    return pl.pallas_call(
        kern,
        grid=(B, N_DIFF_HEADS, nq),
        in_specs=[
            small(),
            pl.BlockSpec((1, hw), lambda b, h, qi: (0, 0)),
            pl.BlockSpec((tq, hw), lambda b, h, qi: (q0 + b * nq + qi, cq + h)),
        ] + kv_specs,
        out_specs=pl.BlockSpec((tq, hw), lambda b, h, qi: (b * nq + qi, h)),
        out_shape=jax.ShapeDtypeStruct((B * nq * tq, DIFF_WIDTH), BF16),
        scratch_shapes=[pltpu.VMEM((hw + HALO, keys), BF16), pltpu.VMEM((1, 2 * tq), F32),
                        pltpu.VMEM((hw + HALO, 2 * tq), F32)],
        compiler_params=_cparams("parallel", "parallel", "arbitrary"),
        name="diff_attn" if with_lat else "diff_attn_ctx",
    )(lam_p, nw, P, *([P] * len(kv_specs)))


def _win_kernel(sink_ref, q_ref, kp_ref, kc_ref, kn_ref, kx_ref, o_ref, *, nb, N, M):
    i = pl.program_id(1)
    wb, kw = WIN_BLOCK, WIN_KV_WIDTH
    q = q_ref[...]
    kcat = jnp.concatenate([kp_ref[:, :kw], kc_ref[:, :kw], kn_ref[:, :kw], kx_ref[:, :kw]], axis=0)
    vcat = jnp.concatenate([kp_ref[:, kw:], kc_ref[:, kw:], kn_ref[:, kw:], kx_ref[:, kw:]], axis=0)
    nk = 3 * wb + M
    rows = WIN_G * wb
    row = lax.broadcasted_iota(jnp.int32, (rows, nk), 0) % wb
    col = lax.broadcasted_iota(jnp.int32, (rows, nk), 1)
    kpos = (i - 1) * wb + col
    n_band = jnp.where(i < nb, N, 0)
    band_ok = (jnp.abs(col - wb - row) <= WINDOW) & (kpos >= 0) & (kpos < n_band)
    valid = band_ok | (col >= 3 * wb)
    lane = lax.broadcasted_iota(jnp.int32, (wb, kw), 1) // WIN_DH
    lane_o = lax.broadcasted_iota(jnp.int32, (rows, kw), 1) // WIN_DH
    grow = lax.broadcasted_iota(jnp.int32, (rows, 1), 0) // wb
    zero = jnp.zeros((wb, kw), BF16)
    acc = jnp.zeros((rows, kw), F32)

    def scores(kv):
        qs = jnp.concatenate([jnp.where(lane == kv, q[:, g * kw:(g + 1) * kw], zero) for g in range(WIN_G)], axis=0)
        return lax.dot_general(qs, kcat, (((1,), (1,)), ((), ())), preferred_element_type=F32)

    pending = [scores(kv) for kv in range(WIN_LOOKAHEAD)]
    for kv in range(WIN_KV_HEADS):
        if kv + WIN_LOOKAHEAD < WIN_KV_HEADS:
            pending.append(scores(kv + WIN_LOOKAHEAD))
        s = jnp.where(valid, pending.pop(0), NEG_BIG)
        sink = jnp.zeros((rows, 1), F32)
        for g in range(WIN_G):
            sink = jnp.where(grow == g, sink_ref[kv * WIN_G + g] * LOG2E, sink)
        m = jnp.maximum(jnp.max(s, axis=1, keepdims=True), sink)
        p = jnp.exp2(s - m)
        l = jnp.sum(p, axis=1, keepdims=True) + jnp.exp2(sink - m)
        o = jnp.dot(p.astype(BF16), vcat, preferred_element_type=F32) / l
        acc = jnp.where(lane_o == kv, o, acc)
    for g in range(WIN_G):
        o_ref[:, g * kw:(g + 1) * kw] = acc[g * wb:(g + 1) * wb].astype(BF16)


def _win_call(P, sink, *, B, N, M, need_ctx):
    wb = WIN_BLOCK
    nb, nbc = N // wb, (M // wb if need_ctx else 0)
    R = B * N + (B * M if need_ctx else 0)
    cq, ckv = COL_WQ // PW, COL_WKV // (2 * WIN_KV_WIDTH)

    def qrow(b, i):
        return jnp.where(i < nb, b * nb + i, B * nb + b * nbc + (i - nb))

    def band(b, i, d):
        return b * nb + jnp.clip(i + d, 0, nb - 1)

    kvw = 2 * WIN_KV_WIDTH
    kern = functools.partial(_win_kernel, nb=nb, N=N, M=M)
    grid_spec = pltpu.PrefetchScalarGridSpec(
        num_scalar_prefetch=1,
        grid=(B, nb + nbc),
        in_specs=[
            pl.BlockSpec((wb, PW), lambda b, i, s: (qrow(b, i), cq)),
            pl.BlockSpec((wb, kvw), lambda b, i, s: (band(b, i, -1), ckv)),
            pl.BlockSpec((wb, kvw), lambda b, i, s: (band(b, i, 0), ckv)),
            pl.BlockSpec((wb, kvw), lambda b, i, s: (band(b, i, 1), ckv)),
            pl.BlockSpec((M, kvw), lambda b, i, s: (B * N // M + b, ckv)),
        ],
        out_specs=pl.BlockSpec((wb, PW), lambda b, i, s: (qrow(b, i), 0)),
    )
    return pl.pallas_call(
        kern,
        grid_spec=grid_spec,
        out_shape=jax.ShapeDtypeStruct((R, WIN_WIDTH), BF16),
        compiler_params=_cparams("parallel", "arbitrary"),
        name="win_attn",
    )(sink, P, P, P, P, P)


def _seq_edges(i, R_lat, N, M):
    r0 = i * CONV_ROWS
    lat = r0 < R_lat
    first = jnp.where(lat, r0 % N == 0, (r0 - R_lat) % M == 0)
    last = jnp.where(lat, (r0 + CONV_ROWS) % N == 0, (r0 + CONV_ROWS - R_lat) % M == 0)
    return first, last


def _conv_taps(buf, main_ref, prev_ref, next_ref, w_ref, b_ref, first, last, k):
    buf[0:HALO] = jnp.where(first, 0.0, prev_ref[...].astype(F32))
    buf[HALO:HALO + CONV_ROWS] = main_ref[...].astype(F32)
    buf[HALO + CONV_ROWS:] = jnp.where(last, 0.0, next_ref[...].astype(F32))
    out = b_ref[...]
    for j in range(k):
        out = out + buf[pl.ds(HALO - k // 2 + j, CONV_ROWS), :] * w_ref[j:j + 1, :]
    return out


def _ssdconv_kernel(m_ref, p_ref, n_ref, w_ref, b_ref, o_ref, buf, *, R_lat, N, M):
    first, last = _seq_edges(pl.program_id(0), R_lat, N, M)
    o_ref[...] = _silu(_conv_taps(buf, m_ref, p_ref, n_ref, w_ref, b_ref, first, last, SSD_CONV_W)).astype(BF16)


def _ssdconv_call(P, w, b, *, B, N, M):
    T = P.shape[0]
    cr, W = CONV_ROWS, SSD_XBC
    hb = cr // HALO
    cb = COL_XBC // W
    nh = T // HALO
    kern = functools.partial(_ssdconv_kernel, R_lat=B * N, N=N, M=M)
    return pl.pallas_call(
        kern,
        grid=(T // cr,),
        in_specs=[
            pl.BlockSpec((cr, W), lambda i: (i, cb)),
            pl.BlockSpec((HALO, W), lambda i: (jnp.maximum(i * hb - 1, 0), cb)),
            pl.BlockSpec((HALO, W), lambda i: (jnp.minimum((i + 1) * hb, nh - 1), cb)),
            pl.BlockSpec((SSD_CONV_W, W), lambda i: (0, 0)),
            pl.BlockSpec((1, W), lambda i: (0, 0)),
        ],
        out_specs=pl.BlockSpec((cr, W), lambda i: (i, 0)),
        out_shape=jax.ShapeDtypeStruct((T, W), BF16),
        scratch_shapes=[pltpu.VMEM((cr + 2 * HALO, W), F32)],
        compiler_params=_cparams("parallel"),
        name="ssd_conv",
    )(P, P, P, w, b)


def _cumsum_rows(a):
    n = a.shape[0]
    tri = (lax.broadcasted_iota(jnp.int32, (n, n), 0) >= lax.broadcasted_iota(jnp.int32, (n, n), 1)).astype(BF16)
    hi = a.astype(BF16)
    r1 = a - hi.astype(F32)
    mid = r1.astype(BF16)
    lo = (r1 - mid.astype(F32)).astype(BF16)
    dot = lambda v: jnp.dot(tri, v, preferred_element_type=F32)
    return dot(hi) + dot(mid) + dot(lo)


def _ssd_prep(dt_ref, bias_ref, alog_ref, e_ref, rows, *, reverse):
    ch = SSD_CHUNK
    x = dt_ref[rows, :].astype(F32) + bias_ref[...]
    dt = jnp.maximum(x, 0.0) + jnp.log(1.0 + jnp.exp(-jnp.abs(x)))
    ad = dt * (-jnp.exp(alog_ref[...]))
    acum = _cumsum_rows(ad)
    total = acum[ch - 1:ch, :]
    if reverse:
        cvec = acum - ad
        e_off = jnp.exp(total - cvec)
        e_ws = jnp.exp(cvec)
    else:
        cvec = acum
        e_off = jnp.exp(acum)
        e_ws = jnp.exp(total - acum)
    tot_rows = jnp.broadcast_to(jnp.exp(total), (HALO, LANES))
    scales = jnp.dot(jnp.concatenate([dt * e_ws, e_off, tot_rows], axis=0).astype(BF16), e_ref[...],
                     preferred_element_type=F32)
    return cvec, cvec.T, dt.T, scales


def _ssd_group(g, prep, u_ref, s_ref, y_ref, rows, *, reverse, lane0):
    cvec, cT, dtT, scales = prep
    ch, hp = SSD_CHUNK, SSD_HEAD_DIM
    hpg = SSD_HEADS // SSD_GROUPS
    gw = hpg * hp
    li = lax.broadcasted_iota(jnp.int32, (ch, ch), 0)
    si = lax.broadcasted_iota(jnp.int32, (ch, ch), 1)
    tri = (si >= li) if reverse else (li >= si)
    hsel = lax.broadcasted_iota(jnp.int32, (ch, gw), 1) // hp
    lanes = slice(g * gw, (g + 1) * gw)
    bg = u_ref[rows, SSD_INNER + g * SSD_STATE:SSD_INNER + (g + 1) * SSD_STATE]
    cg = u_ref[rows, SSD_INNER + (SSD_GROUPS + g) * SSD_STATE:SSD_INNER + (SSD_GROUPS + g + 1) * SSD_STATE]
    xs = u_ref[rows, lanes]
    cb = lax.dot_general(cg, bg, (((1,), (1,)), ((), ())), preferred_element_type=F32)
    sg = s_ref[:, lanes]
    yoff = jnp.dot(cg, sg.astype(BF16), preferred_element_type=F32) * scales[ch:2 * ch, lanes]
    xw = (xs.astype(F32) * scales[0:ch, lanes]).astype(BF16)
    s_ref[:, lanes] = (sg * scales[2 * ch:2 * ch + 1, lanes]
                       + lax.dot_general(bg, xw, (((0,), (0,)), ((), ())), preferred_element_type=F32))
    ydiag = jnp.zeros((ch, gw), F32)
    for r in range(hpg):
        j = lane0 + g * hpg + r
        colv = jnp.broadcast_to(cvec[:, j:j + 1], (ch, ch))
        rowv = jnp.broadcast_to(cT[j:j + 1, :], (ch, ch))
        d = (rowv - colv) if reverse else (colv - rowv)
        lm = jnp.exp(jnp.where(tri, d, NEG_BIG)) * dtT[j:j + 1, :]
        yh = jnp.dot((cb * lm).astype(BF16), xs, preferred_element_type=F32)
        ydiag = jnp.where(hsel == r, yh, ydiag)
    y_ref[rows, lanes] = (ydiag + yoff).astype(BF16)


def _ssd_kernel(uf_ref, dtf_ref, ub_ref, dtb_ref, bias_ref, alog_ref, e_ref, yf_ref, yb_ref, sf, sb):
    @pl.when(pl.program_id(1) == 0)
    def _():
        sf[...] = jnp.zeros_like(sf)
        sb[...] = jnp.zeros_like(sb)

    ch, n = SSD_CHUNK, SSD_STEP_CHUNKS
    f_rows = [slice(k * ch, (k + 1) * ch) for k in range(n)]
    b_rows = f_rows[::-1]
    pf = [_ssd_prep(dtf_ref, bias_ref, alog_ref, e_ref.at[0], r, reverse=False) for r in f_rows]
    pb = [_ssd_prep(dtb_ref, bias_ref, alog_ref, e_ref.at[1], r, reverse=True) for r in b_rows]
    for k in range(n):
        for g in range(SSD_GROUPS):
            _ssd_group(g, pf[k], uf_ref, sf, yf_ref, f_rows[k], reverse=False, lane0=0)
            _ssd_group(g, pb[k], ub_ref, sb, yb_ref, b_rows[k], reverse=True, lane0=SSD_HEADS)


def _ssd_call(U, P, bias, alog, spread, *, B, N, M):
    T = U.shape[0]
    ch = SSD_CHUNK * SSD_STEP_CHUNKS
    ncl, ncc = N // ch, M // ch
    cdt = COL_DT // LANES

    def fwd(b, t):
        return jnp.where(t < ncc, B * ncl + b * ncc + t, b * ncl + (t - ncc))

    def bwd(b, t):
        return jnp.where(t < ncc, B * ncl + b * ncc + (ncc - 1 - t), b * ncl + (ncl - 1 - (t - ncc)))

    return pl.pallas_call(
        _ssd_kernel,
        grid=(B, ncl + ncc),
        in_specs=[
            pl.BlockSpec((ch, SSD_XBC), lambda b, t: (fwd(b, t), 0)),
            pl.BlockSpec((ch, LANES), lambda b, t: (fwd(b, t), cdt)),
            pl.BlockSpec((ch, SSD_XBC), lambda b, t: (bwd(b, t), 0)),
            pl.BlockSpec((ch, LANES), lambda b, t: (bwd(b, t), cdt)),
            pl.BlockSpec((1, LANES), lambda b, t: (0, 0)),
            pl.BlockSpec((1, LANES), lambda b, t: (0, 0)),
            pl.BlockSpec((2, LANES, SSD_INNER), lambda b, t: (0, 0, 0)),
        ],
        out_specs=[
            pl.BlockSpec((ch, SSD_INNER), lambda b, t: (fwd(b, t), 0)),
            pl.BlockSpec((ch, SSD_INNER), lambda b, t: (bwd(b, t), 0)),
        ],
        out_shape=[jax.ShapeDtypeStruct((T, SSD_INNER), BF16)] * 2,
        scratch_shapes=[pltpu.VMEM((SSD_STATE, SSD_INNER), F32)] * 2,
        compiler_params=_cparams("parallel", "arbitrary"),
        name="ssd_scan",
    )(U, P, U, P, bias, alog, spread)


def _merge_kernel(yd_ref, yf_ref, yb_ref, xs_ref, z_ref, yw_ref, gd_ref, gs_ref, gw_ref,
                  wd_ref, ws_ref, ww_ref, dsk_ref, sn_ref, o_ref):
    y = yf_ref[...].astype(F32) + yb_ref[...].astype(F32) + dsk_ref[...] * xs_ref[...].astype(F32)
    y = y * _silu(z_ref[...].astype(F32))
    ms = jnp.mean(y * y, axis=-1, keepdims=True)
    s = (y * lax.rsqrt(ms + NORM_EPS) * sn_ref[...]).astype(BF16)
    dot = lambda a, w: jnp.dot(a, w[...], preferred_element_type=F32)
    sig = lambda r: _sigmoid(r[...].astype(F32))
    acc = sig(gd_ref) * dot(yd_ref[...], wd_ref)
    acc = acc + sig(gs_ref) * dot(s, ws_ref)
    acc = acc + sig(gw_ref) * dot(yw_ref[...], ww_ref)
    o_ref[...] = acc.astype(BF16)


def _merge_call(YD, YF, YB, U, P, YW, wd, ws, ww, dsk, sn, *, layer, R, D):
    tm = 256
    W = SSD_INNER
    cg = COL_GL // D
    row = lambda c: pl.BlockSpec((tm, W), lambda i: (i, c))
    gate = lambda c: pl.BlockSpec((tm, D), lambda i: (i, cg + c))
    wspec = pl.BlockSpec((None, W, D), lambda i: (layer, 0, 0))
    vec = pl.BlockSpec((1, W), lambda i: (0, 0))
    return pl.pallas_call(
        _merge_kernel,
        grid=(R // tm,),
        in_specs=[row(0), row(0), row(0), row(0), row(COL_SZ // W), row(0),
                  gate(0), gate(1), gate(2), wspec, wspec, wspec, vec, vec],
        out_specs=pl.BlockSpec((tm, D), lambda i: (i, 0)),
        out_shape=jax.ShapeDtypeStruct((R, D), BF16),
        compiler_params=_cparams("parallel"),
        name="branch_merge",
    )(YD, YF, YB, U, P, YW, P, P, P, wd, ws, ww, dsk, sn)


def _rope_tables(N, rows_ctx):
    hd = DIFF_DH
    pos = jnp.arange(N)
    row = (pos // GRID_W).astype(F32)
    colp = (pos % GRID_W).astype(F32)
    axis_dim = hd // 2
    inv = ROPE_BASE ** (-jnp.arange(0, axis_dim, 2, dtype=F32) / axis_dim)
    ang = jnp.concatenate([row[:, None] * inv, colp[:, None] * inv], axis=-1)
    lane = jnp.arange(LANES)
    cos = jnp.cos(ang)[:, lane % (hd // 2)]
    sin = jnp.sin(ang)[:, lane % (hd // 2)]
    first = (lane % hd) < hd // 2
    lat = jnp.stack([cos, jnp.where(first, -sin, 0.0), jnp.where(first, 0.0, sin)])
    ctx = jnp.stack([jnp.ones((rows_ctx, LANES), F32), jnp.zeros((rows_ctx, LANES), F32),
                     jnp.zeros((rows_ctx, LANES), F32)])
    k_tab = jnp.concatenate([lat, ctx], axis=1)
    q_tab = k_tab * (DIFF_DH ** -0.5 * LOG2E)
    return jnp.stack([q_tab, k_tab])


def _edge_table(B, N, M):
    pos = jnp.concatenate([jnp.tile(jnp.arange(N), B), jnp.tile(jnp.arange(M), B)])
    last = jnp.concatenate([jnp.full((B * N,), N - 1), jnp.full((B * M,), M - 1)])
    t = jnp.stack([pos != 0, pos != last]).astype(F32)
    return jnp.broadcast_to(t[:, :, None], t.shape + (LANES,))


def _layout_w_in(w):
    L, D = w.shape[:2]
    sizes = (DIFF_WIDTH, DIFF_WIDTH, DIFF_WIDTH, SSD_INNER, SSD_XBC, 2 * SSD_HEADS,
             WIN_WIDTH, WIN_KV_WIDTH, WIN_KV_WIDTH, 3 * D)
    parts, s = [], 0
    for z in sizes:
        parts.append(w[..., s:s + z])
        s += z
    dq, dk, dv, sz, sxbc, sdt, wq, wk, wv, gl = parts
    wq = wq.reshape(L, D, WIN_KV_HEADS, WIN_G, WIN_DH).transpose(0, 1, 3, 2, 4).reshape(L, D, WIN_WIDTH)
    out = jnp.zeros((L, D, P_WIDTH), BF16)
    for off, piece in ((COL_DQ, dq), (COL_DK, dk), (COL_WQ, wq), (COL_WKV, wk), (COL_WKV + WIN_KV_WIDTH, wv),
                       (COL_DT, sdt), (COL_DV, dv), (COL_SZ, sz), (COL_XBC, sxbc), (COL_GL, gl)):
        out = lax.dynamic_update_slice(out, piece.astype(BF16), (0, 0, off))
    return out


def kernel(x, c, ctx, c_ctx, w_ada, b_ada, norm_g, w_in, diff_lambda, diff_norm, ssd_conv_w, ssd_conv_b,
           ssd_a_log, ssd_dt_bias, ssd_d, ssd_norm, win_sink, w_br_diff, w_br_ssd, w_br_win, w_out,
           ffn_w_up, ffn_conv_w, ffn_conv_b, ffn_w_down):
    B, N, D = x.shape
    M = ctx.shape[1]
    L = w_ada.shape[0]
    assert D == 2 * PW and P_WIDTH == COL_GL + 3 * D
    assert N % CONV_ROWS == 0 and M % CONV_ROWS == 0 and (B * N) % M == 0 and B + 1 <= 16
    assert CONV_ROWS % (SSD_CHUNK * SSD_STEP_CHUNKS) == 0
    T = B * N + B * M

    Xa, Xb = x.reshape(B * N, D), ctx.reshape(B * M, D)
    cv = jnp.zeros((16, D), F32).at[:B].set(c).at[B].set(c_ctx)
    mod = _mod_call(cv, w_ada, b_ada)

    tm_in = _pick((N, B * M), 1024, 512, 256)
    tab = _rope_tables(N, tm_in)
    kinds = jnp.asarray(COL_KINDS, jnp.int32)
    edge = _edge_table(B, N, M)
    head_of_lane = jnp.arange(SSD_INNER) // SSD_HEAD_DIM
    spread = jnp.stack([(jnp.arange(LANES)[:, None] == head_of_lane[None, :] + d * SSD_HEADS)
                        for d in range(2)]).astype(BF16)
    pad_lanes = LANES - 2 * SSD_HEADS

    W_in = _layout_w_in(w_in)
    W_bd, W_bs, W_out = w_br_diff.astype(BF16), w_br_ssd.astype(BF16), w_out.astype(BF16)
    W_bw = (w_br_win.reshape(L, WIN_KV_HEADS, WIN_G, WIN_DH, D).transpose(0, 2, 1, 3, 4)
            .reshape(L, WIN_WIDTH, D).astype(BF16))
    W_up, W_down = ffn_w_up.astype(BF16), ffn_w_down.astype(BF16)

    for l in range(L):
        need_ctx = l < L - 1
        lam_init = 0.8 - 0.6 * math.exp(-0.3 * l)
        R = T if need_ctx else B * N
        mod3 = mod[l].reshape(16, 1, 6 * D)
        kw0 = dict(B=B, N=N, M=M)
        kw = dict(layer=l, **kw0)

        P = _inproj_call(Xa, Xb, mod3, norm_g[l, 0:1], W_in, tab, kinds, sc_part=1, sh_part=0, **kw)

        dargs = (P, diff_lambda[l], diff_norm[l].reshape(1, -1))
        YD = _diff_call(*dargs, with_lat=True, lam_init=lam_init, **kw0)
        if need_ctx:
            YD = jnp.concatenate([YD, _diff_call(*dargs, with_lat=False, lam_init=lam_init, **kw0)], axis=0)
        YW = _win_call(P, win_sink[l], need_ctx=need_ctx, **kw0)

        U = _ssdconv_call(P, ssd_conv_w[l], ssd_conv_b[l].reshape(1, -1), **kw0)
        bias = jnp.pad(ssd_dt_bias[l].reshape(1, -1), ((0, 0), (0, pad_lanes)))
        alog = jnp.pad(ssd_a_log[l].reshape(1, -1), ((0, 0), (0, pad_lanes)))
        YF, YB = _ssd_call(U, P, bias, alog, spread, **kw0)

        dsk = jnp.repeat(ssd_d[l], SSD_HEAD_DIM).reshape(1, -1)
        Z = _merge_call(YD, YF, YB, U, P, YW, W_bd, W_bs, W_bw, dsk, ssd_norm[l].reshape(1, -1), layer=l, R=R, D=D)
        X1 = _mnr_call(Z, W_out, Xa, Xb if l == 0 else None, mod3, norm_g[l, 1:2], R=R, gate_part=2, tk=W_out.shape[1],
                       name="out_proj", **kw)

        A = _ffnup_call(X1, mod3, norm_g[l, 2:3], W_up, ffn_conv_w[l], ffn_conv_b[l].reshape(1, -1), edge,
                        R=R, sc_part=4, sh_part=3, **kw)
        Xa = Xb = _mnr_call(A, W_down, X1, None, mod3, norm_g[l, 3:4], R=R, gate_part=5,
                            tk=_pick((A.shape[1],), 2816, 512, 128), name="ffn_down", **kw)

    return Xa[:B * N].reshape(B, N, D)
```

```python
import functools
import math

import jax
import jax.numpy as jnp
from jax import lax
from jax.experimental import pallas as pl
from jax.experimental.pallas import tpu as pltpu

F32 = jnp.float32
BF16 = jnp.bfloat16

GRID_W = 64
ROPE_BASE = 10000.0
NORM_EPS = 1e-6
N_DIFF_HEADS = 8
DIFF_DH = 64
DIFF_WIDTH = N_DIFF_HEADS * 2 * DIFF_DH
SSD_INNER = 1024
SSD_HEAD_DIM = 64
SSD_HEADS = 16
SSD_GROUPS = 4
SSD_STATE = 128
SSD_CONV_W = 5
SSD_CHUNK = 128
SSD_XBC = SSD_INNER + 2 * SSD_GROUPS * SSD_STATE
WIN_HEADS = 16
WIN_KV_HEADS = 4
WIN_G = WIN_HEADS // WIN_KV_HEADS
WIN_DH = 64
WINDOW = 128
WIN_BLOCK = 128
WIN_WIDTH = WIN_HEADS * WIN_DH
WIN_KV_WIDTH = WIN_KV_HEADS * WIN_DH
FFN_CONV_W = 3

LOG2E = 1.4426950408889634
NEG_BIG = -1e30

LANES = 128
HALO = 16
QCOLS = 256
DIFF_LOOKAHEAD = 3
WIN_LOOKAHEAD = 2
SSD_STEP_CHUNKS = 2
CONV_ROWS = 256
VMEM_LIMIT = 56 * 1024 * 1024

PW = 1024
COL_DQ = 0 * PW
COL_DK = 1 * PW
COL_WQ = 2 * PW
COL_WKV = 3 * PW
COL_DT = COL_WKV + 2 * WIN_KV_WIDTH
COL_DV = 4 * PW
COL_SZ = 5 * PW
COL_XBC = 6 * PW
COL_GL = 8 * PW
P_WIDTH = 14 * PW
KIND_NONE, KIND_Q, KIND_K, KIND_K256 = 0, 1, 2, 3
COL_KINDS = (KIND_Q, KIND_K, KIND_Q, KIND_K256) + (KIND_NONE,) * 10


def _cparams(*sem):
    return pltpu.CompilerParams(dimension_semantics=sem, vmem_limit_bytes=VMEM_LIMIT)


def _sigmoid(x):
    return 1.0 / (1.0 + jnp.exp(-x))


def _silu(x):
    return x * _sigmoid(x)


def _pick(n, *cands):
    for c in cands:
        if all(v % c == 0 for v in n):
            return c
    raise ValueError(f"no block size among {cands} divides {n}")


def _mod_kernel(cv_ref, w_ref, b_ref, o_ref):
    s = _silu(cv_ref[...]).astype(BF16)
    o_ref[0] = jnp.dot(s, w_ref[0].astype(BF16), preferred_element_type=F32) + b_ref[0]


def _mod_call(cv, w_ada, b_ada):
    L, D, W6 = w_ada.shape
    tn = 1024
    return pl.pallas_call(
        _mod_kernel,
        grid=(L, W6 // tn),
        in_specs=[
            pl.BlockSpec((16, D), lambda l, j: (0, 0)),
            pl.BlockSpec((1, D, tn), lambda l, j: (l, 0, j)),
            pl.BlockSpec((1, 1, tn), lambda l, j: (l, 0, j)),
        ],
        out_specs=pl.BlockSpec((1, 16, tn), lambda l, j: (l, 0, j)),
        out_shape=jax.ShapeDtypeStruct((L, 16, W6), F32),
        compiler_params=_cparams("parallel", "parallel"),
        name="adaln_mod",
    )(cv, w_ada, b_ada.reshape(L, 1, W6))


def _norm_mod(x, g, sc, sh):
    ms = jnp.mean(x * x, axis=-1, keepdims=True)
    y = x * lax.rsqrt(ms + NORM_EPS) * g
    return y * (1.0 + sc) + sh


def _rope(y, cos, sa, sb):
    w = y.shape[1]
    rep = w // LANES
    if rep > 1:
        cos, sa, sb = (jnp.tile(t, (1, rep)) for t in (cos, sa, sb))
    up = pltpu.roll(y, w - DIFF_DH // 2, axis=1)
    dn = pltpu.roll(y, DIFF_DH // 2, axis=1)
    return y * cos + up * sa + dn * sb


def _inproj_kernel(kind_ref, xa_ref, xb_ref, sc_ref, sh_ref, g_ref, w_ref, tab_ref, o_ref, h_scr, *, n_first):
    j = pl.program_id(1)

    @pl.when(j == 0)
    def _():
        x = jnp.where(pl.program_id(0) < n_first, xa_ref[...], xb_ref[...])
        h_scr[...] = _norm_mod(x, g_ref[...], sc_ref[0], sh_ref[0]).astype(BF16)

    dot = lambda: jnp.dot(h_scr[...], w_ref[...], preferred_element_type=F32)
    kind = kind_ref[j]

    @pl.when(kind == KIND_NONE)
    def _():
        o_ref[...] = dot().astype(BF16)

    @pl.when(jnp.logical_or(kind == KIND_Q, kind == KIND_K))
    def _():
        o_ref[...] = _rope(dot(), tab_ref[0, 0], tab_ref[0, 1], tab_ref[0, 2]).astype(BF16)

    @pl.when(kind == KIND_K256)
    def _():
        kw = WIN_KV_WIDTH
        y = dot()
        o_ref[:, :kw] = _rope(y[:, :kw], tab_ref[0, 0], tab_ref[0, 1], tab_ref[0, 2]).astype(BF16)
        o_ref[:, kw:] = y[:, kw:].astype(BF16)


def _inproj_call(Xa, Xb, mod3, g, W, tab, kinds, *, layer, B, N, M, sc_part, sh_part):
    D = Xa.shape[1]
    T = B * N + B * M
    tm = _pick((N, B * M), 1024, 512, 256)
    n_first = min(Xa.shape[0], T) // tm
    nlat = N // tm
    grp = lambda i: jnp.minimum((i * tm) // N, B)
    rblk = lambda i: jnp.where(i < B * nlat, i % nlat, nlat)
    ksel = lambda k: jnp.where(jnp.logical_or(k == KIND_K, k == KIND_K256), 1, 0)
    grid_spec = pltpu.PrefetchScalarGridSpec(
        num_scalar_prefetch=1,
        grid=(T // tm, P_WIDTH // PW),
        in_specs=[
            pl.BlockSpec((tm, D), lambda i, j, kr: (jnp.minimum(i, n_first - 1), 0)),
            pl.BlockSpec((tm, D), lambda i, j, kr: (jnp.maximum(i - n_first, 0), 0), pipeline_mode=pl.Buffered(1)),
            pl.BlockSpec((1, 1, D), lambda i, j, kr: (grp(i), 0, sc_part)),
            pl.BlockSpec((1, 1, D), lambda i, j, kr: (grp(i), 0, sh_part)),
            pl.BlockSpec((1, D), lambda i, j, kr: (0, 0)),
            pl.BlockSpec((None, D, PW), lambda i, j, kr: (layer, 0, j)),
            pl.BlockSpec((1, 3, tm, LANES), lambda i, j, kr: (ksel(kr[j]), 0, rblk(i), 0)),
        ],
        out_specs=pl.BlockSpec((tm, PW), lambda i, j, kr: (i, j)),
        scratch_shapes=[pltpu.VMEM((tm, D), BF16)],
    )
    return pl.pallas_call(
        functools.partial(_inproj_kernel, n_first=n_first),
        grid_spec=grid_spec,
        out_shape=jax.ShapeDtypeStruct((T, P_WIDTH), BF16),
        compiler_params=_cparams("parallel", "arbitrary"),
        name="in_proj",
    )(kinds, Xa, Xb, mod3, mod3, g, W, tab)


def _ffnup_kernel(x_ref, xp_ref, xn_ref, sc_ref, sh_ref, g_ref, wa_ref, wg_ref, cwa_ref, cwg_ref, ba_ref, bg_ref,
                  edge_ref, o_ref, h_scr, bufa, bufg):
    tm = x_ref.shape[0]

    @pl.when(pl.program_id(1) == 0)
    def _():
        nm = lambda r: _norm_mod(r[...], g_ref[...], sc_ref[0], sh_ref[0]).astype(BF16)
        h_scr[0:HALO] = nm(xp_ref)
        h_scr[HALO:HALO + tm] = nm(x_ref)
        h_scr[HALO + tm:] = nm(xn_ref)

    rep = o_ref.shape[1] // LANES
    prev_ok = jnp.tile(edge_ref[0], (1, rep))
    next_ok = jnp.tile(edge_ref[1], (1, rep))

    def conv(buf, w_ref, cw_ref, b_ref):
        buf[...] = jnp.dot(h_scr[...], w_ref[...], preferred_element_type=F32)
        taps = [buf[pl.ds(HALO - 1 + j, tm), :] for j in range(FFN_CONV_W)]
        out = b_ref[...] + (taps[0] * prev_ok) * cw_ref[0:1, :]
        out = out + taps[1] * cw_ref[1:2, :]
        return out + (taps[2] * next_ok) * cw_ref[2:3, :]

    a = conv(bufa, wa_ref, cwa_ref, ba_ref)
    g = conv(bufg, wg_ref, cwg_ref, bg_ref)
    o_ref[...] = (_silu(a) * g).astype(BF16)


def _ffnup_call(X, mod3, g, W, cw, cb, edge, *, layer, R, B, N, M, sc_part, sh_part):
    D = X.shape[1]
    F = W.shape[2] // 2
    tm = _pick((N, B * M), 1024, 512, 256)
    tn = _pick((F,), 512, 128)
    nc = F // tn
    hb = tm // HALO
    nh = R // HALO
    grp = lambda i: jnp.minimum((i * tm) // N, B)
    col = lambda shape, off: pl.BlockSpec(shape, lambda i, j: (0, j + off))
    return pl.pallas_call(
        _ffnup_kernel,
        grid=(R // tm, nc),
        in_specs=[
            pl.BlockSpec((tm, D), lambda i, j: (i, 0)),
            pl.BlockSpec((HALO, D), lambda i, j: (jnp.maximum(i * hb - 1, 0), 0)),
            pl.BlockSpec((HALO, D), lambda i, j: (jnp.minimum((i + 1) * hb, nh - 1), 0)),
            pl.BlockSpec((1, 1, D), lambda i, j: (grp(i), 0, sc_part)),
            pl.BlockSpec((1, 1, D), lambda i, j: (grp(i), 0, sh_part)),
            pl.BlockSpec((1, D), lambda i, j: (0, 0)),
            pl.BlockSpec((None, D, tn), lambda i, j: (layer, 0, j)),
            pl.BlockSpec((None, D, tn), lambda i, j: (layer, 0, j + nc)),
            col((FFN_CONV_W, tn), 0), col((FFN_CONV_W, tn), nc),
            col((1, tn), 0), col((1, tn), nc),
            pl.BlockSpec((2, tm, LANES), lambda i, j: (0, i, 0)),
        ],
        out_specs=pl.BlockSpec((tm, tn), lambda i, j: (i, j)),
        out_shape=jax.ShapeDtypeStruct((R, F), BF16),
        scratch_shapes=[pltpu.VMEM((tm + 2 * HALO, D), BF16), pltpu.VMEM((tm + 2 * HALO, tn), F32),
                        pltpu.VMEM((tm + 2 * HALO, tn), F32)],
        compiler_params=_cparams("parallel", "arbitrary"),
        name="ffn_up_conv_gate",
    )(X, X, X, mod3, mod3, g, W, W, cw, cw, cb, cb, edge)


def _mnr_kernel(a_ref, w_ref, xa_ref, *refs, n_first):
    xb_ref, (gate_ref, g_ref, o_ref, acc) = (refs[0], refs[1:]) if len(refs) == 5 else (None, refs)
    k = pl.program_id(1)

    @pl.when(k == 0)
    def _():
        acc[...] = jnp.zeros_like(acc)

    acc[...] += jnp.dot(a_ref[...], w_ref[...], preferred_element_type=F32)

    @pl.when(k == pl.num_programs(1) - 1)
    def _():
        y = acc[...]
        ms = jnp.mean(y * y, axis=-1, keepdims=True)
        r = y * lax.rsqrt(ms + NORM_EPS) * g_ref[...]
        x = xa_ref[...]
        if xb_ref is not None:
            x = jnp.where(pl.program_id(0) < n_first, x, xb_ref[...])
        o_ref[...] = x + gate_ref[0] * r


def _mnr_call(A, W, Xa, Xb, mod3, g, *, layer, R, B, N, M, gate_part, tk, name):
    K, D = W.shape[1:]
    tm = _pick((N, B * M), 512, 256)
    n_first = min(Xa.shape[0], R) // tm
    xb_specs = [] if Xb is None else [pl.BlockSpec((tm, D), lambda i, k: (jnp.maximum(i - n_first, 0), 0))]
    grp = lambda i: jnp.minimum((i * tm) // N, B)
    return pl.pallas_call(
        functools.partial(_mnr_kernel, n_first=n_first),
        grid=(R // tm, K // tk),
        in_specs=[
            pl.BlockSpec((tm, tk), lambda i, k: (i, k)),
            pl.BlockSpec((None, tk, D), lambda i, k: (layer, k, 0)),
            pl.BlockSpec((tm, D), lambda i, k: (jnp.minimum(i, n_first - 1), 0)),
        ] + xb_specs + [
            pl.BlockSpec((1, 1, D), lambda i, k: (grp(i), 0, gate_part)),
            pl.BlockSpec((1, D), lambda i, k: (0, 0)),
        ],
        out_specs=pl.BlockSpec((tm, D), lambda i, k: (i, 0)),
        out_shape=jax.ShapeDtypeStruct((R, D), F32),
        scratch_shapes=[pltpu.VMEM((tm, D), F32)],
        compiler_params=_cparams("parallel", "arbitrary"),
        name=name,
    )(A, W, Xa, *([] if Xb is None else [Xb]), mod3, g)


def _diff_kernel(lam_ref, nw_ref, q_ref, *refs, tq, tk, with_lat, lam_init):
    if with_lat:
        kl_ref, kc_ref, vl_ref, vc_ref, o_ref, vt_scr, m_scr, acc_scr = refs
        N = kl_ref.shape[0]
    else:
        kc_ref, vc_ref, o_ref, vt_scr, m_scr, acc_scr = refs
        N = 0
    M = kc_ref.shape[0]
    hd = 2 * DIFF_DH
    n_lat_k = N // tk

    @pl.when(pl.program_id(2) == 0)
    def _():
        for c in range(n_lat_k):
            vt_scr[0:hd, c * tk:(c + 1) * tk] = vl_ref[c * tk:(c + 1) * tk, :].astype(F32).T.astype(BF16)
        vt_scr[0:hd, N:N + M] = vc_ref[...].astype(F32).T.astype(BF16)
        vt_scr[hd:, :] = jnp.ones((vt_scr.shape[0] - hd, N + M), BF16)

    q = q_ref[...]
    lane = lax.broadcasted_iota(jnp.int32, q.shape, 1)
    zero = jnp.zeros_like(q)
    qs = jnp.concatenate([jnp.where(lane < DIFF_DH, q, zero), jnp.where(lane >= DIFF_DH, q, zero)], axis=0)

    m_scr[...] = jnp.full_like(m_scr, NEG_BIG)
    acc_scr[...] = jnp.zeros_like(acc_scr)

    qcols = min(QCOLS, 2 * tq)
    chunks = [(c * tk, tk) for c in range(n_lat_k)] + [(N, M)]
    items = [(k0, kn, t * qcols) for (k0, kn) in chunks for t in range(2 * tq // qcols)]

    def keys(k0, kn):
        return kl_ref[k0:k0 + kn, :] if k0 < N else kc_ref[...]

    def scores(k0, kn, c0):
        return lax.dot_general(keys(k0, kn), qs[c0:c0 + qcols], (((1,), (1,)), ((), ())),
                               preferred_element_type=F32)

    def accumulate(s, k0, kn, c0):
        cols = slice(c0, c0 + qcols)
        m_prev = m_scr[:, cols]
        m_new = jnp.maximum(m_prev, jnp.max(s, axis=0, keepdims=True))
        alpha = jnp.exp2(m_prev - m_new)
        p = jnp.exp2(s - m_new).astype(BF16)
        acc_scr[:, cols] = alpha * acc_scr[:, cols] + jnp.dot(vt_scr[:, k0:k0 + kn], p, preferred_element_type=F32)
        m_scr[:, cols] = m_new

    ahead = min(DIFF_LOOKAHEAD, len(items))
    pending = [scores(*it) for it in items[:ahead]]
    for n, item in enumerate(items):
        if n + ahead < len(items):
            pending.append(scores(*items[n + ahead]))
        accumulate(pending.pop(0), *item)

    lp = lam_ref[...]
    lam = (jnp.exp(jnp.sum(lp[0:1] * lp[1:2], axis=1, keepdims=True))
           - jnp.exp(jnp.sum(lp[2:3] * lp[3:4], axis=1, keepdims=True)) + lam_init)
    acc = acc_scr[...]
    o = acc[:hd, :tq] / acc[hd:hd + 1, :tq] - lam * (acc[:hd, tq:] / acc[hd:hd + 1, tq:])
    ms = jnp.mean(o * o, axis=0, keepdims=True)
    r = o * (lax.rsqrt(ms + NORM_EPS) * (1.0 - lam_init))
    o_ref[...] = (r.T * nw_ref[...]).astype(BF16)


def _diff_call(P, lam_p, nw, *, B, N, M, with_lat, lam_init):
    tk = _pick((N,), 512, 256)
    hw = 2 * DIFF_DH
    cq, ck, cv = COL_DQ // hw, COL_DK // hw, COL_DV // hw
    ctx_blk = B * N // M
    small = lambda: pl.BlockSpec((4, DIFF_DH), lambda b, h, qi: (0, 0))
    lat = lambda col: pl.BlockSpec((N, hw), lambda b, h, qi: (b, col + h))
    ctx = lambda col: pl.BlockSpec((M, hw), lambda b, h, qi: (ctx_blk + b, col + h))
    if with_lat:
        tq = _pick((N,), 2048, 1024, 512, 256)
        nq, q0, keys = N // tq, 0, N + M
        kv_specs = [lat(ck), ctx(ck), lat(cv), ctx(cv)]
    else:
        tq = M
        nq, q0, keys = 1, B * N // tq, M
        kv_specs = [ctx(ck), ctx(cv)]
    kern = functools.partial(_diff_kernel, tq=tq, tk=tk, with_lat=with_lat, lam_init=lam_init)
    return pl.pallas_call(
        kern,
        grid=(B, N_DIFF_HEADS, nq),
        in_specs=[
            small(),
            pl.BlockSpec((1, hw), lambda b, h, qi: (0, 0)),
            pl.BlockSpec((tq, hw), lambda b, h, qi: (q0 + b * nq + qi, cq + h)),
        ] + kv_specs,
        out_specs=pl.BlockSpec((tq, hw), lambda b, h, qi: (b * nq + qi, h)),
        out_shape=jax.ShapeDtypeStruct((B * nq * tq, DIFF_WIDTH), BF16),
        scratch_shapes=[pltpu.VMEM((hw + HALO, keys), BF16), pltpu.VMEM((1, 2 * tq), F32),
                        pltpu.VMEM((hw + HALO, 2 * tq), F32)],
        compiler_params=_cparams("parallel", "parallel", "arbitrary"),
        name="diff_attn" if with_lat else "diff_attn_ctx",
    )(lam_p, nw, P, *([P] * len(kv_specs)))


def _win_kernel(sink_ref, q_ref, kp_ref, kc_ref, kn_ref, kx_ref, o_ref, *, nb, N, M):
    i = pl.program_id(1)
    wb, kw = WIN_BLOCK, WIN_KV_WIDTH
    q = q_ref[...]
    kcat = jnp.concatenate([kp_ref[:, :kw], kc_ref[:, :kw], kn_ref[:, :kw], kx_ref[:, :kw]], axis=0)
    vcat = jnp.concatenate([kp_ref[:, kw:], kc_ref[:, kw:], kn_ref[:, kw:], kx_ref[:, kw:]], axis=0)
    nk = 3 * wb + M
    rows = WIN_G * wb
    row = lax.broadcasted_iota(jnp.int32, (rows, nk), 0) % wb
    col = lax.broadcasted_iota(jnp.int32, (rows, nk), 1)
    kpos = (i - 1) * wb + col
    n_band = jnp.where(i < nb, N, 0)
    band_ok = (jnp.abs(col - wb - row) <= WINDOW) & (kpos >= 0) & (kpos < n_band)
    valid = band_ok | (col >= 3 * wb)
    lane = lax.broadcasted_iota(jnp.int32, (wb, kw), 1) // WIN_DH
    lane_o = lax.broadcasted_iota(jnp.int32, (rows, kw), 1) // WIN_DH
    grow = lax.broadcasted_iota(jnp.int32, (rows, 1), 0) // wb
    zero = jnp.zeros((wb, kw), BF16)
    acc = jnp.zeros((rows, kw), F32)

    def scores(kv):
        qs = jnp.concatenate([jnp.where(lane == kv, q[:, g * kw:(g + 1) * kw], zero) for g in range(WIN_G)], axis=0)
        return lax.dot_general(qs, kcat, (((1,), (1,)), ((), ())), preferred_element_type=F32)

    pending = [scores(kv) for kv in range(WIN_LOOKAHEAD)]
    for kv in range(WIN_KV_HEADS):
        if kv + WIN_LOOKAHEAD < WIN_KV_HEADS:
            pending.append(scores(kv + WIN_LOOKAHEAD))
        s = jnp.where(valid, pending.pop(0), NEG_BIG)
        sink = jnp.zeros((rows, 1), F32)
        for g in range(WIN_G):
            sink = jnp.where(grow == g, sink_ref[kv * WIN_G + g] * LOG2E, sink)
        m = jnp.maximum(jnp.max(s, axis=1, keepdims=True), sink)
        p = jnp.exp2(s - m)
        l = jnp.sum(p, axis=1, keepdims=True) + jnp.exp2(sink - m)
        o = jnp.dot(p.astype(BF16), vcat, preferred_element_type=F32) / l
        acc = jnp.where(lane_o == kv, o, acc)
    for g in range(WIN_G):
        o_ref[:, g * kw:(g + 1) * kw] = acc[g * wb:(g + 1) * wb].astype(BF16)


def _win_call(P, sink, *, B, N, M, need_ctx):
    wb = WIN_BLOCK
    nb, nbc = N // wb, (M // wb if need_ctx else 0)
    R = B * N + (B * M if need_ctx else 0)
    cq, ckv = COL_WQ // PW, COL_WKV // (2 * WIN_KV_WIDTH)

    def qrow(b, i):
        return jnp.where(i < nb, b * nb + i, B * nb + b * nbc + (i - nb))

    def band(b, i, d):
        return b * nb + jnp.clip(i + d, 0, nb - 1)

    kvw = 2 * WIN_KV_WIDTH
    kern = functools.partial(_win_kernel, nb=nb, N=N, M=M)
    grid_spec = pltpu.PrefetchScalarGridSpec(
        num_scalar_prefetch=1,
        grid=(B, nb + nbc),
        in_specs=[
            pl.BlockSpec((wb, PW), lambda b, i, s: (qrow(b, i), cq)),
            pl.BlockSpec((wb, kvw), lambda b, i, s: (band(b, i, -1), ckv)),
            pl.BlockSpec((wb, kvw), lambda b, i, s: (band(b, i, 0), ckv)),
            pl.BlockSpec((wb, kvw), lambda b, i, s: (band(b, i, 1), ckv)),
            pl.BlockSpec((M, kvw), lambda b, i, s: (B * N // M + b, ckv)),
        ],
        out_specs=pl.BlockSpec((wb, PW), lambda b, i, s: (qrow(b, i), 0)),
    )
    return pl.pallas_call(
        kern,
        grid_spec=grid_spec,
        out_shape=jax.ShapeDtypeStruct((R, WIN_WIDTH), BF16),
        compiler_params=_cparams("parallel", "arbitrary"),
        name="win_attn",
    )(sink, P, P, P, P, P)


def _seq_edges(i, R_lat, N, M):
    r0 = i * CONV_ROWS
    lat = r0 < R_lat
    first = jnp.where(lat, r0 % N == 0, (r0 - R_lat) % M == 0)
    last = jnp.where(lat, (r0 + CONV_ROWS) % N == 0, (r0 + CONV_ROWS - R_lat) % M == 0)
    return first, last


def _conv_taps(buf, main_ref, prev_ref, next_ref, w_ref, b_ref, first, last, k):
    buf[0:HALO] = jnp.where(first, 0.0, prev_ref[...].astype(F32))
    buf[HALO:HALO + CONV_ROWS] = main_ref[...].astype(F32)
    buf[HALO + CONV_ROWS:] = jnp.where(last, 0.0, next_ref[...].astype(F32))
    out = b_ref[...]
    for j in range(k):
        out = out + buf[pl.ds(HALO - k // 2 + j, CONV_ROWS), :] * w_ref[j:j + 1, :]
    return out


def _ssdconv_kernel(m_ref, p_ref, n_ref, w_ref, b_ref, o_ref, buf, *, R_lat, N, M):
    first, last = _seq_edges(pl.program_id(0), R_lat, N, M)
    o_ref[...] = _silu(_conv_taps(buf, m_ref, p_ref, n_ref, w_ref, b_ref, first, last, SSD_CONV_W)).astype(BF16)


def _ssdconv_call(P, w, b, *, B, N, M):
    T = P.shape[0]
    cr, W = CONV_ROWS, SSD_XBC
    hb = cr // HALO
    cb = COL_XBC // W
    nh = T // HALO
    kern = functools.partial(_ssdconv_kernel, R_lat=B * N, N=N, M=M)
    return pl.pallas_call(
        kern,
        grid=(T // cr,),
        in_specs=[
            pl.BlockSpec((cr, W), lambda i: (i, cb)),
            pl.BlockSpec((HALO, W), lambda i: (jnp.maximum(i * hb - 1, 0), cb)),
            pl.BlockSpec((HALO, W), lambda i: (jnp.minimum((i + 1) * hb, nh - 1), cb)),
            pl.BlockSpec((SSD_CONV_W, W), lambda i: (0, 0)),
            pl.BlockSpec((1, W), lambda i: (0, 0)),
        ],
        out_specs=pl.BlockSpec((cr, W), lambda i: (i, 0)),
        out_shape=jax.ShapeDtypeStruct((T, W), BF16),
        scratch_shapes=[pltpu.VMEM((cr + 2 * HALO, W), F32)],
        compiler_params=_cparams("parallel"),
        name="ssd_conv",
    )(P, P, P, w, b)


def _cumsum_rows(a):
    n = a.shape[0]
    tri = (lax.broadcasted_iota(jnp.int32, (n, n), 0) >= lax.broadcasted_iota(jnp.int32, (n, n), 1)).astype(BF16)
    hi = a.astype(BF16)
    r1 = a - hi.astype(F32)
    mid = r1.astype(BF16)
    lo = (r1 - mid.astype(F32)).astype(BF16)
    dot = lambda v: jnp.dot(tri, v, preferred_element_type=F32)
    return dot(hi) + dot(mid) + dot(lo)


def _ssd_prep(dt_ref, bias_ref, alog_ref, rows):
    x = dt_ref[rows, :].astype(F32) + bias_ref[...]
    dt = jnp.maximum(x, 0.0) + jnp.log(1.0 + jnp.exp(-jnp.abs(x)))
    ad = dt * (-jnp.exp(alog_ref[...]))
    return dt, ad, _cumsum_rows(ad)


def _ssd_scales(prep, e_ref, *, reverse):
    dt, ad, acum = prep
    ch = SSD_CHUNK
    total = acum[ch - 1:ch, :]
    if reverse:
        cvec = acum - ad
        e_off, e_ws = jnp.exp(total - cvec), jnp.exp(cvec)
    else:
        e_off, e_ws = jnp.exp(acum), jnp.exp(total - acum)
    tot_rows = jnp.broadcast_to(jnp.exp(total), (HALO, LANES))
    return jnp.dot(jnp.concatenate([dt * e_ws, e_off, tot_rows], axis=0).astype(BF16), e_ref[...],
                   preferred_element_type=F32)


def _ssd_group_refs(g, u_ref, rows):
    hpg = SSD_HEADS // SSD_GROUPS
    lanes = slice(g * hpg * SSD_HEAD_DIM, (g + 1) * hpg * SSD_HEAD_DIM)
    bg = u_ref[rows, SSD_INNER + g * SSD_STATE:SSD_INNER + (g + 1) * SSD_STATE]
    cg = u_ref[rows, SSD_INNER + (SSD_GROUPS + g) * SSD_STATE:SSD_INNER + (SSD_GROUPS + g + 1) * SSD_STATE]
    return lanes, bg, cg, u_ref[rows, lanes]


def _ssd_state_group(g, scales, u_ref, s_ref, rows):
    ch = SSD_CHUNK
    lanes, bg, cg, xs = _ssd_group_refs(g, u_ref, rows)
    sg = s_ref[:, lanes]
    yoff = jnp.dot(cg, sg.astype(BF16), preferred_element_type=F32) * scales[ch:2 * ch, lanes]
    xw = (xs.astype(F32) * scales[0:ch, lanes]).astype(BF16)
    s_ref[:, lanes] = (sg * scales[2 * ch:2 * ch + 1, lanes]
                       + lax.dot_general(bg, xw, (((0,), (0,)), ((), ())), preferred_element_type=F32))
    return yoff


def _ssd_diag_group(g, decay, u_ref, rows):
    cf, cfT, cb_, cbT, dtT = decay
    ch, hp = SSD_CHUNK, SSD_HEAD_DIM
    hpg = SSD_HEADS // SSD_GROUPS
    li = lax.broadcasted_iota(jnp.int32, (ch, ch), 0)
    si = lax.broadcasted_iota(jnp.int32, (ch, ch), 1)
    hsel = lax.broadcasted_iota(jnp.int32, (ch, hpg * hp), 1) // hp
    _, bg, cg, xs = _ssd_group_refs(g, u_ref, rows)
    cb = lax.dot_general(cg, bg, (((1,), (1,)), ((), ())), preferred_element_type=F32)
    ydiag = jnp.zeros((ch, hpg * hp), F32)
    for r in range(hpg):
        jf = g * hpg + r
        jb = SSD_HEADS + jf
        bc = lambda v: jnp.broadcast_to(v, (ch, ch))
        fwd = jnp.exp(jnp.where(li >= si, bc(cf[:, jf:jf + 1]) - bc(cfT[jf:jf + 1, :]), NEG_BIG)) * dtT[jf:jf + 1, :]
        bwd = jnp.exp(jnp.where(si >= li, bc(cbT[jb:jb + 1, :]) - bc(cb_[:, jb:jb + 1]), NEG_BIG)) * dtT[jb:jb + 1, :]
        yh = jnp.dot((cb * (fwd + bwd)).astype(BF16), xs, preferred_element_type=F32)
        ydiag = jnp.where(hsel == r, yh, ydiag)
    return ydiag


def _ssd_kernel(uf_ref, dtf_ref, ub_ref, dtb_ref, bias_ref, alog_ref, e_ref, yf_ref, yb_ref, sf, sb):
    @pl.when(pl.program_id(1) == 0)
    def _():
        sf[...] = jnp.zeros_like(sf)
        sb[...] = jnp.zeros_like(sb)

    ch, n = SSD_CHUNK, SSD_STEP_CHUNKS
    f_rows = [slice(k * ch, (k + 1) * ch) for k in range(n)]
    b_rows = f_rows[::-1]
    for k in range(n):
        pf = _ssd_prep(dtf_ref, bias_ref, alog_ref, f_rows[k])
        pb = _ssd_prep(dtb_ref, bias_ref, alog_ref, b_rows[k])
        sc_f = _ssd_scales(pf, e_ref.at[0], reverse=False)
        sc_b = _ssd_scales(pb, e_ref.at[1], reverse=True)
        dt, ad, acum = pf
        excl = acum - ad
        decay = (acum, acum.T, excl, excl.T, dt.T)
        for g in range(SSD_GROUPS):
            lanes = _ssd_group_refs(g, uf_ref, f_rows[k])[0]
            yoff_f = _ssd_state_group(g, sc_f, uf_ref, sf, f_rows[k])
            yoff_b = _ssd_state_group(g, sc_b, ub_ref, sb, b_rows[k])
            yb_ref[b_rows[k], lanes] = yoff_b.astype(BF16)
            yf_ref[f_rows[k], lanes] = (_ssd_diag_group(g, decay, uf_ref, f_rows[k]) + yoff_f).astype(BF16)


def _ssd_call(U, P, bias, alog, spread, *, B, N, M):
    T = U.shape[0]
    ch = SSD_CHUNK * SSD_STEP_CHUNKS
    ncl, ncc = N // ch, M // ch
    cdt = COL_DT // LANES

    def fwd(b, t):
        return jnp.where(t < ncc, B * ncl + b * ncc + t, b * ncl + (t - ncc))

    def bwd(b, t):
        return jnp.where(t < ncc, B * ncl + b * ncc + (ncc - 1 - t), b * ncl + (ncl - 1 - (t - ncc)))

    return pl.pallas_call(
        _ssd_kernel,
        grid=(B, ncl + ncc),
        in_specs=[
            pl.BlockSpec((ch, SSD_XBC), lambda b, t: (fwd(b, t), 0)),
            pl.BlockSpec((ch, LANES), lambda b, t: (fwd(b, t), cdt)),
            pl.BlockSpec((ch, SSD_XBC), lambda b, t: (bwd(b, t), 0)),
            pl.BlockSpec((ch, LANES), lambda b, t: (bwd(b, t), cdt)),
            pl.BlockSpec((1, LANES), lambda b, t: (0, 0)),
            pl.BlockSpec((1, LANES), lambda b, t: (0, 0)),
            pl.BlockSpec((2, LANES, SSD_INNER), lambda b, t: (0, 0, 0)),
        ],
        out_specs=[
            pl.BlockSpec((ch, SSD_INNER), lambda b, t: (fwd(b, t), 0)),
            pl.BlockSpec((ch, SSD_INNER), lambda b, t: (bwd(b, t), 0)),
        ],
        out_shape=[jax.ShapeDtypeStruct((T, SSD_INNER), BF16)] * 2,
        scratch_shapes=[pltpu.VMEM((SSD_STATE, SSD_INNER), F32)] * 2,
        compiler_params=_cparams("parallel", "arbitrary"),
        name="ssd_scan",
    )(U, P, U, P, bias, alog, spread)


def _merge_kernel(yd_ref, yf_ref, yb_ref, xs_ref, z_ref, yw_ref, gd_ref, gs_ref, gw_ref,
                  wd_ref, ws_ref, ww_ref, dsk_ref, sn_ref, o_ref):
    y = yf_ref[...].astype(F32) + yb_ref[...].astype(F32) + dsk_ref[...] * xs_ref[...].astype(F32)
    y = y * _silu(z_ref[...].astype(F32))
    ms = jnp.mean(y * y, axis=-1, keepdims=True)
    s = (y * lax.rsqrt(ms + NORM_EPS) * sn_ref[...]).astype(BF16)
    dot = lambda a, w: jnp.dot(a, w[...], preferred_element_type=F32)
    sig = lambda r: _sigmoid(r[...].astype(F32))
    acc = sig(gd_ref) * dot(yd_ref[...], wd_ref)
    acc = acc + sig(gs_ref) * dot(s, ws_ref)
    acc = acc + sig(gw_ref) * dot(yw_ref[...], ww_ref)
    o_ref[...] = acc.astype(BF16)


def _merge_call(YD, YF, YB, U, P, YW, wd, ws, ww, dsk, sn, *, layer, R, D):
    tm = 256
    W = SSD_INNER
    cg = COL_GL // D
    row = lambda c: pl.BlockSpec((tm, W), lambda i: (i, c))
    gate = lambda c: pl.BlockSpec((tm, D), lambda i: (i, cg + c))
    wspec = pl.BlockSpec((None, W, D), lambda i: (layer, 0, 0))
    vec = pl.BlockSpec((1, W), lambda i: (0, 0))
    return pl.pallas_call(
        _merge_kernel,
        grid=(R // tm,),
        in_specs=[row(0), row(0), row(0), row(0), row(COL_SZ // W), row(0),
                  gate(0), gate(1), gate(2), wspec, wspec, wspec, vec, vec],
        out_specs=pl.BlockSpec((tm, D), lambda i: (i, 0)),
        out_shape=jax.ShapeDtypeStruct((R, D), BF16),
        compiler_params=_cparams("parallel"),
        name="branch_merge",
    )(YD, YF, YB, U, P, YW, P, P, P, wd, ws, ww, dsk, sn)


def _rope_tables(N, rows_ctx):
    hd = DIFF_DH
    pos = jnp.arange(N)
    row = (pos // GRID_W).astype(F32)
    colp = (pos % GRID_W).astype(F32)
    axis_dim = hd // 2
    inv = ROPE_BASE ** (-jnp.arange(0, axis_dim, 2, dtype=F32) / axis_dim)
    ang = jnp.concatenate([row[:, None] * inv, colp[:, None] * inv], axis=-1)
    lane = jnp.arange(LANES)
    cos = jnp.cos(ang)[:, lane % (hd // 2)]
    sin = jnp.sin(ang)[:, lane % (hd // 2)]
    first = (lane % hd) < hd // 2
    lat = jnp.stack([cos, jnp.where(first, -sin, 0.0), jnp.where(first, 0.0, sin)])
    ctx = jnp.stack([jnp.ones((rows_ctx, LANES), F32), jnp.zeros((rows_ctx, LANES), F32),
                     jnp.zeros((rows_ctx, LANES), F32)])
    k_tab = jnp.concatenate([lat, ctx], axis=1)
    q_tab = k_tab * (DIFF_DH ** -0.5 * LOG2E)
    return jnp.stack([q_tab, k_tab])


def _edge_table(B, N, M):
    pos = jnp.concatenate([jnp.tile(jnp.arange(N), B), jnp.tile(jnp.arange(M), B)])
    last = jnp.concatenate([jnp.full((B * N,), N - 1), jnp.full((B * M,), M - 1)])
    t = jnp.stack([pos != 0, pos != last]).astype(F32)
    return jnp.broadcast_to(t[:, :, None], t.shape + (LANES,))


def _layout_w_in(w):
    L, D = w.shape[:2]
    sizes = (DIFF_WIDTH, DIFF_WIDTH, DIFF_WIDTH, SSD_INNER, SSD_XBC, 2 * SSD_HEADS,
             WIN_WIDTH, WIN_KV_WIDTH, WIN_KV_WIDTH, 3 * D)
    parts, s = [], 0
    for z in sizes:
        parts.append(w[..., s:s + z])
        s += z
    dq, dk, dv, sz, sxbc, sdt, wq, wk, wv, gl = parts
    wq = wq.reshape(L, D, WIN_KV_HEADS, WIN_G, WIN_DH).transpose(0, 1, 3, 2, 4).reshape(L, D, WIN_WIDTH)
    out = jnp.zeros((L, D, P_WIDTH), BF16)
    for off, piece in ((COL_DQ, dq), (COL_DK, dk), (COL_WQ, wq), (COL_WKV, wk), (COL_WKV + WIN_KV_WIDTH, wv),
                       (COL_DT, sdt), (COL_DV, dv), (COL_SZ, sz), (COL_XBC, sxbc), (COL_GL, gl)):
        out = lax.dynamic_update_slice(out, piece.astype(BF16), (0, 0, off))
    return out


def kernel(x, c, ctx, c_ctx, w_ada, b_ada, norm_g, w_in, diff_lambda, diff_norm, ssd_conv_w, ssd_conv_b,
           ssd_a_log, ssd_dt_bias, ssd_d, ssd_norm, win_sink, w_br_diff, w_br_ssd, w_br_win, w_out,
           ffn_w_up, ffn_conv_w, ffn_conv_b, ffn_w_down):
    B, N, D = x.shape
    M = ctx.shape[1]
    L = w_ada.shape[0]
    assert D == 2 * PW and P_WIDTH == COL_GL + 3 * D
    assert N % CONV_ROWS == 0 and M % CONV_ROWS == 0 and (B * N) % M == 0 and B + 1 <= 16
    assert CONV_ROWS % (SSD_CHUNK * SSD_STEP_CHUNKS) == 0
    T = B * N + B * M

    Xa, Xb = x.reshape(B * N, D), ctx.reshape(B * M, D)
    cv = jnp.zeros((16, D), F32).at[:B].set(c).at[B].set(c_ctx)
    mod = _mod_call(cv, w_ada, b_ada)

    tm_in = _pick((N, B * M), 1024, 512, 256)
    tab = _rope_tables(N, tm_in)
    kinds = jnp.asarray(COL_KINDS, jnp.int32)
    edge = _edge_table(B, N, M)
    head_of_lane = jnp.arange(SSD_INNER) // SSD_HEAD_DIM
    spread = jnp.stack([(jnp.arange(LANES)[:, None] == head_of_lane[None, :] + d * SSD_HEADS)
                        for d in range(2)]).astype(BF16)
    pad_lanes = LANES - 2 * SSD_HEADS

    W_in = _layout_w_in(w_in)
    W_bd, W_bs, W_out = w_br_diff.astype(BF16), w_br_ssd.astype(BF16), w_out.astype(BF16)
    W_bw = (w_br_win.reshape(L, WIN_KV_HEADS, WIN_G, WIN_DH, D).transpose(0, 2, 1, 3, 4)
            .reshape(L, WIN_WIDTH, D).astype(BF16))
    W_up, W_down = ffn_w_up.astype(BF16), ffn_w_down.astype(BF16)

    for l in range(L):
        need_ctx = l < L - 1
        lam_init = 0.8 - 0.6 * math.exp(-0.3 * l)
        R = T if need_ctx else B * N
        mod3 = mod[l].reshape(16, 1, 6 * D)
        kw0 = dict(B=B, N=N, M=M)
        kw = dict(layer=l, **kw0)

        P = _inproj_call(Xa, Xb, mod3, norm_g[l, 0:1], W_in, tab, kinds, sc_part=1, sh_part=0, **kw)

        dargs = (P, diff_lambda[l], diff_norm[l].reshape(1, -1))
        YD = _diff_call(*dargs, with_lat=True, lam_init=lam_init, **kw0)
        if need_ctx:
            YD = jnp.concatenate([YD, _diff_call(*dargs, with_lat=False, lam_init=lam_init, **kw0)], axis=0)
        YW = _win_call(P, win_sink[l], need_ctx=need_ctx, **kw0)

        U = _ssdconv_call(P, ssd_conv_w[l], ssd_conv_b[l].reshape(1, -1), **kw0)
        bias = jnp.pad(ssd_dt_bias[l].reshape(1, -1), ((0, 0), (0, pad_lanes)))
        alog = jnp.pad(ssd_a_log[l].reshape(1, -1), ((0, 0), (0, pad_lanes)))
        YF, YB = _ssd_call(U, P, bias, alog, spread, **kw0)

        dsk = jnp.repeat(ssd_d[l], SSD_HEAD_DIM).reshape(1, -1)
        Z = _merge_call(YD, YF, YB, U, P, YW, W_bd, W_bs, W_bw, dsk, ssd_norm[l].reshape(1, -1), layer=l, R=R, D=D)
        X1 = _mnr_call(Z, W_out, Xa, Xb if l == 0 else None, mod3, norm_g[l, 1:2], R=R, gate_part=2, tk=W_out.shape[1],
                       name="out_proj", **kw)

        A = _ffnup_call(X1, mod3, norm_g[l, 2:3], W_up, ffn_conv_w[l], ffn_conv_b[l].reshape(1, -1), edge,
                        R=R, sc_part=4, sh_part=3, **kw)
        Xa = Xb = _mnr_call(A, W_down, X1, None, mod3, norm_g[l, 3:4], R=R, gate_part=5,
                            tk=_pick((A.shape[1],), 2816, 512, 128), name="ffn_down", **kw)

    return Xa[:B * N].reshape(B, N, D)
```

```python
import functools
import math

import jax
import jax.numpy as jnp
from jax import lax
from jax.experimental import pallas as pl
from jax.experimental.pallas import tpu as pltpu

F32 = jnp.float32
BF16 = jnp.bfloat16

GRID_W = 64
ROPE_BASE = 10000.0
NORM_EPS = 1e-6
N_DIFF_HEADS = 8
DIFF_DH = 64
DIFF_WIDTH = N_DIFF_HEADS * 2 * DIFF_DH
SSD_INNER = 1024
SSD_HEAD_DIM = 64
SSD_HEADS = 16
SSD_GROUPS = 4
SSD_STATE = 128
SSD_CONV_W = 5
SSD_CHUNK = 128
SSD_XBC = SSD_INNER + 2 * SSD_GROUPS * SSD_STATE
WIN_HEADS = 16
WIN_KV_HEADS = 4
WIN_G = WIN_HEADS // WIN_KV_HEADS
WIN_DH = 64
WINDOW = 128
WIN_BLOCK = 128
WIN_WIDTH = WIN_HEADS * WIN_DH
WIN_KV_WIDTH = WIN_KV_HEADS * WIN_DH
FFN_CONV_W = 3

LOG2E = 1.4426950408889634
NEG_BIG = -1e30

LANES = 128
HALO = 16
QCOLS = 256
DIFF_LOOKAHEAD = 3
WIN_LOOKAHEAD = 2
SSD_STEP_CHUNKS = 2
CONV_ROWS = 256
VMEM_LIMIT = 56 * 1024 * 1024

PW = 1024
COL_DQ = 0 * PW
COL_DK = 1 * PW
COL_WQ = 2 * PW
COL_WKV = 3 * PW
COL_DT = COL_WKV + 2 * WIN_KV_WIDTH
COL_DV = 4 * PW
COL_SZ = 5 * PW
COL_XBC = 6 * PW
COL_GL = 8 * PW
P_WIDTH = 14 * PW
KIND_NONE, KIND_Q, KIND_K, KIND_K256 = 0, 1, 2, 3
COL_KINDS = (KIND_Q, KIND_K, KIND_Q, KIND_K256) + (KIND_NONE,) * 10


def _cparams(*sem):
    return pltpu.CompilerParams(dimension_semantics=sem, vmem_limit_bytes=VMEM_LIMIT)


def _sigmoid(x):
    return 1.0 / (1.0 + jnp.exp(-x))


def _silu(x):
    return x * _sigmoid(x)


def _pick(n, *cands):
    for c in cands:
        if all(v % c == 0 for v in n):
            return c
    raise ValueError(f"no block size among {cands} divides {n}")


def _mod_kernel(cv_ref, w_ref, b_ref, o_ref):
    s = _silu(cv_ref[...]).astype(BF16)
    o_ref[0] = jnp.dot(s, w_ref[0].astype(BF16), preferred_element_type=F32) + b_ref[0]


def _mod_call(cv, w_ada, b_ada):
    L, D, W6 = w_ada.shape
    tn = 1024
    return pl.pallas_call(
        _mod_kernel,
        grid=(L, W6 // tn),
        in_specs=[
            pl.BlockSpec((16, D), lambda l, j: (0, 0)),
            pl.BlockSpec((1, D, tn), lambda l, j: (l, 0, j)),
            pl.BlockSpec((1, 1, tn), lambda l, j: (l, 0, j)),
        ],
        out_specs=pl.BlockSpec((1, 16, tn), lambda l, j: (l, 0, j)),
        out_shape=jax.ShapeDtypeStruct((L, 16, W6), F32),
        compiler_params=_cparams("parallel", "parallel"),
        name="adaln_mod",
    )(cv, w_ada, b_ada.reshape(L, 1, W6))


def _norm_mod(x, g, sc, sh):
    ms = jnp.mean(x * x, axis=-1, keepdims=True)
    y = x * lax.rsqrt(ms + NORM_EPS) * g
    return y * (1.0 + sc) + sh


def _rope(y, cos, sa, sb):
    w = y.shape[1]
    rep = w // LANES
    if rep > 1:
        cos, sa, sb = (jnp.tile(t, (1, rep)) for t in (cos, sa, sb))
    up = pltpu.roll(y, w - DIFF_DH // 2, axis=1)
    dn = pltpu.roll(y, DIFF_DH // 2, axis=1)
    return y * cos + up * sa + dn * sb


def _inproj_kernel(kind_ref, xa_ref, xb_ref, sc_ref, sh_ref, g_ref, w_ref, tab_ref, o_ref, h_scr, *, n_first):
    j = pl.program_id(1)

    @pl.when(j == 0)
    def _():
        x = jnp.where(pl.program_id(0) < n_first, xa_ref[...], xb_ref[...])
        h_scr[...] = _norm_mod(x, g_ref[...], sc_ref[0], sh_ref[0]).astype(BF16)

    dot = lambda: jnp.dot(h_scr[...], w_ref[...], preferred_element_type=F32)
    kind = kind_ref[j]

    @pl.when(kind == KIND_NONE)
    def _():
        o_ref[...] = dot().astype(BF16)

    @pl.when(jnp.logical_or(kind == KIND_Q, kind == KIND_K))
    def _():
        o_ref[...] = _rope(dot(), tab_ref[0, 0], tab_ref[0, 1], tab_ref[0, 2]).astype(BF16)

    @pl.when(kind == KIND_K256)
    def _():
        kw = WIN_KV_WIDTH
        y = dot()
        o_ref[:, :kw] = _rope(y[:, :kw], tab_ref[0, 0], tab_ref[0, 1], tab_ref[0, 2]).astype(BF16)
        o_ref[:, kw:] = y[:, kw:].astype(BF16)


def _inproj_call(Xa, Xb, mod3, g, W, tab, kinds, *, layer, B, N, M, sc_part, sh_part):
    D = Xa.shape[1]
    T = B * N + B * M
    tm = _pick((N, B * M), 1024, 512, 256)
    n_first = min(Xa.shape[0], T) // tm
    nlat = N // tm
    grp = lambda i: jnp.minimum((i * tm) // N, B)
    rblk = lambda i: jnp.where(i < B * nlat, i % nlat, nlat)
    ksel = lambda k: jnp.where(jnp.logical_or(k == KIND_K, k == KIND_K256), 1, 0)
    grid_spec = pltpu.PrefetchScalarGridSpec(
        num_scalar_prefetch=1,
        grid=(T // tm, P_WIDTH // PW),
        in_specs=[
            pl.BlockSpec((tm, D), lambda i, j, kr: (jnp.minimum(i, n_first - 1), 0)),
            pl.BlockSpec((tm, D), lambda i, j, kr: (jnp.maximum(i - n_first, 0), 0), pipeline_mode=pl.Buffered(1)),
            pl.BlockSpec((1, 1, D), lambda i, j, kr: (grp(i), 0, sc_part)),
            pl.BlockSpec((1, 1, D), lambda i, j, kr: (grp(i), 0, sh_part)),
            pl.BlockSpec((1, D), lambda i, j, kr: (0, 0)),
            pl.BlockSpec((None, D, PW), lambda i, j, kr: (layer, 0, j)),
            pl.BlockSpec((1, 3, tm, LANES), lambda i, j, kr: (ksel(kr[j]), 0, rblk(i), 0)),
        ],
        out_specs=pl.BlockSpec((tm, PW), lambda i, j, kr: (i, j)),
        scratch_shapes=[pltpu.VMEM((tm, D), BF16)],
    )
    return pl.pallas_call(
        functools.partial(_inproj_kernel, n_first=n_first),
        grid_spec=grid_spec,
        out_shape=jax.ShapeDtypeStruct((T, P_WIDTH), BF16),
        compiler_params=_cparams("parallel", "arbitrary"),
        name="in_proj",
    )(kinds, Xa, Xb, mod3, mod3, g, W, tab)


def _ffnup_kernel(x_ref, xp_ref, xn_ref, sc_ref, sh_ref, g_ref, wa_ref, wg_ref, cwa_ref, cwg_ref, ba_ref, bg_ref,
                  edge_ref, o_ref, h_scr, bufa, bufg):
    tm = x_ref.shape[0]

    @pl.when(pl.program_id(1) == 0)
    def _():
        nm = lambda r: _norm_mod(r[...], g_ref[...], sc_ref[0], sh_ref[0]).astype(BF16)
        h_scr[0:HALO] = nm(xp_ref)
        h_scr[HALO:HALO + tm] = nm(x_ref)
        h_scr[HALO + tm:] = nm(xn_ref)

    rep = o_ref.shape[1] // LANES
    prev_ok = jnp.tile(edge_ref[0], (1, rep))
    next_ok = jnp.tile(edge_ref[1], (1, rep))

    def conv(buf, w_ref, cw_ref, b_ref):
        buf[...] = jnp.dot(h_scr[...], w_ref[...], preferred_element_type=F32)
        taps = [buf[pl.ds(HALO - 1 + j, tm), :] for j in range(FFN_CONV_W)]
        out = b_ref[...] + (taps[0] * prev_ok) * cw_ref[0:1, :]
        out = out + taps[1] * cw_ref[1:2, :]
        return out + (taps[2] * next_ok) * cw_ref[2:3, :]

    a = conv(bufa, wa_ref, cwa_ref, ba_ref)
    g = conv(bufg, wg_ref, cwg_ref, bg_ref)
    o_ref[...] = (_silu(a) * g).astype(BF16)


def _ffnup_call(X, mod3, g, W, cw, cb, edge, *, layer, R, B, N, M, sc_part, sh_part):
    D = X.shape[1]
    F = W.shape[2] // 2
    tm = _pick((N, B * M), 1024, 512, 256)
    tn = _pick((F,), 512, 128)
    nc = F // tn
    hb = tm // HALO
    nh = R // HALO
    grp = lambda i: jnp.minimum((i * tm) // N, B)
    col = lambda shape, off: pl.BlockSpec(shape, lambda i, j: (0, j + off))
    return pl.pallas_call(
        _ffnup_kernel,
        grid=(R // tm, nc),
        in_specs=[
            pl.BlockSpec((tm, D), lambda i, j: (i, 0)),
            pl.BlockSpec((HALO, D), lambda i, j: (jnp.maximum(i * hb - 1, 0), 0)),
            pl.BlockSpec((HALO, D), lambda i, j: (jnp.minimum((i + 1) * hb, nh - 1), 0)),
            pl.BlockSpec((1, 1, D), lambda i, j: (grp(i), 0, sc_part)),
            pl.BlockSpec((1, 1, D), lambda i, j: (grp(i), 0, sh_part)),
            pl.BlockSpec((1, D), lambda i, j: (0, 0)),
            pl.BlockSpec((None, D, tn), lambda i, j: (layer, 0, j)),
            pl.BlockSpec((None, D, tn), lambda i, j: (layer, 0, j + nc)),
            col((FFN_CONV_W, tn), 0), col((FFN_CONV_W, tn), nc),
            col((1, tn), 0), col((1, tn), nc),
            pl.BlockSpec((2, tm, LANES), lambda i, j: (0, i, 0)),
        ],
        out_specs=pl.BlockSpec((tm, tn), lambda i, j: (i, j)),
        out_shape=jax.ShapeDtypeStruct((R, F), BF16),
        scratch_shapes=[pltpu.VMEM((tm + 2 * HALO, D), BF16), pltpu.VMEM((tm + 2 * HALO, tn), F32),
                        pltpu.VMEM((tm + 2 * HALO, tn), F32)],
        compiler_params=_cparams("parallel", "arbitrary"),
        name="ffn_up_conv_gate",
    )(X, X, X, mod3, mod3, g, W, W, cw, cw, cb, cb, edge)


def _mnr_kernel(a_ref, w_ref, xa_ref, *refs, n_first):
    xb_ref, (gate_ref, g_ref, o_ref, acc) = (refs[0], refs[1:]) if len(refs) == 5 else (None, refs)
    k = pl.program_id(1)

    @pl.when(k == 0)
    def _():
        acc[...] = jnp.zeros_like(acc)

    acc[...] += jnp.dot(a_ref[...], w_ref[...], preferred_element_type=F32)

    @pl.when(k == pl.num_programs(1) - 1)
    def _():
        y = acc[...]
        ms = jnp.mean(y * y, axis=-1, keepdims=True)
        r = y * lax.rsqrt(ms + NORM_EPS) * g_ref[...]
        x = xa_ref[...]
        if xb_ref is not None:
            x = jnp.where(pl.program_id(0) < n_first, x, xb_ref[...])
        o_ref[...] = x + gate_ref[0] * r


def _mnr_call(A, W, Xa, Xb, mod3, g, *, layer, R, B, N, M, gate_part, tk, tm, name):
    K, D = W.shape[1:]
    w_mode = dict(pipeline_mode=pl.Buffered(1)) if tk == K else {}
    n_first = min(Xa.shape[0], R) // tm
    xb_specs = [] if Xb is None else [pl.BlockSpec((tm, D), lambda i, k: (jnp.maximum(i - n_first, 0), 0))]
    grp = lambda i: jnp.minimum((i * tm) // N, B)
    return pl.pallas_call(
        functools.partial(_mnr_kernel, n_first=n_first),
        grid=(R // tm, K // tk),
        in_specs=[
            pl.BlockSpec((tm, tk), lambda i, k: (i, k)),
            pl.BlockSpec((None, tk, D), lambda i, k: (layer, k, 0), **w_mode),
            pl.BlockSpec((tm, D), lambda i, k: (jnp.minimum(i, n_first - 1), 0)),
        ] + xb_specs + [
            pl.BlockSpec((1, 1, D), lambda i, k: (grp(i), 0, gate_part)),
            pl.BlockSpec((1, D), lambda i, k: (0, 0)),
        ],
        out_specs=pl.BlockSpec((tm, D), lambda i, k: (i, 0)),
        out_shape=jax.ShapeDtypeStruct((R, D), F32),
        scratch_shapes=[pltpu.VMEM((tm, D), F32)],
        compiler_params=_cparams("parallel", "arbitrary"),
        name=name,
    )(A, W, Xa, *([] if Xb is None else [Xb]), mod3, g)


def _diff_kernel(lam_ref, nw_ref, q_ref, *refs, tq, tk, with_lat, lam_init):
    if with_lat:
        kl_ref, kc_ref, vl_ref, vc_ref, o_ref, vt_scr, m_scr, acc_scr = refs
        N = kl_ref.shape[0]
    else:
        kc_ref, vc_ref, o_ref, vt_scr, m_scr, acc_scr = refs
        N = 0
    M = kc_ref.shape[0]
    hd = 2 * DIFF_DH
    n_lat_k = N // tk

    @pl.when(pl.program_id(2) == 0)
    def _():
        for c in range(n_lat_k):
            vt_scr[0:hd, c * tk:(c + 1) * tk] = vl_ref[c * tk:(c + 1) * tk, :].astype(F32).T.astype(BF16)
        vt_scr[0:hd, N:N + M] = vc_ref[...].astype(F32).T.astype(BF16)
        vt_scr[hd:, :] = jnp.ones((vt_scr.shape[0] - hd, N + M), BF16)

    q = q_ref[...]
    lane = lax.broadcasted_iota(jnp.int32, q.shape, 1)
    zero = jnp.zeros_like(q)
    qs = jnp.concatenate([jnp.where(lane < DIFF_DH, q, zero), jnp.where(lane >= DIFF_DH, q, zero)], axis=0)

    m_scr[...] = jnp.full_like(m_scr, NEG_BIG)
    acc_scr[...] = jnp.zeros_like(acc_scr)

    qcols = min(QCOLS, 2 * tq)
    chunks = [(c * tk, tk) for c in range(n_lat_k)] + [(N, M)]
    items = [(k0, kn, t * qcols) for (k0, kn) in chunks for t in range(2 * tq // qcols)]

    def keys(k0, kn):
        return kl_ref[k0:k0 + kn, :] if k0 < N else kc_ref[...]

    def scores(k0, kn, c0):
        return lax.dot_general(keys(k0, kn), qs[c0:c0 + qcols], (((1,), (1,)), ((), ())),
                               preferred_element_type=F32)

    def accumulate(s, k0, kn, c0):
        cols = slice(c0, c0 + qcols)
        m_prev = m_scr[:, cols]
        m_new = jnp.maximum(m_prev, jnp.max(s, axis=0, keepdims=True))
        alpha = jnp.exp2(m_prev - m_new)
        p = jnp.exp2(s - m_new).astype(BF16)
        acc_scr[:, cols] = alpha * acc_scr[:, cols] + jnp.dot(vt_scr[:, k0:k0 + kn], p, preferred_element_type=F32)
        m_scr[:, cols] = m_new

    ahead = min(DIFF_LOOKAHEAD, len(items))
    pending = [scores(*it) for it in items[:ahead]]
    for n, item in enumerate(items):
        if n + ahead < len(items):
            pending.append(scores(*items[n + ahead]))
        accumulate(pending.pop(0), *item)

    lp = lam_ref[...]
    lam = (jnp.exp(jnp.sum(lp[0:1] * lp[1:2], axis=1, keepdims=True))
           - jnp.exp(jnp.sum(lp[2:3] * lp[3:4], axis=1, keepdims=True)) + lam_init)
    acc = acc_scr[...]
    o = acc[:hd, :tq] / acc[hd:hd + 1, :tq] - lam * (acc[:hd, tq:] / acc[hd:hd + 1, tq:])
    ms = jnp.mean(o * o, axis=0, keepdims=True)
    r = o * (lax.rsqrt(ms + NORM_EPS) * (1.0 - lam_init))
    o_ref[...] = (r.T * nw_ref[...]).astype(BF16)


def _diff_call(P, lam_p, nw, *, B, N, M, with_lat, lam_init):
    tk = _pick((N,), 512, 256)
    hw = 2 * DIFF_DH
    cq, ck, cv = COL_DQ // hw, COL_DK // hw, COL_DV // hw
    ctx_blk = B * N // M
    small = lambda: pl.BlockSpec((4, DIFF_DH), lambda b, h, qi: (0, 0))
    lat = lambda col: pl.BlockSpec((N, hw), lambda b, h, qi: (b, col + h))
    ctx = lambda col: pl.BlockSpec((M, hw), lambda b, h, qi: (ctx_blk + b, col + h))
    if with_lat:
        tq = _pick((N,), 2048, 1024, 512, 256)
        nq, q0, keys = N // tq, 0, N + M
        kv_specs = [lat(ck), ctx(ck), lat(cv), ctx(cv)]
    else:
        tq = M
        nq, q0, keys = 1, B * N // tq, M
        kv_specs = [ctx(ck), ctx(cv)]
    kern = functools.partial(_diff_kernel, tq=tq, tk=tk, with_lat=with_lat, lam_init=lam_init)
    return pl.pallas_call(
        kern,
        grid=(B, N_DIFF_HEADS, nq),
        in_specs=[
            small(),
            pl.BlockSpec((1, hw), lambda b, h, qi: (0, 0)),
            pl.BlockSpec((tq, hw), lambda b, h, qi: (q0 + b * nq + qi, cq + h)),
        ] + kv_specs,
        out_specs=pl.BlockSpec((tq, hw), lambda b, h, qi: (b * nq + qi, h)),
        out_shape=jax.ShapeDtypeStruct((B * nq * tq, DIFF_WIDTH), BF16),
        scratch_shapes=[pltpu.VMEM((hw + HALO, keys), BF16), pltpu.VMEM((1, 2 * tq), F32),
                        pltpu.VMEM((hw + HALO, 2 * tq), F32)],
        compiler_params=_cparams("parallel", "parallel", "arbitrary"),
        name="diff_attn" if with_lat else "diff_attn_ctx",
    )(lam_p, nw, P, *([P] * len(kv_specs)))


def _win_kernel(sink_ref, q_ref, kp_ref, kc_ref, kn_ref, kx_ref, o_ref, *, nb, N, M):
    i = pl.program_id(1)
    wb, kw = WIN_BLOCK, WIN_KV_WIDTH
    q = q_ref[...]
    kcat = jnp.concatenate([kp_ref[:, :kw], kc_ref[:, :kw], kn_ref[:, :kw], kx_ref[:, :kw]], axis=0)
    vcat = jnp.concatenate([kp_ref[:, kw:], kc_ref[:, kw:], kn_ref[:, kw:], kx_ref[:, kw:]], axis=0)
    nk = 3 * wb + M
    rows = WIN_G * wb
    row = lax.broadcasted_iota(jnp.int32, (rows, nk), 0) % wb
    col = lax.broadcasted_iota(jnp.int32, (rows, nk), 1)
    kpos = (i - 1) * wb + col
    n_band = jnp.where(i < nb, N, 0)
    band_ok = (jnp.abs(col - wb - row) <= WINDOW) & (kpos >= 0) & (kpos < n_band)
    valid = band_ok | (col >= 3 * wb)
    lane = lax.broadcasted_iota(jnp.int32, (wb, kw), 1) // WIN_DH
    lane_o = lax.broadcasted_iota(jnp.int32, (rows, kw), 1) // WIN_DH
    grow = lax.broadcasted_iota(jnp.int32, (rows, 1), 0) // wb
    zero = jnp.zeros((wb, kw), BF16)
    acc = jnp.zeros((rows, kw), F32)

    def scores(kv):
        qs = jnp.concatenate([jnp.where(lane == kv, q[:, g * kw:(g + 1) * kw], zero) for g in range(WIN_G)], axis=0)
        return lax.dot_general(qs, kcat, (((1,), (1,)), ((), ())), preferred_element_type=F32)

    pending = [scores(kv) for kv in range(WIN_LOOKAHEAD)]
    for kv in range(WIN_KV_HEADS):
        if kv + WIN_LOOKAHEAD < WIN_KV_HEADS:
            pending.append(scores(kv + WIN_LOOKAHEAD))
        s = jnp.where(valid, pending.pop(0), NEG_BIG)
        sink = jnp.zeros((rows, 1), F32)
        for g in range(WIN_G):
            sink = jnp.where(grow == g, sink_ref[kv * WIN_G + g] * LOG2E, sink)
        m = jnp.maximum(jnp.max(s, axis=1, keepdims=True), sink)
        p = jnp.exp2(s - m)
        l = jnp.sum(p, axis=1, keepdims=True) + jnp.exp2(sink - m)
        o = jnp.dot(p.astype(BF16), vcat, preferred_element_type=F32) / l
        acc = jnp.where(lane_o == kv, o, acc)
    for g in range(WIN_G):
        o_ref[:, g * kw:(g + 1) * kw] = acc[g * wb:(g + 1) * wb].astype(BF16)


def _win_call(P, sink, *, B, N, M, need_ctx):
    wb = WIN_BLOCK
    nb, nbc = N // wb, (M // wb if need_ctx else 0)
    R = B * N + (B * M if need_ctx else 0)
    cq, ckv = COL_WQ // PW, COL_WKV // (2 * WIN_KV_WIDTH)

    def qrow(b, i):
        return jnp.where(i < nb, b * nb + i, B * nb + b * nbc + (i - nb))

    def band(b, i, d):
        return b * nb + jnp.clip(i + d, 0, nb - 1)

    kvw = 2 * WIN_KV_WIDTH
    kern = functools.partial(_win_kernel, nb=nb, N=N, M=M)
    grid_spec = pltpu.PrefetchScalarGridSpec(
        num_scalar_prefetch=1,
        grid=(B, nb + nbc),
        in_specs=[
            pl.BlockSpec((wb, PW), lambda b, i, s: (qrow(b, i), cq)),
            pl.BlockSpec((wb, kvw), lambda b, i, s: (band(b, i, -1), ckv)),
            pl.BlockSpec((wb, kvw), lambda b, i, s: (band(b, i, 0), ckv)),
            pl.BlockSpec((wb, kvw), lambda b, i, s: (band(b, i, 1), ckv)),
            pl.BlockSpec((M, kvw), lambda b, i, s: (B * N // M + b, ckv)),
        ],
        out_specs=pl.BlockSpec((wb, PW), lambda b, i, s: (qrow(b, i), 0)),
    )
    return pl.pallas_call(
        kern,
        grid_spec=grid_spec,
        out_shape=jax.ShapeDtypeStruct((R, WIN_WIDTH), BF16),
        compiler_params=_cparams("parallel", "arbitrary"),
        name="win_attn",
    )(sink, P, P, P, P, P)


def _seq_edges(i, R_lat, N, M):
    r0 = i * CONV_ROWS
    lat = r0 < R_lat
    first = jnp.where(lat, r0 % N == 0, (r0 - R_lat) % M == 0)
    last = jnp.where(lat, (r0 + CONV_ROWS) % N == 0, (r0 + CONV_ROWS - R_lat) % M == 0)
    return first, last


def _conv_taps(buf, main_ref, prev_ref, next_ref, w_ref, b_ref, first, last, k):
    buf[0:HALO] = jnp.where(first, 0.0, prev_ref[...].astype(F32))
    buf[HALO:HALO + CONV_ROWS] = main_ref[...].astype(F32)
    buf[HALO + CONV_ROWS:] = jnp.where(last, 0.0, next_ref[...].astype(F32))
    out = b_ref[...]
    for j in range(k):
        out = out + buf[pl.ds(HALO - k // 2 + j, CONV_ROWS), :] * w_ref[j:j + 1, :]
    return out


def _ssdconv_kernel(m_ref, p_ref, n_ref, w_ref, b_ref, o_ref, buf, *, R_lat, N, M):
    first, last = _seq_edges(pl.program_id(0), R_lat, N, M)
    o_ref[...] = _silu(_conv_taps(buf, m_ref, p_ref, n_ref, w_ref, b_ref, first, last, SSD_CONV_W)).astype(BF16)


def _ssdconv_call(P, w, b, *, B, N, M):
    T = P.shape[0]
    cr, W = CONV_ROWS, SSD_XBC
    hb = cr // HALO
    cb = COL_XBC // W
    nh = T // HALO
    kern = functools.partial(_ssdconv_kernel, R_lat=B * N, N=N, M=M)
    return pl.pallas_call(
        kern,
        grid=(T // cr,),
        in_specs=[
            pl.BlockSpec((cr, W), lambda i: (i, cb)),
            pl.BlockSpec((HALO, W), lambda i: (jnp.maximum(i * hb - 1, 0), cb)),
            pl.BlockSpec((HALO, W), lambda i: (jnp.minimum((i + 1) * hb, nh - 1), cb)),
            pl.BlockSpec((SSD_CONV_W, W), lambda i: (0, 0)),
            pl.BlockSpec((1, W), lambda i: (0, 0)),
        ],
        out_specs=pl.BlockSpec((cr, W), lambda i: (i, 0)),
        out_shape=jax.ShapeDtypeStruct((T, W), BF16),
        scratch_shapes=[pltpu.VMEM((cr + 2 * HALO, W), F32)],
        compiler_params=_cparams("parallel"),
        name="ssd_conv",
    )(P, P, P, w, b)


def _cumsum_rows(a):
    n = a.shape[0]
    tri = (lax.broadcasted_iota(jnp.int32, (n, n), 0) >= lax.broadcasted_iota(jnp.int32, (n, n), 1)).astype(BF16)
    hi = a.astype(BF16)
    r1 = a - hi.astype(F32)
    mid = r1.astype(BF16)
    lo = (r1 - mid.astype(F32)).astype(BF16)
    dot = lambda v: jnp.dot(tri, v, preferred_element_type=F32)
    return dot(hi) + dot(mid) + dot(lo)


def _ssd_prep(dt_ref, bias_ref, alog_ref, rows):
    x = dt_ref[rows, :].astype(F32) + bias_ref[...]
    dt = jnp.maximum(x, 0.0) + jnp.log(1.0 + jnp.exp(-jnp.abs(x)))
    ad = dt * (-jnp.exp(alog_ref[...]))
    return dt, ad, _cumsum_rows(ad)


def _ssd_scales(prep, e_ref, *, reverse):
    dt, ad, acum = prep
    ch = SSD_CHUNK
    total = acum[ch - 1:ch, :]
    if reverse:
        cvec = acum - ad
        e_off, e_ws = jnp.exp(total - cvec), jnp.exp(cvec)
    else:
        e_off, e_ws = jnp.exp(acum), jnp.exp(total - acum)
    tot_rows = jnp.broadcast_to(jnp.exp(total), (HALO, LANES))
    return jnp.dot(jnp.concatenate([dt * e_ws, e_off, tot_rows], axis=0).astype(BF16), e_ref[...],
                   preferred_element_type=F32)


def _ssd_group_refs(g, u_ref, rows):
    hpg = SSD_HEADS // SSD_GROUPS
    lanes = slice(g * hpg * SSD_HEAD_DIM, (g + 1) * hpg * SSD_HEAD_DIM)
    bg = u_ref[rows, SSD_INNER + g * SSD_STATE:SSD_INNER + (g + 1) * SSD_STATE]
    cg = u_ref[rows, SSD_INNER + (SSD_GROUPS + g) * SSD_STATE:SSD_INNER + (SSD_GROUPS + g + 1) * SSD_STATE]
    return lanes, bg, cg, u_ref[rows, lanes]


def _ssd_state_group(g, scales, u_ref, s_ref, rows):
    ch = SSD_CHUNK
    lanes, bg, cg, xs = _ssd_group_refs(g, u_ref, rows)
    sg = s_ref[:, lanes]
    yoff = jnp.dot(cg, sg.astype(BF16), preferred_element_type=F32) * scales[ch:2 * ch, lanes]
    xw = (xs.astype(F32) * scales[0:ch, lanes]).astype(BF16)
    s_ref[:, lanes] = (sg * scales[2 * ch:2 * ch + 1, lanes]
                       + lax.dot_general(bg, xw, (((0,), (0,)), ((), ())), preferred_element_type=F32))
    return yoff


def _ssd_diag_group(g, decay, u_ref, rows):
    cf, cfT, cb_, cbT, dtT = decay
    ch, hp = SSD_CHUNK, SSD_HEAD_DIM
    hpg = SSD_HEADS // SSD_GROUPS
    li = lax.broadcasted_iota(jnp.int32, (ch, ch), 0)
    si = lax.broadcasted_iota(jnp.int32, (ch, ch), 1)
    hsel = lax.broadcasted_iota(jnp.int32, (ch, hpg * hp), 1) // hp
    _, bg, cg, xs = _ssd_group_refs(g, u_ref, rows)
    cb = lax.dot_general(cg, bg, (((1,), (1,)), ((), ())), preferred_element_type=F32)
    ydiag = jnp.zeros((ch, hpg * hp), F32)
    for r in range(hpg):
        jf = g * hpg + r
        jb = SSD_HEADS + jf
        bc = lambda v: jnp.broadcast_to(v, (ch, ch))
        fwd = jnp.exp(jnp.where(li >= si, bc(cf[:, jf:jf + 1]) - bc(cfT[jf:jf + 1, :]), NEG_BIG)) * dtT[jf:jf + 1, :]
        bwd = jnp.exp(jnp.where(si >= li, bc(cbT[jb:jb + 1, :]) - bc(cb_[:, jb:jb + 1]), NEG_BIG)) * dtT[jb:jb + 1, :]
        yh = jnp.dot((cb * (fwd + bwd)).astype(BF16), xs, preferred_element_type=F32)
        ydiag = jnp.where(hsel == r, yh, ydiag)
    return ydiag


def _ssd_kernel(uf_ref, dtf_ref, ub_ref, dtb_ref, bias_ref, alog_ref, e_ref, yf_ref, yb_ref, sf, sb):
    @pl.when(pl.program_id(1) == 0)
    def _():
        sf[...] = jnp.zeros_like(sf)
        sb[...] = jnp.zeros_like(sb)

    ch, n = SSD_CHUNK, SSD_STEP_CHUNKS
    f_rows = [slice(k * ch, (k + 1) * ch) for k in range(n)]
    b_rows = f_rows[::-1]
    for k in range(n):
        pf = _ssd_prep(dtf_ref, bias_ref, alog_ref, f_rows[k])
        pb = _ssd_prep(dtb_ref, bias_ref, alog_ref, b_rows[k])
        sc_f = _ssd_scales(pf, e_ref.at[0], reverse=False)
        sc_b = _ssd_scales(pb, e_ref.at[1], reverse=True)
        dt, ad, acum = pf
        excl = acum - ad
        decay = (acum, acum.T, excl, excl.T, dt.T)
        for g in range(SSD_GROUPS):
            lanes = _ssd_group_refs(g, uf_ref, f_rows[k])[0]
            yoff_f = _ssd_state_group(g, sc_f, uf_ref, sf, f_rows[k])
            yoff_b = _ssd_state_group(g, sc_b, ub_ref, sb, b_rows[k])
            yb_ref[b_rows[k], lanes] = yoff_b.astype(BF16)
            yf_ref[f_rows[k], lanes] = (_ssd_diag_group(g, decay, uf_ref, f_rows[k]) + yoff_f).astype(BF16)


def _ssd_call(U, P, bias, alog, spread, *, B, N, M):
    T = U.shape[0]
    ch = SSD_CHUNK * SSD_STEP_CHUNKS
    ncl, ncc = N // ch, M // ch
    cdt = COL_DT // LANES

    def fwd(b, t):
        return jnp.where(t < ncc, B * ncl + b * ncc + t, b * ncl + (t - ncc))

    def bwd(b, t):
        return jnp.where(t < ncc, B * ncl + b * ncc + (ncc - 1 - t), b * ncl + (ncl - 1 - (t - ncc)))

    return pl.pallas_call(
        _ssd_kernel,
        grid=(B, ncl + ncc),
        in_specs=[
            pl.BlockSpec((ch, SSD_XBC), lambda b, t: (fwd(b, t), 0)),
            pl.BlockSpec((ch, LANES), lambda b, t: (fwd(b, t), cdt)),
            pl.BlockSpec((ch, SSD_XBC), lambda b, t: (bwd(b, t), 0)),
            pl.BlockSpec((ch, LANES), lambda b, t: (bwd(b, t), cdt)),
            pl.BlockSpec((1, LANES), lambda b, t: (0, 0)),
            pl.BlockSpec((1, LANES), lambda b, t: (0, 0)),
            pl.BlockSpec((2, LANES, SSD_INNER), lambda b, t: (0, 0, 0)),
        ],
        out_specs=[
            pl.BlockSpec((ch, SSD_INNER), lambda b, t: (fwd(b, t), 0)),
            pl.BlockSpec((ch, SSD_INNER), lambda b, t: (bwd(b, t), 0)),
        ],
        out_shape=[jax.ShapeDtypeStruct((T, SSD_INNER), BF16)] * 2,
        scratch_shapes=[pltpu.VMEM((SSD_STATE, SSD_INNER), F32)] * 2,
        compiler_params=_cparams("parallel", "arbitrary"),
        name="ssd_scan",
    )(U, P, U, P, bias, alog, spread)


def _merge_kernel(yd_ref, yf_ref, yb_ref, xs_ref, z_ref, yw_ref, gd_ref, gs_ref, gw_ref,
                  wd_ref, ws_ref, ww_ref, dsk_ref, sn_ref, o_ref):
    y = yf_ref[...].astype(F32) + yb_ref[...].astype(F32) + dsk_ref[...] * xs_ref[...].astype(F32)
    y = y * _silu(z_ref[...].astype(F32))
    ms = jnp.mean(y * y, axis=-1, keepdims=True)
    s = (y * lax.rsqrt(ms + NORM_EPS) * sn_ref[...]).astype(BF16)
    dot = lambda a, w: jnp.dot(a, w[...], preferred_element_type=F32)
    sig = lambda r: _sigmoid(r[...].astype(F32))
    acc = sig(gd_ref) * dot(yd_ref[...], wd_ref)
    acc = acc + sig(gs_ref) * dot(s, ws_ref)
    acc = acc + sig(gw_ref) * dot(yw_ref[...], ww_ref)
    o_ref[...] = acc.astype(BF16)


def _merge_call(YD, YF, YB, U, P, YW, wd, ws, ww, dsk, sn, *, layer, R, D):
    tm = 256
    W = SSD_INNER
    cg = COL_GL // D
    row = lambda c: pl.BlockSpec((tm, W), lambda i: (i, c))
    gate = lambda c: pl.BlockSpec((tm, D), lambda i: (i, cg + c))
    wspec = pl.BlockSpec((None, W, D), lambda i: (layer, 0, 0))
    vec = pl.BlockSpec((1, W), lambda i: (0, 0))
    return pl.pallas_call(
        _merge_kernel,
        grid=(R // tm,),
        in_specs=[row(0), row(0), row(0), row(0), row(COL_SZ // W), row(0),
                  gate(0), gate(1), gate(2), wspec, wspec, wspec, vec, vec],
        out_specs=pl.BlockSpec((tm, D), lambda i: (i, 0)),
        out_shape=jax.ShapeDtypeStruct((R, D), BF16),
        compiler_params=_cparams("parallel"),
        name="branch_merge",
    )(YD, YF, YB, U, P, YW, P, P, P, wd, ws, ww, dsk, sn)


def _rope_tables(N, rows_ctx):
    hd = DIFF_DH
    pos = jnp.arange(N)
    row = (pos // GRID_W).astype(F32)
    colp = (pos % GRID_W).astype(F32)
    axis_dim = hd // 2
    inv = ROPE_BASE ** (-jnp.arange(0, axis_dim, 2, dtype=F32) / axis_dim)
    ang = jnp.concatenate([row[:, None] * inv, colp[:, None] * inv], axis=-1)
    lane = jnp.arange(LANES)
    cos = jnp.cos(ang)[:, lane % (hd // 2)]
    sin = jnp.sin(ang)[:, lane % (hd // 2)]
    first = (lane % hd) < hd // 2
    lat = jnp.stack([cos, jnp.where(first, -sin, 0.0), jnp.where(first, 0.0, sin)])
    ctx = jnp.stack([jnp.ones((rows_ctx, LANES), F32), jnp.zeros((rows_ctx, LANES), F32),
                     jnp.zeros((rows_ctx, LANES), F32)])
    k_tab = jnp.concatenate([lat, ctx], axis=1)
    q_tab = k_tab * (DIFF_DH ** -0.5 * LOG2E)
    return jnp.stack([q_tab, k_tab])


def _edge_table(B, N, M):
    pos = jnp.concatenate([jnp.tile(jnp.arange(N), B), jnp.tile(jnp.arange(M), B)])
    last = jnp.concatenate([jnp.full((B * N,), N - 1), jnp.full((B * M,), M - 1)])
    t = jnp.stack([pos != 0, pos != last]).astype(F32)
    return jnp.broadcast_to(t[:, :, None], t.shape + (LANES,))


def _layout_w_in(w):
    L, D = w.shape[:2]
    sizes = (DIFF_WIDTH, DIFF_WIDTH, DIFF_WIDTH, SSD_INNER, SSD_XBC, 2 * SSD_HEADS,
             WIN_WIDTH, WIN_KV_WIDTH, WIN_KV_WIDTH, 3 * D)
    parts, s = [], 0
    for z in sizes:
        parts.append(w[..., s:s + z])
        s += z
    dq, dk, dv, sz, sxbc, sdt, wq, wk, wv, gl = parts
    wq = wq.reshape(L, D, WIN_KV_HEADS, WIN_G, WIN_DH).transpose(0, 1, 3, 2, 4).reshape(L, D, WIN_WIDTH)
    out = jnp.zeros((L, D, P_WIDTH), BF16)
    for off, piece in ((COL_DQ, dq), (COL_DK, dk), (COL_WQ, wq), (COL_WKV, wk), (COL_WKV + WIN_KV_WIDTH, wv),
                       (COL_DT, sdt), (COL_DV, dv), (COL_SZ, sz), (COL_XBC, sxbc), (COL_GL, gl)):
        out = lax.dynamic_update_slice(out, piece.astype(BF16), (0, 0, off))
    return out


def kernel(x, c, ctx, c_ctx, w_ada, b_ada, norm_g, w_in, diff_lambda, diff_norm, ssd_conv_w, ssd_conv_b,
           ssd_a_log, ssd_dt_bias, ssd_d, ssd_norm, win_sink, w_br_diff, w_br_ssd, w_br_win, w_out,
           ffn_w_up, ffn_conv_w, ffn_conv_b, ffn_w_down):
    B, N, D = x.shape
    M = ctx.shape[1]
    L = w_ada.shape[0]
    assert D == 2 * PW and P_WIDTH == COL_GL + 3 * D
    assert N % CONV_ROWS == 0 and M % CONV_ROWS == 0 and (B * N) % M == 0 and B + 1 <= 16
    assert CONV_ROWS % (SSD_CHUNK * SSD_STEP_CHUNKS) == 0
    T = B * N + B * M

    Xa, Xb = x.reshape(B * N, D), ctx.reshape(B * M, D)
    cv = jnp.zeros((16, D), F32).at[:B].set(c).at[B].set(c_ctx)
    mod = _mod_call(cv, w_ada, b_ada)

    tm_in = _pick((N, B * M), 1024, 512, 256)
    tab = _rope_tables(N, tm_in)
    kinds = jnp.asarray(COL_KINDS, jnp.int32)
    edge = _edge_table(B, N, M)
    head_of_lane = jnp.arange(SSD_INNER) // SSD_HEAD_DIM
    spread = jnp.stack([(jnp.arange(LANES)[:, None] == head_of_lane[None, :] + d * SSD_HEADS)
                        for d in range(2)]).astype(BF16)
    pad_lanes = LANES - 2 * SSD_HEADS

    W_in = _layout_w_in(w_in)
    W_bd, W_bs, W_out = w_br_diff.astype(BF16), w_br_ssd.astype(BF16), w_out.astype(BF16)
    W_bw = (w_br_win.reshape(L, WIN_KV_HEADS, WIN_G, WIN_DH, D).transpose(0, 2, 1, 3, 4)
            .reshape(L, WIN_WIDTH, D).astype(BF16))
    W_up, W_down = ffn_w_up.astype(BF16), ffn_w_down.astype(BF16)

    for l in range(L):
        need_ctx = l < L - 1
        lam_init = 0.8 - 0.6 * math.exp(-0.3 * l)
        R = T if need_ctx else B * N
        mod3 = mod[l].reshape(16, 1, 6 * D)
        kw0 = dict(B=B, N=N, M=M)
        kw = dict(layer=l, **kw0)

        P = _inproj_call(Xa, Xb, mod3, norm_g[l, 0:1], W_in, tab, kinds, sc_part=1, sh_part=0, **kw)

        dargs = (P, diff_lambda[l], diff_norm[l].reshape(1, -1))
        YD = _diff_call(*dargs, with_lat=True, lam_init=lam_init, **kw0)
        if need_ctx:
            YD = jnp.concatenate([YD, _diff_call(*dargs, with_lat=False, lam_init=lam_init, **kw0)], axis=0)
        YW = _win_call(P, win_sink[l], need_ctx=need_ctx, **kw0)

        U = _ssdconv_call(P, ssd_conv_w[l], ssd_conv_b[l].reshape(1, -1), **kw0)
        bias = jnp.pad(ssd_dt_bias[l].reshape(1, -1), ((0, 0), (0, pad_lanes)))
        alog = jnp.pad(ssd_a_log[l].reshape(1, -1), ((0, 0), (0, pad_lanes)))
        YF, YB = _ssd_call(U, P, bias, alog, spread, **kw0)

        dsk = jnp.repeat(ssd_d[l], SSD_HEAD_DIM).reshape(1, -1)
        Z = _merge_call(YD, YF, YB, U, P, YW, W_bd, W_bs, W_bw, dsk, ssd_norm[l].reshape(1, -1), layer=l, R=R, D=D)
        X1 = _mnr_call(Z, W_out, Xa, Xb if l == 0 else None, mod3, norm_g[l, 1:2], R=R, gate_part=2, tk=W_out.shape[1],
                       tm=_pick((N, B * M), 512, 256), name="out_proj", **kw)

        A = _ffnup_call(X1, mod3, norm_g[l, 2:3], W_up, ffn_conv_w[l], ffn_conv_b[l].reshape(1, -1), edge,
                        R=R, sc_part=4, sh_part=3, **kw)
        Xa = Xb = _mnr_call(A, W_down, X1, None, mod3, norm_g[l, 3:4], R=R, gate_part=5,
                            tk=W_down.shape[1], tm=CONV_ROWS, name="ffn_down", **kw)

    return Xa[:B * N].reshape(B, N, D)
```
